```python
import math
import jax, jax.numpy as jnp
from jax import lax
import numpy as np

D_MODEL = 1024
BATCH = 2
SEQ = 8192
DEPTH = 2
DEC_BATCH = 32
DEC_SEQ = 1
PAST_LEN = 8192
PAGE_SIZE = 128

HEAD_DIM = 64
ROPE_THETA = 10000.0
NORM_EPS = 1e-6
D_FF = 2816
Q_BLOCK = 128
NSA_HEADS = 8
NSA_KV_HEADS = 2
NSA_GROUP = NSA_HEADS // NSA_KV_HEADS
CMP_LEN = 32
CMP_STRIDE = 16
SLC_BLOCK = 64
N_SEL = 16
NSA_WINDOW = 512
FORCED_BLOCK_SCORE = 1.0e4
DIFF_HEADS = 4
DIFF_VDIM = 2 * HEAD_DIM
DIFF_NORM_EPS = 1e-5
DIL_GROUPS = ((128, 1), (512, 4), (2048, 16))
DIL_HEADS = 8
N_EVEN = (DEPTH + 1) // 2
N_ODD = DEPTH // 2
AB_SIZES = (NSA_HEADS * HEAD_DIM, 6 * NSA_KV_HEADS * HEAD_DIM, 3 * NSA_HEADS,
            DIFF_HEADS * 2 * HEAD_DIM, DIFF_HEADS * 2 * HEAD_DIM, DIFF_HEADS * DIFF_VDIM)
AB_COLS = sum(AB_SIZES)
AB_OUT = NSA_HEADS * HEAD_DIM + DIFF_HEADS * DIFF_VDIM
C_COLS = len(DIL_GROUPS) * 3 * DIL_HEADS * HEAD_DIM
C_OUT = DIL_HEADS * HEAD_DIM

kernel_name = 'nsa_diff_dilated_hybrid_step'


def rmsnorm(x, g, eps=NORM_EPS):
    xf = x.astype(jnp.float32)
    y = xf * lax.rsqrt(jnp.mean(xf * xf, axis=-1, keepdims=True) + eps)
    return (y * g.astype(jnp.float32)).astype(x.dtype)


def rope(x, pos):
    half = x.shape[-1] // 2
    inv = ROPE_THETA ** (-jnp.arange(half, dtype=jnp.float32) / half)
    ang = pos.astype(jnp.float32)[:, None] * inv[None, :]
    ang = ang.reshape((pos.shape[0],) + (1,) * (x.ndim - 3) + (half,))
    c, s = jnp.cos(ang), jnp.sin(ang)
    xf = x.astype(jnp.float32)
    x1, x2 = xf[..., :half], xf[..., half:]
    return jnp.concatenate([x1 * c - x2 * s, x2 * c + x1 * s], axis=-1).astype(x.dtype)


def swiglu(x, w_in, w_out):
    gate, up = jnp.split(x @ w_in, 2, axis=-1)
    return (jax.nn.silu(gate) * up) @ w_out


def ffn_half(h, g_pre, g_post, w_in, w_out):
    return h + 0.5 * rmsnorm(swiglu(rmsnorm(h, g_pre), w_in, w_out), g_post)


def masked_softmax(s, mask):
    s = jnp.where(mask, s.astype(jnp.float32), -jnp.inf)
    m = jnp.max(s, axis=-1, keepdims=True)
    m = jnp.where(jnp.isfinite(m), m, 0.0)
    e = jnp.exp(s - m)
    den = jnp.sum(e, axis=-1, keepdims=True)
    p = e / jnp.where(den > 0, den, 1.0)
    return p, (jnp.log(den) + m)[..., 0]


def gather_pages(pool, page_table):
    b, n = page_table.shape
    rows = jnp.take(pool, page_table, axis=0)
    return rows.reshape((b, n * pool.shape[1]) + pool.shape[2:])


def sweep(fn, n_q):
    if n_q > Q_BLOCK and n_q % Q_BLOCK == 0:
        outs = lax.map(lambda i: fn(i * Q_BLOCK, Q_BLOCK), jnp.arange(n_q // Q_BLOCK))
        return jnp.moveaxis(outs, 0, 1).reshape((outs.shape[1], n_q) + outs.shape[3:])
    return fn(0, n_q)


def nsa_compress(k, w, pe):
    B, L = k.shape[:2]
    n_sub = -(-L // CMP_STRIDE)
    k = jnp.pad(k, ((0, 0), (0, n_sub * CMP_STRIDE - L), (0, 0), (0, 0)))
    sub = k.reshape(B, n_sub, CMP_STRIDE, NSA_KV_HEADS, HEAD_DIM)
    r = CMP_LEN // CMP_STRIDE
    nb = n_sub - r + 1
    blocks = jnp.concatenate([sub[:, i:i + nb] for i in range(r)], axis=2)
    blocks = blocks + pe[None, None, :, None, :].astype(k.dtype)
    comp = jnp.einsum('bnlkd,lde->bnke', blocks, w)
    ends = jnp.arange(nb) * CMP_STRIDE + CMP_LEN - 1
    return comp, ends


def nsa_select_blocks(k):
    B, L = k.shape[:2]
    n_sel = -(-L // SLC_BLOCK)
    k = jnp.pad(k, ((0, 0), (0, n_sel * SLC_BLOCK - L), (0, 0), (0, 0)))
    return k.reshape(B, n_sel, SLC_BLOCK, NSA_KV_HEADS, HEAD_DIM).transpose(0, 3, 1, 2, 4)


def nsa_attend(q, gate, qpos, kcmp, vcmp, cmp_end, kslc, vslc, win, wpos):
    B, Q = q.shape[:2]
    q = q.reshape(B, Q, NSA_KV_HEADS, NSA_GROUP, HEAD_DIM)
    scale = HEAD_DIM ** -0.5
    s = jnp.einsum('bqkgd,bnkd->bqkgn', q, kcmp) * scale
    p_cmp, _ = masked_softmax(s, (cmp_end[None, :] <= qpos[:, None])[None, :, None, None, :])
    o_cmp = jnp.einsum('bqkgn,bnkd->bqkgd', p_cmp.astype(vcmp.dtype), vcmp)
    n_sel = kslc.shape[2]
    r = SLC_BLOCK // CMP_STRIDE
    imp = p_cmp.sum(axis=3)
    imp = jnp.pad(imp, ((0, 0), (0, 0), (0, 0), (0, n_sel * r - imp.shape[-1])))
    imp = imp.reshape(B, Q, NSA_KV_HEADS, n_sel, r).sum(-1)
    cur = qpos // SLC_BLOCK
    blk = jnp.arange(n_sel)
    forced = ((blk[None, :] == 0) | (blk[None, :] == cur[:, None]))[None, :, None, :]
    causal = (blk[None, :] <= cur[:, None])[None, :, None, :]
    score = jnp.where(causal, jnp.where(forced, FORCED_BLOCK_SCORE, imp), -jnp.inf)
    top_v, top_i = lax.top_k(score, min(N_SEL, n_sel))
    bi = jnp.arange(B)[:, None, None, None]
    hi = jnp.arange(NSA_KV_HEADS)[None, None, :, None]
    ksel = kslc[bi, hi, top_i]
    vsel = vslc[bi, hi, top_i]
    kpos = top_i[..., None] * SLC_BLOCK + jnp.arange(SLC_BLOCK)
    msel = (kpos <= qpos[None, :, None, None, None]) & jnp.isfinite(top_v)[..., None]
    nk = top_i.shape[-1] * SLC_BLOCK
    ksel = ksel.reshape(B, Q, NSA_KV_HEADS, nk, HEAD_DIM)
    vsel = vsel.reshape(B, Q, NSA_KV_HEADS, nk, HEAD_DIM)
    msel = msel.reshape(B, Q, NSA_KV_HEADS, nk)
    s = jnp.einsum('bqkgd,bqknd->bqkgn', q, ksel) * scale
    p, _ = masked_softmax(s, msel[:, :, :, None, :])
    o_sel = jnp.einsum('bqkgn,bqknd->bqkgd', p.astype(vsel.dtype), vsel)
    kw, vw = win[:, :, 0], win[:, :, 1]
    dist = qpos[:, None] - wpos[None, :]
    mw = (dist >= 0) & (dist < NSA_WINDOW) & (wpos >= 0)[None, :]
    s = jnp.einsum('bqkgd,blkd->bqkgl', q, kw) * scale
    p, _ = masked_softmax(s, mw[None, :, None, None, :])
    o_win = jnp.einsum('bqkgl,blkd->bqkgd', p.astype(vw.dtype), vw)
    g = jax.nn.sigmoid(gate.astype(jnp.float32)).reshape(B, Q, NSA_KV_HEADS, NSA_GROUP, 3)
    out = g[..., 0:1] * o_cmp + g[..., 1:2] * o_sel + g[..., 2:3] * o_win
    return out.astype(q.dtype).reshape(B, Q, NSA_HEADS * HEAD_DIM)


def diff_attend(q, qpos, k, v, kpos, lam, norm_g, lam_init):
    B, Q = q.shape[:2]
    s = jnp.einsum('bqhcd,blhcd->bhcql', q, k) * HEAD_DIM ** -0.5
    p, _ = masked_softmax(s, (kpos[None, :] <= qpos[:, None])[None, None, None])
    a = p[:, :, 0] - lam * p[:, :, 1]
    o = jnp.einsum('bhql,blhe->bqhe', a.astype(v.dtype), v)
    o = rmsnorm(o, norm_g, DIFF_NORM_EPS) * (1.0 - lam_init)
    return o.reshape(B, Q, DIFF_HEADS * DIFF_VDIM)


def ab_project(hn, pos, w_in):
    B, T, _ = hn.shape
    idx = np.cumsum(AB_SIZES)[:-1].tolist()
    qa, kvb, gl, qd, kd, vd = jnp.split(hn @ w_in, idx, axis=-1)
    qa = rope(qa.reshape(B, T, NSA_HEADS, HEAD_DIM), pos)
    kvb = kvb.reshape(B, T, 6, NSA_KV_HEADS, HEAD_DIM)
    k3 = rope(kvb[:, :, 0::2], pos)
    v3 = kvb[:, :, 1::2]
    gate = gl.reshape(B, T, NSA_HEADS, 3)
    qd = rope(qd.reshape(B, T, DIFF_HEADS, 2, HEAD_DIM), pos)
    kd = rope(kd.reshape(B, T, DIFF_HEADS, 2, HEAD_DIM), pos)
    vd = vd.reshape(B, T, DIFF_HEADS, DIFF_VDIM)
    rows_nsa = jnp.stack([k3[:, :, 0], v3[:, :, 0], k3[:, :, 1], v3[:, :, 1]], axis=2)
    rows_win = jnp.stack([k3[:, :, 2], v3[:, :, 2]], axis=2)
    rows_diff = jnp.stack([kd.reshape(B, T, DIFF_HEADS, 2 * HEAD_DIM), vd], axis=2)
    return qa, gate, qd, rows_nsa, rows_win, rows_diff


def ab_mix(qa, gate, qd, q_pos0, nsa_full, win_full, win_offset, diff_full,
           w_cmp, pe_cmp, lam_vec, dn_g, lam_init):
    B, n_q = qa.shape[:2]
    L = nsa_full.shape[1]
    kcmp, cmp_end = nsa_compress(nsa_full[:, :, 0], w_cmp[0], pe_cmp[0])
    vcmp, _ = nsa_compress(nsa_full[:, :, 1], w_cmp[1], pe_cmp[1])
    kslc = nsa_select_blocks(nsa_full[:, :, 2])
    vslc = nsa_select_blocks(nsa_full[:, :, 3])
    win_pad = jnp.pad(win_full, ((0, 0), (NSA_WINDOW, 0), (0, 0), (0, 0), (0, 0)))
    kd = diff_full[:, :, 0].reshape(B, L, DIFF_HEADS, 2, HEAD_DIM)
    vd = diff_full[:, :, 1]
    kpos = jnp.arange(L)
    lv = lam_vec.astype(jnp.float32)
    lam = jnp.exp(jnp.sum(lv[0] * lv[1])) - jnp.exp(jnp.sum(lv[2] * lv[3])) + lam_init

    def fn(start, size):
        qpos = q_pos0 + start + jnp.arange(size)
        sl = lambda x: lax.dynamic_slice_in_dim(x, start, size, axis=1)
        win = lax.dynamic_slice_in_dim(win_pad, start + win_offset, size + NSA_WINDOW, axis=1)
        wpos = q_pos0 + start - NSA_WINDOW + jnp.arange(size + NSA_WINDOW)
        o_nsa = nsa_attend(sl(qa), sl(gate), qpos, kcmp, vcmp, cmp_end, kslc, vslc, win, wpos)
        o_diff = diff_attend(sl(qd), qpos, kd, vd, kpos, lam, dn_g, lam_init)
        return jnp.concatenate([o_nsa, o_diff], axis=-1)

    return sweep(fn, n_q)


def dil_project(hn, pos, w_in):
    B, T, _ = hn.shape
    proj = (hn @ w_in).reshape(B, T, len(DIL_GROUPS), 3, DIL_HEADS, HEAD_DIM)
    return rope(proj[:, :, :, 0], pos), rope(proj[:, :, :, 1], pos), proj[:, :, :, 2]


def dilated_band(q, k, v, dilation, band):
    B, T, H, Dh = q.shape
    unit = dilation * band
    Tp = -(-T // unit) * unit
    n = Tp // dilation
    nb = n // band

    def to_sub(x):
        x = jnp.pad(x, ((0, 0), (0, Tp - T), (0, 0), (0, 0)))
        return x.reshape(B, n, dilation, H, Dh).transpose(0, 2, 1, 3, 4).reshape(B, dilation, nb, band, H, Dh)

    def with_prev(x):
        prev = jnp.pad(x, ((0, 0), (0, 0), (1, 0), (0, 0), (0, 0), (0, 0)))[:, :, :-1]
        return jnp.concatenate([prev, x], axis=3)

    qs = to_sub(q)
    k2, v2 = with_prev(to_sub(k)), with_prev(to_sub(v))
    s = jnp.einsum('brnqhd,brnkhd->brnhqk', qs, k2) * Dh ** -0.5
    qi = jnp.arange(band)[:, None] + band
    kj = jnp.arange(2 * band)[None, :]
    rel = qi - kj
    band_ok = (rel >= 0) & (rel <= band)
    first = (jnp.arange(nb) == 0)[:, None, None] & (kj < band)[None]
    mask = band_ok[None] & ~first
    p, lse = masked_softmax(s, mask[None, None, :, None])
    o = jnp.einsum('brnhqk,brnkhd->brnqhd', p.astype(v.dtype), v2)
    o = o.reshape(B, dilation, n, H, Dh).transpose(0, 2, 1, 3, 4).reshape(B, Tp, H, Dh)[:, :T]
    lse = lse.transpose(0, 1, 2, 4, 3).reshape(B, dilation, n, H).transpose(0, 2, 1, 3).reshape(B, Tp, H)[:, :T]
    return o, lse


def dilated_gather(q, k_seq, v_seq, buf_len, dilation, band):
    S = q.shape[1]
    idx = buf_len + jnp.arange(S)[:, None] - dilation * jnp.arange(band + 1)[None, :]
    valid = idx >= 0
    idx = jnp.maximum(idx, 0)
    kg = k_seq[:, idx]
    vg = v_seq[:, idx]
    s = jnp.einsum('bshd,bskhd->bshk', q, kg) * q.shape[-1] ** -0.5
    p, lse = masked_softmax(s, valid[None, :, None, :])
    o = jnp.einsum('bshk,bskhd->bshd', p.astype(v_seq.dtype), vg)
    return o, lse


def dil_combine(outs, lses):
    alpha = jax.nn.softmax(jnp.stack(lses, axis=0).astype(jnp.float32), axis=0)
    o = jnp.sum(alpha[..., None] * jnp.stack(outs, axis=0).astype(jnp.float32), axis=0)
    return o.astype(outs[0].dtype)


def dil_mix_prompt(hn, pos, w_in, w_out):
    B, T, _ = hn.shape
    q, k, v = dil_project(hn, pos, w_in)
    outs, lses, rows = [], [], []
    for gi, (win, dil) in enumerate(DIL_GROUPS):
        o, l = dilated_band(q[:, :, gi], k[:, :, gi], v[:, :, gi], dil, win // dil)
        outs.append(o)
        lses.append(l)
        rows.append(jnp.stack([k[:, :, gi], v[:, :, gi]], axis=2)[:, -min(win, T):])
    return dil_combine(outs, lses).reshape(B, T, C_OUT) @ w_out, rows


def dil_mix_sample(hn, pos, bufs, w_in, w_out):
    B, S, _ = hn.shape
    q, k, v = dil_project(hn, pos, w_in)
    outs, lses, rows = [], [], []
    for gi, (win, dil) in enumerate(DIL_GROUPS):
        buf = bufs[gi]
        wb = buf.shape[1]
        full = jnp.concatenate([buf, jnp.stack([k[:, :, gi], v[:, :, gi]], axis=2)], axis=1)
        o, l = dilated_gather(q[:, :, gi], full[:, :, 0], full[:, :, 1], wb, dil, win // dil)
        outs.append(o)
        lses.append(l)
        rows.append(full[:, -min(win, wb + S):])
    return dil_combine(outs, lses).reshape(B, S, C_OUT) @ w_out, rows


def setup_inputs(seed: int = 0) -> dict:
    key = jax.random.key(seed)
    ks = jax.random.split(key, 20)
    n_pages = PAST_LEN // PAGE_SIZE
    n_pool = (DEC_BATCH * n_pages * 5) // 4
    f32 = jnp.float32
    nrm = lambda k, shape, sc=1.0: jax.random.normal(k, shape, f32) * sc
    page_table = jax.random.permutation(ks[0], n_pool)[:DEC_BATCH * n_pages].reshape(DEC_BATCH, n_pages).astype(jnp.int32)
    return {
        'x_prompt': nrm(ks[1], (BATCH, SEQ, D_MODEL)),
        'x_sample': nrm(ks[2], (DEC_BATCH, DEC_SEQ, D_MODEL)),
        'cache_nsa': nrm(ks[3], (N_EVEN, n_pool, PAGE_SIZE, 4, NSA_KV_HEADS, HEAD_DIM)),
        'cache_diff': nrm(ks[4], (N_EVEN, n_pool, PAGE_SIZE, 2, DIFF_HEADS, 2 * HEAD_DIM)),
        'state_nsa_win': nrm(ks[5], (N_EVEN, DEC_BATCH, min(NSA_WINDOW, PAST_LEN), 2, NSA_KV_HEADS, HEAD_DIM)),
        'state_dil_0': nrm(ks[6], (N_ODD, DEC_BATCH, min(DIL_GROUPS[0][0], PAST_LEN), 2, DIL_HEADS, HEAD_DIM)),
        'state_dil_1': nrm(ks[7], (N_ODD, DEC_BATCH, min(DIL_GROUPS[1][0], PAST_LEN), 2, DIL_HEADS, HEAD_DIM)),
        'state_dil_2': nrm(ks[8], (N_ODD, DEC_BATCH, min(DIL_GROUPS[2][0], PAST_LEN), 2, DIL_HEADS, HEAD_DIM)),
        'page_table': page_table,
        'norm_g': 1.0 + nrm(ks[9], (DEPTH, 6, D_MODEL), 0.02),
        'ffn_w_in': nrm(ks[10], (DEPTH, 2, D_MODEL, 2 * D_FF), D_MODEL ** -0.5),
        'ffn_w_out': nrm(ks[11], (DEPTH, 2, D_FF, D_MODEL), D_FF ** -0.5),
        'w_in_ab': nrm(ks[12], (N_EVEN, D_MODEL, AB_COLS), D_MODEL ** -0.5),
        'w_out_ab': nrm(ks[13], (N_EVEN, AB_OUT, D_MODEL), AB_OUT ** -0.5),
        'nsa_w_cmp': nrm(ks[14], (N_EVEN, 2, CMP_LEN, HEAD_DIM, HEAD_DIM), (CMP_LEN * HEAD_DIM) ** -0.5),
        'nsa_pe_cmp': nrm(ks[15], (N_EVEN, 2, CMP_LEN, HEAD_DIM), 0.1),
        'diff_lambda': nrm(ks[16], (N_EVEN, 4, HEAD_DIM), 0.1),
        'diff_norm_g': 1.0 + nrm(ks[17], (N_EVEN, DIFF_VDIM), 0.02),
        'w_in_c': nrm(ks[18], (N_ODD, D_MODEL, C_COLS), D_MODEL ** -0.5),
        'w_out_c': nrm(ks[19], (N_ODD, C_OUT, D_MODEL), C_OUT ** -0.5),
    }


def reference(x_prompt, x_sample, cache_nsa, cache_diff, state_nsa_win, state_dil_0, state_dil_1,
              state_dil_2, page_table, norm_g, ffn_w_in, ffn_w_out, w_in_ab, w_out_ab, nsa_w_cmp,
              nsa_pe_cmp, diff_lambda, diff_norm_g, w_in_c, w_out_c):
    n_p = x_prompt.shape[1]
    n_s = x_sample.shape[1]
    past_len = page_table.shape[1] * PAGE_SIZE
    pos_p = jnp.arange(n_p, dtype=jnp.int32)
    pos_s = past_len + jnp.arange(n_s, dtype=jnp.int32)
    hp, hs = x_prompt, x_sample
    nsa_p, nsa_s, win_p, win_s, diff_p, diff_s = [], [], [], [], [], []
    dil_p = [[] for _ in DIL_GROUPS]
    dil_s = [[] for _ in DIL_GROUPS]
    for layer in range(DEPTH):
        g = norm_g[layer]
        hp = ffn_half(hp, g[0], g[1], ffn_w_in[layer, 0], ffn_w_out[layer, 0])
        hs = ffn_half(hs, g[0], g[1], ffn_w_in[layer, 0], ffn_w_out[layer, 0])
        if layer % 2 == 0:
            e = layer // 2
            lam_init = 0.8 - 0.6 * math.exp(-0.3 * layer)
            qa, gate, qd, r_nsa, r_win, r_diff = ab_project(rmsnorm(hp, g[2]), pos_p, w_in_ab[e])
            mp = ab_mix(qa, gate, qd, 0, r_nsa, r_win, 0, r_diff, nsa_w_cmp[e], nsa_pe_cmp[e],
                        diff_lambda[e], diff_norm_g[e], lam_init) @ w_out_ab[e]
            nsa_p.append(r_nsa)
            win_p.append(r_win[:, -min(NSA_WINDOW, n_p):])
            diff_p.append(r_diff)
            qa, gate, qd, r_nsa, r_win, r_diff = ab_project(rmsnorm(hs, g[2]), pos_s, w_in_ab[e])
            nsa_full = jnp.concatenate([gather_pages(cache_nsa[e], page_table), r_nsa], axis=1)
            diff_full = jnp.concatenate([gather_pages(cache_diff[e], page_table), r_diff], axis=1)
            wb = state_nsa_win.shape[2]
            win_full = jnp.concatenate([state_nsa_win[e], r_win], axis=1)
            ms = ab_mix(qa, gate, qd, past_len, nsa_full, win_full, wb, diff_full, nsa_w_cmp[e],
                        nsa_pe_cmp[e], diff_lambda[e], diff_norm_g[e], lam_init) @ w_out_ab[e]
            nsa_s.append(r_nsa)
            win_s.append(win_full[:, -min(NSA_WINDOW, wb + n_s):])
            diff_s.append(r_diff)
        else:
            o = layer // 2
            mp, rows_p = dil_mix_prompt(rmsnorm(hp, g[2]), pos_p, w_in_c[o], w_out_c[o])
            ms, rows_s = dil_mix_sample(rmsnorm(hs, g[2]), pos_s,
                                        (state_dil_0[o], state_dil_1[o], state_dil_2[o]),
                                        w_in_c[o], w_out_c[o])
            for gi in range(len(DIL_GROUPS)):
                dil_p[gi].append(rows_p[gi])
                dil_s[gi].append(rows_s[gi])
        hp = hp + rmsnorm(mp, g[3])
        hs = hs + rmsnorm(ms, g[3])
        hp = ffn_half(hp, g[4], g[5], ffn_w_in[layer, 1], ffn_w_out[layer, 1])
        hs = ffn_half(hs, g[4], g[5], ffn_w_in[layer, 1], ffn_w_out[layer, 1])
    return (hp, hs, jnp.stack(nsa_p), jnp.stack(nsa_s), jnp.stack(win_p), jnp.stack(win_s),
            jnp.stack(diff_p), jnp.stack(diff_s), jnp.stack(dil_p[0]), jnp.stack(dil_s[0]),
            jnp.stack(dil_p[1]), jnp.stack(dil_s[1]), jnp.stack(dil_p[2]), jnp.stack(dil_s[2]))
```

```python
import functools
import math

import jax
import jax.numpy as jnp
import numpy as np
from jax import lax
from jax.experimental import pallas as pl
from jax.experimental.pallas import tpu as pltpu

F32 = jnp.float32
BF16 = jnp.bfloat16

LANES = 128
HEAD_DIM = 64
ROPE_THETA = 10000.0
NORM_EPS = 1e-6
PAGE_SIZE = 128
NSA_HEADS = 8
NSA_KV_HEADS = 2
NSA_GROUP = NSA_HEADS // NSA_KV_HEADS
CMP_LEN = 32
CMP_STRIDE = 16
SLC_BLOCK = 64
N_SEL = 16
NSA_WINDOW = 512
FORCED_BLOCK_SCORE = 1.0e4
DIFF_HEADS = 4
DIFF_NORM_EPS = 1e-5
DIL_GROUPS = ((128, 1), (512, 4), (2048, 16))
DIL_HEADS = 8
DIL_BAND = 128
NEG_BIG = -1.0e30
VMEM_LIMIT_BYTES = 56 * 1024 * 1024


def _cparams(*sem):
    return pltpu.CompilerParams(dimension_semantics=sem, vmem_limit_bytes=VMEM_LIMIT_BYTES)


def _const_spec(shape):
    nd = len(shape)
    return pl.BlockSpec(shape, lambda *_: (0,) * nd, pipeline_mode=pl.Buffered(1))


def _rms(x, g, eps):
    return x * lax.rsqrt(jnp.mean(x * x, axis=-1, keepdims=True) + eps) * g


def _dot(a, b):
    return jnp.dot(a, b, preferred_element_type=F32)


def _dot_nt(a, b):
    return lax.dot_general(a, b, (((1,), (1,)), ((), ())), preferred_element_type=F32)


def _split3(x):
    hi = x.astype(BF16)
    r1 = x - hi.astype(F32)
    mid = r1.astype(BF16)
    lo = (r1 - mid.astype(F32)).astype(BF16)
    return hi, mid, lo


FFN_CHUNK = 256


def _ffn_kernel(x_ref, gpre_ref, gpost_ref, win_ref, wout_ref, o_ref, *, d_ff):
    x = x_ref[...]
    xn = _rms(x, gpre_ref[...], NORM_EPS).astype(BF16)
    acc = jnp.zeros(x.shape, F32)
    for c in range(d_ff // FFN_CHUNK):
        lo = c * FFN_CHUNK
        gate = _dot(xn, win_ref[:, lo:lo + FFN_CHUNK])
        up = _dot(xn, win_ref[:, d_ff + lo:d_ff + lo + FFN_CHUNK])
        act = (gate * jax.nn.sigmoid(gate) * up).astype(BF16)
        acc = acc + _dot(act, wout_ref[lo:lo + FFN_CHUNK, :])
    o_ref[...] = x + 0.5 * _rms(acc, gpost_ref[...], NORM_EPS)


def _token_tile(n):
    return 512 if n % 512 == 0 else n


def ffn_half(h, g_pre, g_post, w_in, w_out):
    n, d = h.shape
    d_ff = w_out.shape[0]
    tm = _token_tile(n)
    return pl.pallas_call(
        functools.partial(_ffn_kernel, d_ff=d_ff),
        grid=(n // tm,),
        in_specs=[pl.BlockSpec((tm, d), lambda i: (i, 0)),
                  _const_spec((1, d)), _const_spec((1, d)),
                  _const_spec(w_in.shape), _const_spec(w_out.shape)],
        out_specs=pl.BlockSpec((tm, d), lambda i: (i, 0)),
        out_shape=jax.ShapeDtypeStruct((n, d), F32),
        compiler_params=_cparams("parallel"),
        name="ffn_half",
    )(h, g_pre.reshape(1, d), g_post.reshape(1, d), w_in, w_out)


def _rope_slab(y, cos, sin):
    lane = lax.broadcasted_iota(jnp.int32, y.shape, 1)
    swapped = jnp.where(lane % HEAD_DIM < HEAD_DIM // 2,
                        pltpu.roll(y, LANES - HEAD_DIM // 2, 1),
                        pltpu.roll(y, HEAD_DIM // 2, 1))
    return y * cos + swapped * sin


def _proj_kernel(x_ref, g_ref, cos_ref, sin_ref, w_ref, *out_refs, plan):
    xn = _rms(x_ref[...], g_ref[...], NORM_EPS).astype(BF16)
    cos = cos_ref[...]
    sin = sin_ref[...]
    for col0, nslab, rope, scale, dests in plan:
        y = _dot(xn, w_ref[:, col0:col0 + nslab * LANES])
        for j in range(nslab):
            ys = y[:, j * LANES:(j + 1) * LANES]
            if rope:
                ys = _rope_slab(ys, cos, sin)
            if scale != 1.0:
                ys = ys * scale
            for out_idx, slab in dests[j]:
                ref = out_refs[out_idx]
                ref[:, slab * LANES:(slab + 1) * LANES] = ys.astype(ref.dtype)


def project(h, g, cos, sin, w, plan, out_defs):
    n, d = h.shape
    tm = _token_tile(n)
    return pl.pallas_call(
        functools.partial(_proj_kernel, plan=plan),
        grid=(n // tm,),
        in_specs=[pl.BlockSpec((tm, d), lambda i: (i, 0)), _const_spec((1, d)),
                  pl.BlockSpec((tm, LANES), lambda i: (i, 0)),
                  pl.BlockSpec((tm, LANES), lambda i: (i, 0)),
                  _const_spec(w.shape)],
        out_specs=[pl.BlockSpec((tm, c), lambda i: (i, 0)) for c, _ in out_defs],
        out_shape=[jax.ShapeDtypeStruct((n, c), dt) for c, dt in out_defs],
        compiler_params=_cparams("parallel"),
        name="project",
    )(h, g.reshape(1, d), cos, sin, w)


def _outproj_kernel(h_ref, m_ref, w_ref, g_ref, o_ref):
    y = _dot(m_ref[...], w_ref[...])
    o_ref[...] = h_ref[...] + _rms(y, g_ref[...], NORM_EPS)


def outproj(h, m, w, g):
    n, d = h.shape
    c = m.shape[1]
    tm = _token_tile(n)
    return pl.pallas_call(
        _outproj_kernel,
        grid=(n // tm,),
        in_specs=[pl.BlockSpec((tm, d), lambda i: (i, 0)), pl.BlockSpec((tm, c), lambda i: (i, 0)),
                  _const_spec(w.shape), _const_spec((1, d))],
        out_specs=pl.BlockSpec((tm, d), lambda i: (i, 0)),
        out_shape=jax.ShapeDtypeStruct((n, d), F32),
        compiler_params=_cparams("parallel"),
        name="outproj",
    )(h, m, w, g.reshape(1, d))


def _outproj_dil_kernel(h_ref, o0_ref, o1_ref, o2_ref, l0_ref, l1_ref, l2_ref, w_ref, g_ref, o_ref):
    l0, l1, l2 = l0_ref[...], l1_ref[...], l2_ref[...]
    mx = jnp.maximum(jnp.maximum(l0, l1), l2)
    e0, e1, e2 = jnp.exp(l0 - mx), jnp.exp(l1 - mx), jnp.exp(l2 - mx)
    den = e0 + e1 + e2
    mix = (e0 / den) * o0_ref[...] + (e1 / den) * o1_ref[...] + (e2 / den) * o2_ref[...]
    y = _dot(mix.astype(BF16), w_ref[...])
    o_ref[...] = h_ref[...] + _rms(y, g_ref[...], NORM_EPS)


def outproj_dil(h, outs, lses, w, g):
    n, d = h.shape
    c = outs[0].shape[1]
    tm = _token_tile(n)
    row = pl.BlockSpec((tm, c), lambda i: (i, 0))
    return pl.pallas_call(
        _outproj_dil_kernel,
        grid=(n // tm,),
        in_specs=[pl.BlockSpec((tm, d), lambda i: (i, 0))] + [row] * 6
                 + [_const_spec(w.shape), _const_spec((1, d))],
        out_specs=pl.BlockSpec((tm, d), lambda i: (i, 0)),
        out_shape=jax.ShapeDtypeStruct((n, d), F32),
        compiler_params=_cparams("parallel"),
        name="outproj_dil",
    )(h, *outs, *lses, w, g.reshape(1, d))


def _compress_rows(k_ref, v_ref, pe_ref, wa_ref, wb_ref, nsub):
    half = CMP_LEN // 2
    acc_a = jnp.zeros((nsub, 2 * LANES), F32)
    acc_b = jnp.zeros((nsub, 2 * LANES), F32)
    for l in range(half):
        x = jnp.concatenate([k_ref[pl.ds(l, nsub, stride=CMP_STRIDE), :],
                             v_ref[pl.ds(l, nsub, stride=CMP_STRIDE), :]], axis=1)
        acc_a = acc_a + _dot((x + pe_ref[l:l + 1, :]).astype(BF16), wa_ref[l])
        acc_b = acc_b + _dot((x + pe_ref[half + l:half + l + 1, :]).astype(BF16), wb_ref[l])
    comp = acc_a + pltpu.roll(acc_b, nsub - 1, 0)
    row = lax.broadcasted_iota(jnp.int32, comp.shape, 0)
    return jnp.where(row < nsub - 1, comp, 0.0)


def _compress_kernel(k_ref, v_ref, pe_ref, wa_ref, wb_ref, o_ref, *, nsub):
    o_ref[...] = _compress_rows(k_ref, v_ref, pe_ref, wa_ref, wb_ref, nsub).astype(o_ref.dtype)


def nsa_compress_prompt(rows_nsa, pe4, wa, wb):
    b, t, _ = rows_nsa.shape
    nsub = t // CMP_STRIDE
    return pl.pallas_call(
        functools.partial(_compress_kernel, nsub=nsub),
        grid=(b,),
        in_specs=[pl.BlockSpec((None, t, LANES), lambda i: (i, 0, 0)),
                  pl.BlockSpec((None, t, LANES), lambda i: (i, 0, 1)),
                  _const_spec(pe4.shape), _const_spec(wa.shape), _const_spec(wb.shape)],
        out_specs=pl.BlockSpec((None, nsub, 2 * LANES), lambda i: (i, 0, 0)),
        out_shape=jax.ShapeDtypeStruct((b, nsub, 2 * LANES), BF16),
        compiler_params=_cparams("parallel"),
        name="nsa_compress_prompt",
    )(rows_nsa, rows_nsa, pe4, wa, wb)


NSA_TQ = 128
NSA_TK = 512


def _softmax_rows(s, mask):
    s = jnp.where(mask, s, -jnp.inf)
    m = jnp.max(s, axis=-1, keepdims=True)
    m = jnp.where(m == -jnp.inf, 0.0, m)
    e = jnp.exp(s - m)
    den = jnp.sum(e, axis=-1, keepdims=True)
    return e / jnp.where(den > 0, den, 1.0)


def _topk_mask_t(score_t, k):
    j_io = lax.broadcasted_iota(jnp.int32, score_t.shape, 0)
    nj = score_t.shape[0]
    sel = jnp.zeros(score_t.shape, F32)
    for _ in range(k):
        m = jnp.max(score_t, axis=0, keepdims=True)
        cand = jnp.where(score_t == m, j_io, nj)
        jmin = jnp.min(cand, axis=0, keepdims=True)
        hit = j_io == jmin
        sel = jnp.where(jnp.logical_and(hit, m > -jnp.inf), 1.0, sel)
        score_t = jnp.where(hit, -jnp.inf, score_t)
    return sel


def _nsa_prompt_kernel(q_ref, gate_ref, cmp_ref, slc_ref, win_ref, eall_ref, selt_ref, o_ref, *, t_len):
    tq = NSA_TQ
    qs = pl.program_id(1) * tq
    rows = NSA_GROUP * tq
    qpos = qs + lax.broadcasted_iota(jnp.int32, (rows, 1), 0) % tq
    ncmp = cmp_ref.shape[0]
    gates = jax.nn.sigmoid(gate_ref[...])
    lane = lax.broadcasted_iota(jnp.int32, (tq, LANES), 1)
    head_out = []
    for kvh in range(NSA_KV_HEADS):
        q4 = jnp.concatenate(
            [q_ref[:, (NSA_GROUP * kvh + g) * LANES:(NSA_GROUP * kvh + g + 1) * LANES]
             for g in range(NSA_GROUP)], axis=0)
        s = _dot_nt(q4, cmp_ref[:, 0:LANES])
        cmp_end = lax.broadcasted_iota(jnp.int32, (1, ncmp), 1) * CMP_STRIDE + (CMP_LEN - 1)
        p = _softmax_rows(s, cmp_end <= qpos)
        o_cmp = _dot(p.astype(BF16), cmp_ref[:, LANES:2 * LANES])
        psum = p[0:tq] + p[tq:2 * tq] + p[2 * tq:3 * tq] + p[3 * tq:4 * tq]
        hi, mid, lo = _split3(psum)
        selt = selt_ref[...]
        imp_t = _dot_nt(selt, hi) + _dot_nt(selt, mid) + _dot_nt(selt, lo)
        j_io = lax.broadcasted_iota(jnp.int32, imp_t.shape, 0)
        cur = (qs + lax.broadcasted_iota(jnp.int32, imp_t.shape, 1)) // SLC_BLOCK
        forced = jnp.logical_or(j_io == 0, j_io == cur)
        score_t = jnp.where(j_io <= cur, jnp.where(forced, FORCED_BLOCK_SCORE, imp_t), -jnp.inf)
        sel_q = _topk_mask_t(score_t, N_SEL).T
        bias = jnp.where(sel_q > 0.5, 0.0, NEG_BIG).astype(BF16)
        qext = jnp.concatenate([q4, jnp.concatenate([bias] * NSA_GROUP, axis=0)], axis=1)

        def sel_step(t, carry):
            m, l, acc = carry
            ks = pl.multiple_of(t * NSA_TK, NSA_TK)
            kext = jnp.concatenate([slc_ref[pl.ds(ks, NSA_TK), 0:LANES], eall_ref[pl.ds(ks, NSA_TK), :]],
                                   axis=1)
            sc = _dot_nt(qext, kext)
            kpos = ks + lax.broadcasted_iota(jnp.int32, (1, NSA_TK), 1)
            sc = jnp.where(kpos <= qpos, sc, NEG_BIG)
            m_new = jnp.maximum(m, jnp.max(sc, axis=-1, keepdims=True))
            alpha = jnp.exp(m - m_new)
            pe = jnp.exp(sc - m_new)
            l = alpha * l + jnp.sum(pe, axis=-1, keepdims=True)
            acc = alpha * acc + _dot(pe.astype(BF16), slc_ref[pl.ds(ks, NSA_TK), LANES:2 * LANES])
            return m_new, l, acc

        n_tiles = (qs + tq + NSA_TK - 1) // NSA_TK
        init = (jnp.full((rows, 1), NEG_BIG, F32), jnp.zeros((rows, 1), F32), jnp.zeros((rows, LANES), F32))
        _, l_sel, acc_sel = lax.fori_loop(0, n_tiles, sel_step, init)
        o_sel = acc_sel / l_sel
        wlen = NSA_WINDOW + tq
        ws = pl.multiple_of(jnp.maximum(qs - NSA_WINDOW, 0), tq)
        sw = _dot_nt(q4, win_ref[pl.ds(ws, wlen), 0:LANES])
        dist = qpos - (ws + lax.broadcasted_iota(jnp.int32, (1, wlen), 1))
        pw = _softmax_rows(sw, jnp.logical_and(dist >= 0, dist < NSA_WINDOW))
        o_win = _dot(pw.astype(BF16), win_ref[pl.ds(ws, wlen), LANES:2 * LANES])
        for g in range(NSA_GROUP):
            h = NSA_GROUP * kvh + g
            r = slice(g * tq, (g + 1) * tq)
            head_out.append(gates[:, 3 * h:3 * h + 1] * o_cmp[r] + gates[:, 3 * h + 1:3 * h + 2] * o_sel[r]
                            + gates[:, 3 * h + 2:3 * h + 3] * o_win[r])
    for pair in range(NSA_HEADS // 2):
        a, b = head_out[2 * pair], head_out[2 * pair + 1]
        if (2 * pair) // NSA_GROUP == 0:
            slab = jnp.where(lane < HEAD_DIM, a, pltpu.roll(b, HEAD_DIM, 1))
        else:
            slab = jnp.where(lane < HEAD_DIM, pltpu.roll(a, HEAD_DIM, 1), b)
        o_ref[:, pair * LANES:(pair + 1) * LANES] = slab.astype(o_ref.dtype)


def nsa_attend_prompt(q_nsa, gate, cmp_kv, kv_nsa16, kv_win16, eall, selt):
    b, t, _ = q_nsa.shape
    ncmp = cmp_kv.shape[1]
    return pl.pallas_call(
        functools.partial(_nsa_prompt_kernel, t_len=t),
        grid=(b, t // NSA_TQ),
        in_specs=[pl.BlockSpec((None, NSA_TQ, NSA_HEADS * LANES), lambda i, j: (i, j, 0)),
                  pl.BlockSpec((None, NSA_TQ, LANES), lambda i, j: (i, j, 0)),
                  pl.BlockSpec((None, ncmp, 2 * LANES), lambda i, j: (i, 0, 0)),
                  pl.BlockSpec((None, t, 2 * LANES), lambda i, j: (i, 0, 1)),
                  pl.BlockSpec((None, t, 2 * LANES), lambda i, j: (i, 0, 0)),
                  _const_spec(eall.shape), _const_spec(selt.shape)],
        out_specs=pl.BlockSpec((None, NSA_TQ, NSA_HEADS * HEAD_DIM), lambda i, j: (i, j, 0)),
        out_shape=jax.ShapeDtypeStruct((b, t, NSA_HEADS * HEAD_DIM), BF16),
        compiler_params=_cparams("parallel", "parallel"),
        name="nsa_attend_prompt",
    )(q_nsa, gate, cmp_kv, kv_nsa16, kv_win16, eall, selt)


DIFF_TQ = 256
DIFF_TK = 512


def _diff_lambda(lam_ref, lam_init):
    lv = lam_ref[...]
    a = jnp.sum(lv[0:1] * lv[1:2], axis=-1, keepdims=True)
    b = jnp.sum(lv[2:3] * lv[3:4], axis=-1, keepdims=True)
    return jnp.exp(a) - jnp.exp(b) + lam_init


def _diff_prompt_kernel(q_ref, k_ref, v_ref, lam_ref, ng_ref, o_ref, *, lam_init):
    tq = DIFF_TQ
    qs = pl.program_id(2) * tq
    q2 = jnp.concatenate([q_ref[:, 0:LANES], q_ref[:, LANES:2 * LANES]], axis=0)
    qpos = qs + lax.broadcasted_iota(jnp.int32, (2 * tq, 1), 0) % tq

    def step(t, carry):
        m, l, acc = carry
        ks = pl.multiple_of(t * DIFF_TK, DIFF_TK)
        sc = _dot_nt(q2, k_ref[pl.ds(ks, DIFF_TK), :])
        kpos = ks + lax.broadcasted_iota(jnp.int32, (1, DIFF_TK), 1)
        sc = jnp.where(kpos <= qpos, sc, NEG_BIG)
        m_new = jnp.maximum(m, jnp.max(sc, axis=-1, keepdims=True))
        alpha = jnp.exp(m - m_new)
        pe = jnp.exp(sc - m_new)
        l = alpha * l + jnp.sum(pe, axis=-1, keepdims=True)
        acc = alpha * acc + _dot(pe.astype(BF16), v_ref[pl.ds(ks, DIFF_TK), :])
        return m_new, l, acc

    n_tiles = (qs + tq + DIFF_TK - 1) // DIFF_TK
    init = (jnp.full((2 * tq, 1), NEG_BIG, F32), jnp.zeros((2 * tq, 1), F32), jnp.zeros((2 * tq, LANES), F32))
    _, l, acc = lax.fori_loop(0, n_tiles, step, init)
    o = acc / l
    lam = _diff_lambda(lam_ref, lam_init)
    o = o[0:tq] - lam * o[tq:2 * tq]
    o_ref[...] = (_rms(o, ng_ref[...], DIFF_NORM_EPS) * (1.0 - lam_init)).astype(o_ref.dtype)


def diff_attend_prompt(q_diff, kv_diff16, lam_vec, norm_g, lam_init):
    b, t, _ = q_diff.shape
    return pl.pallas_call(
        functools.partial(_diff_prompt_kernel, lam_init=lam_init),
        grid=(b, DIFF_HEADS, t // DIFF_TQ),
        in_specs=[pl.BlockSpec((None, DIFF_TQ, 2 * LANES), lambda i, h, j: (i, j, h)),
                  pl.BlockSpec((None, t, LANES), lambda i, h, j: (i, 0, h)),
                  pl.BlockSpec((None, t, LANES), lambda i, h, j: (i, 0, DIFF_HEADS + h)),
                  _const_spec(lam_vec.shape), _const_spec((1, LANES))],
        out_specs=pl.BlockSpec((None, DIFF_TQ, LANES), lambda i, h, j: (i, j, h)),
        out_shape=jax.ShapeDtypeStruct((b, t, DIFF_HEADS * LANES), BF16),
        compiler_params=_cparams("parallel", "parallel", "parallel"),
        name="diff_attend_prompt",
    )(q_diff, kv_diff16, kv_diff16, lam_vec, norm_g.reshape(1, LANES))


def _dil_prompt_kernel(q_ref, kp_ref, kc_ref, vp_ref, vc_ref, o_ref, lse_ref):
    band = DIL_BAND
    first = pl.program_id(2) == 0
    qi = lax.broadcasted_iota(jnp.int32, (band, 2 * band), 0) + band
    kj = lax.broadcasted_iota(jnp.int32, (band, 2 * band), 1)
    rel = qi - kj
    ok = jnp.logical_and(rel >= 0, rel <= band)
    ok = jnp.logical_and(ok, jnp.logical_not(jnp.logical_and(first, kj < band)))
    lane = lax.broadcasted_iota(jnp.int32, (band, LANES), 1)
    for pair in range(DIL_HEADS // 2):
        cs = slice(pair * LANES, (pair + 1) * LANES)
        k2 = jnp.concatenate([kp_ref[:, cs], kc_ref[:, cs]], axis=0)
        v2 = jnp.concatenate([vp_ref[:, cs], vc_ref[:, cs]], axis=0)
        outs, lses = [], []
        for hh in range(2):
            h = 2 * pair + hh
            s = _dot_nt(q_ref[:, h * LANES:(h + 1) * LANES], k2)
            s = jnp.where(ok, s, -jnp.inf)
            m = jnp.max(s, axis=-1, keepdims=True)
            e = jnp.exp(s - m)
            den = jnp.sum(e, axis=-1, keepdims=True)
            outs.append(_dot((e / den).astype(BF16), v2))
            lses.append(jnp.log(den) + m)
        o_ref[:, cs] = jnp.where(lane < HEAD_DIM, outs[0], outs[1])
        lse_ref[:, cs] = jnp.where(lane < HEAD_DIM, lses[0], lses[1])


def dil_attend_prompt(q_dil, kv_dil16, gi, dilation):
    b, t, cq = q_dil.shape
    ckv = kv_dil16.shape[2]
    ng = len(DIL_GROUPS)
    n = t // dilation
    nq = n // DIL_BAND
    hw = DIL_HEADS * HEAD_DIM
    qv = q_dil.reshape(b, n, dilation * cq)
    kvv = kv_dil16.reshape(b, n, dilation * ckv)
    kvb = ckv // hw

    def kv_spec(which, prev):
        def imap(i, r, u):
            return (i, jnp.maximum(u - 1, 0) if prev else u, r * kvb + 2 * gi + which)
        return pl.BlockSpec((None, DIL_BAND, hw), imap)

    o_spec = pl.BlockSpec((None, DIL_BAND, hw), lambda i, r, u: (i, u, r))
    o, lse = pl.pallas_call(
        _dil_prompt_kernel,
        grid=(b, dilation, nq),
        in_specs=[pl.BlockSpec((None, DIL_BAND, DIL_HEADS * LANES), lambda i, r, u: (i, u, r * ng + gi)),
                  kv_spec(0, True), kv_spec(0, False), kv_spec(1, True), kv_spec(1, False)],
        out_specs=[o_spec, o_spec],
        out_shape=[jax.ShapeDtypeStruct((b, n, dilation * hw), F32)] * 2,
        compiler_params=_cparams("parallel", "parallel", "parallel"),
        name="dil_attend_prompt",
    )(qv, kvv, kvv, kvv, kvv)
    return o.reshape(b, t, hw), lse.reshape(b, t, hw)


AB_SIZES = (NSA_HEADS * HEAD_DIM, 6 * NSA_KV_HEADS * HEAD_DIM, 3 * NSA_HEADS,
            DIFF_HEADS * 2 * HEAD_DIM, DIFF_HEADS * 2 * HEAD_DIM, DIFF_HEADS * 2 * HEAD_DIM)


def _rope_tables(pos):
    half = HEAD_DIM // 2
    inv = ROPE_THETA ** (-jnp.arange(half, dtype=F32) / half)
    ang = pos.astype(F32)[:, None] * inv[None, :]
    c, s = jnp.cos(ang), jnp.sin(ang)
    return jnp.tile(c, (1, 4)), jnp.tile(jnp.concatenate([-s, s], axis=1), (1, 2))


def _pad_heads(w, offsets):
    z = jnp.zeros_like(w)
    lo = jnp.concatenate([w, z], axis=-1)
    hi = jnp.concatenate([z, w], axis=-1)
    at_lo = (np.asarray(offsets) == 0)[None, :, None]
    return jnp.where(at_lo, lo, hi).reshape(w.shape[0], -1)


def _prep_w_ab(w):
    d = w.shape[0]
    qa, kvb, gl, qd, kd, vd = jnp.split(w, np.cumsum(AB_SIZES)[:-1].tolist(), axis=1)
    qa = _pad_heads(qa.reshape(d, NSA_HEADS, HEAD_DIM), [(h // NSA_GROUP) * HEAD_DIM for h in range(NSA_HEADS)])
    kvb = kvb.reshape(d, 6, LANES)
    k3 = kvb[:, 0::2].reshape(d, 3 * LANES)
    v3 = kvb[:, 1::2].reshape(d, 3 * LANES)
    qd = _pad_heads(qd.reshape(d, 2 * DIFF_HEADS, HEAD_DIM), [(i % 2) * HEAD_DIM for i in range(2 * DIFF_HEADS)])
    gl = jnp.pad(gl, ((0, 0), (0, LANES - gl.shape[1])))
    return jnp.concatenate([qa, k3, v3, qd, kd, vd, gl], axis=1).astype(BF16)


AB_OUT_DEFS = ((1024, BF16), (512, F32), (512, BF16), (256, F32), (256, BF16),
               (1024, BF16), (1024, F32), (1024, BF16), (128, F32))
_QSCALE = HEAD_DIM ** -0.5
AB_PLAN = (
    (0, 8, True, _QSCALE, tuple(((0, j),) for j in range(8))),
    (1024, 3, True, 1.0, (((1, 0), (2, 0)), ((1, 2), (2, 2)), ((3, 0), (4, 0)))),
    (1408, 3, False, 1.0, (((1, 1), (2, 1)), ((1, 3), (2, 3)), ((3, 1), (4, 1)))),
    (1792, 8, True, _QSCALE, tuple(((5, j),) for j in range(8))),
    (2816, 4, True, 1.0, tuple(((6, j), (7, j)) for j in range(4))),
    (3328, 4, False, 1.0, tuple(((6, 4 + j), (7, 4 + j)) for j in range(4))),
    (3840, 1, False, 1.0, (((8, 0),),)),
)


def _prep_w_c(w):
    d = w.shape[0]
    ng = len(DIL_GROUPS)
    w = w.reshape(d, ng, 3, DIL_HEADS, HEAD_DIM)
    cols = []
    for g in range(ng):
        cols.append(_pad_heads(w[:, g, 0], [(h % 2) * HEAD_DIM for h in range(DIL_HEADS)]))
        cols.append(w[:, g, 1].reshape(d, -1))
        cols.append(w[:, g, 2].reshape(d, -1))
    return jnp.concatenate(cols, axis=1).astype(BF16)


C_OUT_DEFS = ((3072, BF16), (3072, BF16), (1024, F32), (1024, F32), (1024, F32))
C_PLAN = tuple(
    seg for g in range(3) for seg in (
        (g * 2048, 8, True, _QSCALE, tuple(((0, g * 8 + j),) for j in range(8))),
        (g * 2048 + 1024, 4, True, 1.0, tuple(((1, g * 8 + j), (2 + g, j)) for j in range(4))),
        (g * 2048 + 1536, 4, False, 1.0, tuple(((1, g * 8 + 4 + j), (2 + g, 4 + j)) for j in range(4))),
    ))


def _prep_cmp(w_cmp, pe_cmp):
    wk, wv = w_cmp[0], w_cmp[1]
    z = jnp.zeros_like(wk)
    w4 = jnp.concatenate([jnp.concatenate([wk, z, z, z], axis=-1), jnp.concatenate([z, wk, z, z], axis=-1),
                          jnp.concatenate([z, z, wv, z], axis=-1), jnp.concatenate([z, z, z, wv], axis=-1)],
                         axis=1).astype(BF16)
    pe4 = jnp.concatenate([pe_cmp[0], pe_cmp[0], pe_cmp[1], pe_cmp[1]], axis=-1)
    half = CMP_LEN // 2
    return pe4, w4[:half], w4[half:]


def _block_indicator(n_keys):
    return (jnp.arange(n_keys)[:, None] // SLC_BLOCK == jnp.arange(LANES)[None, :]).astype(BF16)


def _cmp_to_block(n_cmp):
    r = SLC_BLOCK // CMP_STRIDE
    return (jnp.arange(n_cmp)[None, :] // r == jnp.arange(LANES)[:, None]).astype(BF16)


def ab_mix_prompt(h, b, t, g_in, w_ab, cmp_prep, lam_vec, dn_g, lam_init, cos, sin):
    q_nsa, rows_nsa, kv_nsa16, rows_win, kv_win16, q_diff, rows_diff, kv_diff16, gate = project(
        h, g_in, cos, sin, w_ab, AB_PLAN, AB_OUT_DEFS)
    r3 = lambda x: x.reshape(b, t, x.shape[-1])
    pe4, wa, wb = cmp_prep
    cmp_kv = nsa_compress_prompt(r3(rows_nsa), pe4, wa, wb)
    o_nsa = nsa_attend_prompt(r3(q_nsa), r3(gate), cmp_kv, r3(kv_nsa16), r3(kv_win16),
                              _block_indicator(t), _cmp_to_block(t // CMP_STRIDE))
    o_diff = diff_attend_prompt(r3(q_diff), r3(kv_diff16), lam_vec, dn_g, lam_init)
    mixed = jnp.concatenate([o_nsa, o_diff], axis=-1).reshape(b * t, -1)
    return mixed, r3(rows_nsa), r3(rows_win), r3(rows_diff)


def dil_mix_prompt(h, b, t, g_in, w_c, cos, sin):
    outs = project(h, g_in, cos, sin, w_c, C_PLAN, C_OUT_DEFS)
    q_dil = outs[0].reshape(b, t, -1)
    kv_dil16 = outs[1].reshape(b, t, -1)
    os_, lses = [], []
    for gi, (win, dil) in enumerate(DIL_GROUPS):
        o, lse = dil_attend_prompt(q_dil, kv_dil16, gi, dil)
        os_.append(o.reshape(b * t, -1))
        lses.append(lse.reshape(b * t, -1))
    rows = [x.reshape(b, t, -1) for x in outs[2:]]
    return os_, lses, rows


def _page_copies(cache_ref, pt_ref, kbuf_ref, vbuf_ref, sem_ref, bi, slot, n_pages):
    cps = []
    for j in range(n_pages):
        for part, buf in enumerate((kbuf_ref, vbuf_ref)):
            cps.append(pltpu.make_async_copy(cache_ref.at[pt_ref[bi, j], :, pl.ds(part * LANES, LANES)],
                                             buf.at[slot, pl.ds(j * PAGE_SIZE, PAGE_SIZE), :],
                                             sem_ref.at[slot]))
    return cps


def _compress_sample_kernel(pt_ref, cache_ref, pe_ref, wa_ref, wb_ref, o_ref, kbuf_ref, vbuf_ref, sem_ref, *,
                            n_pages):
    i = pl.program_id(0)
    slot = i % 2
    nsub = n_pages * PAGE_SIZE // CMP_STRIDE
    copies = functools.partial(_page_copies, cache_ref, pt_ref, kbuf_ref, vbuf_ref, sem_ref, n_pages=n_pages)

    @pl.when(i == 0)
    def _():
        for cp in copies(0, 0):
            cp.start()

    @pl.when(i + 1 < pl.num_programs(0))
    def _():
        for cp in copies(i + 1, 1 - slot):
            cp.start()

    for cp in copies(i, slot):
        cp.wait()
    o_ref[...] = _compress_rows(kbuf_ref.at[slot], vbuf_ref.at[slot], pe_ref, wa_ref, wb_ref,
                                nsub).astype(o_ref.dtype)


def nsa_compress_sample(cache, page_table, pe4, wa, wb):
    bs, n_pages = page_table.shape
    past = n_pages * PAGE_SIZE
    nsub = past // CMP_STRIDE
    grid_spec = pltpu.PrefetchScalarGridSpec(
        num_scalar_prefetch=1,
        grid=(bs,),
        in_specs=[pl.BlockSpec(memory_space=pl.ANY),
                  pl.BlockSpec(pe4.shape, lambda i, pt: (0, 0)),
                  pl.BlockSpec(wa.shape, lambda i, pt: (0, 0, 0)),
                  pl.BlockSpec(wb.shape, lambda i, pt: (0, 0, 0))],
        out_specs=pl.BlockSpec((None, nsub, 2 * LANES), lambda i, pt: (i, 0, 0)),
        scratch_shapes=[pltpu.VMEM((2, past, LANES), F32), pltpu.VMEM((2, past, LANES), F32),
                        pltpu.SemaphoreType.DMA((2,))],
    )
    return pl.pallas_call(
        functools.partial(_compress_sample_kernel, n_pages=n_pages),
        grid_spec=grid_spec,
        out_shape=jax.ShapeDtypeStruct((bs, nsub, 2 * LANES), BF16),
        compiler_params=_cparams("arbitrary"),
        name="nsa_compress_sample",
    )(page_table, cache, pe4, wa, wb)


def _pad_rows(x, rows):
    return jnp.concatenate([x, jnp.zeros((rows - x.shape[0], x.shape[1]), x.dtype)], axis=0)


def _group_sum_rows(x):
    parts = [jnp.sum(x[NSA_GROUP * k:NSA_GROUP * (k + 1)], axis=0, keepdims=True) for k in range(NSA_KV_HEADS)]
    return _pad_rows(jnp.concatenate(parts, axis=0), x.shape[0])


def _nsa_sample_cmp_kernel(q_ref, cmp_ref, selt_ref, ocmp_ref, imp_ref, *, qpos):
    q8 = q_ref[...]
    ncmp = cmp_ref.shape[0]
    s = _dot_nt(q8, cmp_ref[:, 0:LANES])
    cmp_end = lax.broadcasted_iota(jnp.int32, (1, ncmp), 1) * CMP_STRIDE + (CMP_LEN - 1)
    p = _softmax_rows(s, cmp_end <= qpos)
    ocmp_ref[...] = _dot(p.astype(BF16), cmp_ref[:, LANES:2 * LANES])
    hi, mid, lo = _split3(_group_sum_rows(p))
    selt = selt_ref[...]
    imp_ref[...] = _dot_nt(hi, selt) + _dot_nt(mid, selt) + _dot_nt(lo, selt)


def nsa_sample_cmp(q8, cmp_kv, selt, qpos):
    bs = q8.shape[0]
    ncmp = cmp_kv.shape[1]
    blk = pl.BlockSpec((None, NSA_HEADS, LANES), lambda i: (i, 0, 0))
    return pl.pallas_call(
        functools.partial(_nsa_sample_cmp_kernel, qpos=qpos),
        grid=(bs,),
        in_specs=[blk, pl.BlockSpec((None, ncmp, 2 * LANES), lambda i: (i, 0, 0)), _const_spec(selt.shape)],
        out_specs=[blk, blk],
        out_shape=[jax.ShapeDtypeStruct((bs, NSA_HEADS, LANES), F32)] * 2,
        compiler_params=_cparams("parallel"),
        name="nsa_sample_cmp",
    )(q8, cmp_kv, selt)


def _topk_lanes_kernel(imp_ref, idx_ref, *, k):
    score = imp_ref[...]
    lane = lax.broadcasted_iota(jnp.int32, score.shape, 1)
    lane_f = lane.astype(F32)
    score = jnp.where(lane == 0, FORCED_BLOCK_SCORE, score)
    idx = jnp.zeros(score.shape, F32)
    for r in range(k):
        m = jnp.max(score, axis=-1, keepdims=True)
        jmin = jnp.min(jnp.where(score == m, lane_f, float(LANES)), axis=-1, keepdims=True)
        idx = jnp.where(lane == r, jmin, idx)
        score = jnp.where(lane_f == jmin, -jnp.inf, score)
    idx_ref[...] = idx.astype(jnp.int32)


def topk_lanes(imp, k):
    bs = imp.shape[0]
    x = imp.reshape(bs * NSA_HEADS, LANES)
    out = pl.pallas_call(
        functools.partial(_topk_lanes_kernel, k=k),
        out_shape=jax.ShapeDtypeStruct(x.shape, jnp.int32),
        name="topk_lanes",
    )(x)
    return out.reshape(bs, NSA_HEADS, LANES)


N_SEL_CACHE = N_SEL - 1


def _sel_copies(cache_ref, pt_ref, sel_ref, buf_ref, sem_ref, bi, slot):
    cps = []
    for kvh in range(NSA_KV_HEADS):
        for r in range(N_SEL_CACHE):
            j = sel_ref[bi, kvh * N_SEL_CACHE + r]
            page = pt_ref[bi, j // 2]
            row0 = pl.multiple_of((j % 2) * SLC_BLOCK, SLC_BLOCK)
            cps.append(pltpu.make_async_copy(
                cache_ref.at[page, pl.ds(row0, SLC_BLOCK), pl.ds(2 * LANES, 2 * LANES)],
                buf_ref.at[slot, kvh, pl.ds(r * SLC_BLOCK, SLC_BLOCK), :],
                sem_ref.at[slot]))
    return cps


def _pick_gate(gates8, branch):
    row = lax.broadcasted_iota(jnp.int32, gates8.shape, 0)
    lane = lax.broadcasted_iota(jnp.int32, gates8.shape, 1)
    return jnp.sum(jnp.where(lane == 3 * row + branch, gates8, 0.0), axis=-1, keepdims=True)


def _nsa_sample_attend_kernel(pt_ref, sel_ref, q_ref, gate_ref, ocmp_ref, new_ref, win_ref, cache_ref,
                              o_ref, buf_ref, sem_ref):
    i = pl.program_id(0)
    slot = i % 2

    @pl.when(i == 0)
    def _():
        for cp in _sel_copies(cache_ref, pt_ref, sel_ref, buf_ref, sem_ref, 0, 0):
            cp.start()

    @pl.when(i + 1 < pl.num_programs(0))
    def _():
        for cp in _sel_copies(cache_ref, pt_ref, sel_ref, buf_ref, sem_ref, i + 1, 1 - slot):
            cp.start()

    q8 = q_ref[...]
    q8f = q8.astype(F32)
    row = lax.broadcasted_iota(jnp.int32, (NSA_HEADS, 1), 0)
    new = new_ref[...]
    rnd = lambda x: x.astype(BF16).astype(F32)

    def with_new_key(s, v_mat, k_new, v_new, mask=None):
        s_new = jnp.sum(q8f * rnd(k_new), axis=-1, keepdims=True)
        if mask is not None:
            s = jnp.where(mask, s, -jnp.inf)
        m = jnp.maximum(jnp.max(s, axis=-1, keepdims=True), s_new)
        e = jnp.exp(s - m)
        e_new = jnp.exp(s_new - m)
        den = jnp.sum(e, axis=-1, keepdims=True) + e_new
        return (_dot((e / den).astype(BF16), v_mat) + rnd(e_new / den) * rnd(v_new))

    wb = win_ref.shape[0]
    widx = lax.broadcasted_iota(jnp.int32, (1, wb), 1)
    o_win = with_new_key(_dot_nt(q8, win_ref[:, 0:LANES].astype(BF16)), win_ref[:, LANES:2 * LANES].astype(BF16),
                         new[:, 4 * LANES:5 * LANES], new[:, 5 * LANES:6 * LANES], mask=widx > wb - NSA_WINDOW)
    for cp in _sel_copies(cache_ref, pt_ref, sel_ref, buf_ref, sem_ref, i, slot):
        cp.wait()
    o_sel = []
    for kvh in range(NSA_KV_HEADS):
        kmat = buf_ref[slot, kvh, :, 0:LANES].astype(BF16)
        vmat = buf_ref[slot, kvh, :, LANES:2 * LANES].astype(BF16)
        o_sel.append(with_new_key(_dot_nt(q8, kmat), vmat, new[:, 2 * LANES:3 * LANES], new[:, 3 * LANES:4 * LANES]))
    o_sel = jnp.where(row < NSA_GROUP, o_sel[0], o_sel[1])
    gates8 = jnp.broadcast_to(jax.nn.sigmoid(gate_ref[...]), (NSA_HEADS, LANES))
    o_ref[...] = (_pick_gate(gates8, 0) * ocmp_ref[...] + _pick_gate(gates8, 1) * o_sel
                  + _pick_gate(gates8, 2) * o_win)


def nsa_sample_attend(page_table, sel_idx, q8, gate, o_cmp, new_rows, win_state, cache):
    bs = q8.shape[0]
    wb = win_state.shape[1]
    blk = lambda w: pl.BlockSpec((None, NSA_HEADS, w), lambda i, pt, sel: (i, 0, 0))
    one = lambda w: pl.BlockSpec((None, 1, w), lambda i, pt, sel: (i, 0, 0))
    grid_spec = pltpu.PrefetchScalarGridSpec(
        num_scalar_prefetch=2,
        grid=(bs,),
        in_specs=[blk(LANES), one(LANES), blk(LANES), one(new_rows.shape[-1]),
                  pl.BlockSpec((None, wb, 2 * LANES), lambda i, pt, sel: (i, 0, 0)),
                  pl.BlockSpec(memory_space=pl.ANY)],
        out_specs=blk(LANES),
        scratch_shapes=[pltpu.VMEM((2, NSA_KV_HEADS, N_SEL_CACHE * SLC_BLOCK, 2 * LANES), F32),
                        pltpu.SemaphoreType.DMA((2,))],
    )
    return pl.pallas_call(
        _nsa_sample_attend_kernel,
        grid_spec=grid_spec,
        out_shape=jax.ShapeDtypeStruct((bs, NSA_HEADS, LANES), F32),
        compiler_params=_cparams("arbitrary"),
        name="nsa_sample_attend",
    )(page_table, sel_idx, q8, gate, o_cmp, new_rows, win_state, cache)


DIFF_PAGES_PER_STEP = 8


def _head_expand(n_heads, width):
    r = lax.broadcasted_iota(jnp.int32, (LANES, n_heads * width), 0)
    c = lax.broadcasted_iota(jnp.int32, (LANES, n_heads * width), 1)
    return (c // width == r).astype(BF16)


def _diff_sample_kernel(pt_ref, qt_ref, new_ref, lam_ref, ng_ref, *rest, lam_init):
    pages = rest[:DIFF_PAGES_PER_STEP]
    o_ref = rest[DIFF_PAGES_PER_STEP]
    s_ref, m_ref, l_ref, acc_ref = rest[DIFF_PAGES_PER_STEP + 1:]
    phase = pl.program_id(1)
    c = pl.program_id(2)
    hw = DIFF_HEADS * LANES
    qt = _pad_rows(qt_ref[...], LANES)
    s_new = _dot_nt(_pad_rows(new_ref[:, 0:hw], 8).astype(BF16), qt)[0:1]
    lam = _diff_lambda(lam_ref, lam_init)

    def mix(pn):
        return pn - lam * pltpu.roll(pn, LANES - DIFF_HEADS, 1)

    @pl.when(jnp.logical_and(phase == 0, c == 0))
    def _():
        m_ref[...] = s_new
        l_ref[...] = jnp.ones_like(s_new)
        acc_ref[...] = jnp.zeros_like(acc_ref)

    @pl.when(phase == 0)
    def _():
        m = m_ref[...]
        l = l_ref[...]
        for i, page in enumerate(pages):
            s = _dot_nt(page[...].astype(BF16), qt)
            s_ref[pl.ds(pl.multiple_of((c * DIFF_PAGES_PER_STEP + i) * PAGE_SIZE, PAGE_SIZE), PAGE_SIZE), :] = s
            m_new = jnp.maximum(m, jnp.max(s, axis=0, keepdims=True))
            l = l * jnp.exp(m - m_new) + jnp.sum(jnp.exp(s - m_new), axis=0, keepdims=True)
            m = m_new
        m_ref[...] = m
        l_ref[...] = l

    @pl.when(phase == 1)
    def _():
        m = m_ref[...]
        inv_l = 1.0 / l_ref[...]
        expand = _head_expand(DIFF_HEADS, LANES)
        acc = acc_ref[...]
        for i, page in enumerate(pages):
            s = s_ref[pl.ds(pl.multiple_of((c * DIFF_PAGES_PER_STEP + i) * PAGE_SIZE, PAGE_SIZE), PAGE_SIZE), :]
            a = mix(jnp.exp(s - m) * inv_l)
            w = _dot(a.astype(BF16), expand) * page[...]
            acc = acc + jnp.sum(w.reshape(PAGE_SIZE // 8, 8, hw), axis=0)
        acc_ref[...] = acc

        @pl.when(c == pl.num_programs(2) - 1)
        def _():
            a_new = _pad_rows(mix(jnp.exp(s_new - m) * inv_l), 8)
            o = (jnp.sum(acc, axis=0, keepdims=True)
                 + _dot(a_new.astype(BF16), expand)[0:1] * new_ref[:, hw:2 * hw].astype(BF16).astype(F32))
            for h in range(DIFF_HEADS):
                oh = o[:, h * LANES:(h + 1) * LANES]
                o_ref[:, h * LANES:(h + 1) * LANES] = _rms(oh, ng_ref[...], DIFF_NORM_EPS) * (1.0 - lam_init)


def diff_attend_sample(page_table, qt, new_rows, lam_vec, norm_g, cache, lam_init):
    bs, n_pages = page_table.shape
    p = DIFF_PAGES_PER_STEP
    hw = DIFF_HEADS * LANES
    page_specs = [pl.BlockSpec((None, PAGE_SIZE, hw),
                               functools.partial(lambda i, ph, c, pt, k: (pt[i, c * p + k], 0, ph), k=k))
                  for k in range(p)]
    grid_spec = pltpu.PrefetchScalarGridSpec(
        num_scalar_prefetch=1,
        grid=(bs, 2, n_pages // p),
        in_specs=[pl.BlockSpec((None, 2 * DIFF_HEADS, hw), lambda i, ph, c, pt: (i, 0, 0)),
                  pl.BlockSpec((None, 1, 2 * hw), lambda i, ph, c, pt: (i, 0, 0)),
                  pl.BlockSpec(lam_vec.shape, lambda i, ph, c, pt: (0, 0)),
                  pl.BlockSpec((1, LANES), lambda i, ph, c, pt: (0, 0))] + page_specs,
        out_specs=pl.BlockSpec((None, 1, hw), lambda i, ph, c, pt: (i, 0, 0)),
        scratch_shapes=[pltpu.VMEM((n_pages * PAGE_SIZE, LANES), F32), pltpu.VMEM((1, LANES), F32),
                        pltpu.VMEM((1, LANES), F32), pltpu.VMEM((8, hw), F32)],
    )
    return pl.pallas_call(
        functools.partial(_diff_sample_kernel, lam_init=lam_init),
        grid_spec=grid_spec,
        out_shape=jax.ShapeDtypeStruct((bs, 1, hw), F32),
        compiler_params=_cparams("parallel", "arbitrary", "arbitrary"),
        name="diff_attend_sample",
    )(page_table, qt, new_rows, lam_vec, norm_g.reshape(1, LANES), *([cache] * p))


def _dil_sample_kernel(qt_ref, new0_ref, new1_ref, new2_ref, st0_ref, st1_ref, st2_ref, o_ref):
    hw = DIL_HEADS * HEAD_DIM
    expand = _head_expand(DIL_HEADS, HEAD_DIM)
    outs, lses = [], []
    for g, (new_ref, st_ref) in enumerate(((new0_ref, st0_ref), (new1_ref, st1_ref), (new2_ref, st2_ref))):
        qt = _pad_rows(qt_ref[g], LANES)
        s = _dot_nt(st_ref[:, 0:hw].astype(BF16), qt)
        s_new = _dot_nt(_pad_rows(new_ref[:, 0:hw], 8).astype(BF16), qt)[0:1]
        m = jnp.maximum(jnp.max(s, axis=0, keepdims=True), s_new)
        e = jnp.exp(s - m)
        e_new = jnp.exp(s_new - m)
        den = jnp.sum(e, axis=0, keepdims=True) + e_new
        w = _dot((e / den).astype(BF16), expand) * st_ref[:, hw:2 * hw].astype(BF16).astype(F32)
        o = (jnp.sum(w, axis=0, keepdims=True)
             + _dot(_pad_rows(e_new / den, 8).astype(BF16), expand)[0:1]
             * new_ref[:, hw:2 * hw].astype(BF16).astype(F32))
        outs.append(o)
        lses.append(jnp.log(den) + m)
    mx = jnp.maximum(jnp.maximum(lses[0], lses[1]), lses[2])
    es = [jnp.exp(l - mx) for l in lses]
    tot = es[0] + es[1] + es[2]
    mix = jnp.zeros((1, hw), F32)
    for g in range(3):
        hi, mid, lo = _split3(_pad_rows(es[g] / tot, 8))
        alpha = (_dot(hi, expand) + _dot(mid, expand) + _dot(lo, expand))[0:1]
        mix = mix + alpha * outs[g]
    o_ref[...] = mix


def dil_attend_sample(qt, news, states):
    bs = qt.shape[0]
    hw = DIL_HEADS * HEAD_DIM
    st_specs, st_views = [], []
    for (win, dil), st in zip(DIL_GROUPS, states):
        st_views.append(st.reshape(bs, win // dil, dil * 2 * hw))
        st_specs.append(pl.BlockSpec((None, win // dil, 2 * hw), lambda i: (i, 0, 0)))
    return pl.pallas_call(
        _dil_sample_kernel,
        grid=(bs,),
        in_specs=[pl.BlockSpec((None, 3, DIL_HEADS, hw), lambda i: (i, 0, 0, 0))]
                 + [pl.BlockSpec((None, 1, 2 * hw), lambda i: (i, 0, 0))] * 3 + st_specs,
        out_specs=pl.BlockSpec((None, 1, hw), lambda i: (i, 0, 0)),
        out_shape=jax.ShapeDtypeStruct((bs, 1, hw), F32),
        compiler_params=_cparams("parallel"),
        name="dil_attend_sample",
    )(qt, *news, *st_views)


def _diff_qt(q_diff):
    bs = q_diff.shape[0]
    qd = q_diff.reshape(bs, DIFF_HEADS, 2, LANES).transpose(0, 2, 1, 3)
    eye = jnp.eye(DIFF_HEADS, dtype=q_diff.dtype)
    return (qd[:, :, :, None, :] * eye[None, None, :, :, None]).reshape(bs, 2 * DIFF_HEADS, DIFF_HEADS * LANES)


def _dil_qt(q_dil):
    bs = q_dil.shape[0]
    q = q_dil.reshape(bs, len(DIL_GROUPS), DIL_HEADS, LANES)
    onehot = (jnp.arange(DIL_HEADS)[:, None] // 2 == jnp.arange(DIL_HEADS // 2)[None, :]).astype(q.dtype)
    return (q[:, :, :, None, :] * onehot[None, None, :, :, None]).reshape(
        bs, len(DIL_GROUPS), DIL_HEADS, DIL_HEADS // 2 * LANES)


def ab_mix_sample(hs, g_in, w_ab, cmp_prep, lam_vec, dn_g, lam_init, cos, sin,
                  cache_nsa, cache_diff, win_state, page_table):
    bs = hs.shape[0]
    q_nsa, rows_nsa, _, rows_win, _, q_diff, rows_diff, _, gate = project(
        hs, g_in, cos, sin, w_ab, AB_PLAN, AB_OUT_DEFS)
    n_pages = page_table.shape[1]
    past = n_pages * PAGE_SIZE
    assert past // SLC_BLOCK == LANES, "selection-block axis is laid out on the 128 lanes"
    n_pool = cache_nsa.shape[0]
    cache_n = cache_nsa.reshape(n_pool, PAGE_SIZE, -1)
    cache_d = cache_diff.reshape(n_pool, PAGE_SIZE, -1)
    pe4, wa, wb = cmp_prep
    cmp_kv = nsa_compress_sample(cache_n, page_table, pe4, wa, wb)
    q8 = q_nsa.reshape(bs, NSA_HEADS, LANES)
    o_cmp, imp = nsa_sample_cmp(q8, cmp_kv, _cmp_to_block(past // CMP_STRIDE), past)
    sel_idx = topk_lanes(imp, N_SEL_CACHE)[:, :NSA_KV_HEADS, :N_SEL_CACHE].reshape(bs, -1)
    new_rows = jnp.concatenate([rows_nsa, rows_win], axis=-1).reshape(bs, 1, -1)
    o8 = nsa_sample_attend(page_table, sel_idx, q8, gate.reshape(bs, 1, LANES), o_cmp, new_rows,
                           win_state.reshape(bs, win_state.shape[1], -1), cache_n)
    o8 = o8.reshape(bs, NSA_HEADS, 2, HEAD_DIM)
    o_nsa = jnp.concatenate([o8[:, :NSA_GROUP, 0], o8[:, NSA_GROUP:, 1]], axis=1).reshape(bs, -1)
    o_diff = diff_attend_sample(page_table, _diff_qt(q_diff), rows_diff.reshape(bs, 1, -1), lam_vec, dn_g,
                                cache_d, lam_init).reshape(bs, -1)
    mixed = jnp.concatenate([o_nsa, o_diff], axis=-1).astype(BF16)
    return mixed, rows_nsa, rows_win, rows_diff


def dil_mix_sample(hs, g_in, w_c, cos, sin, states):
    bs = hs.shape[0]
    outs = project(hs, g_in, cos, sin, w_c, C_PLAN, C_OUT_DEFS)
    news = [x.reshape(bs, 1, -1) for x in outs[2:]]
    sts = []
    for (win, dil), st in zip(DIL_GROUPS, states):
        assert st.shape[1] == win, "state buffer must hold the full dilated window"
        sts.append(st.reshape(bs, win, -1))
    o = dil_attend_sample(_dil_qt(outs[0]), news, sts).reshape(bs, -1)
    return o.astype(BF16), outs[2:]


def kernel(x_prompt, x_sample, cache_nsa, cache_diff, state_nsa_win, state_dil_0, state_dil_1, state_dil_2,
           page_table, norm_g, ffn_w_in, ffn_w_out, w_in_ab, w_out_ab, nsa_w_cmp, nsa_pe_cmp, diff_lambda,
           diff_norm_g, w_in_c, w_out_c):
    b, t, d = x_prompt.shape
    bs, ns, _ = x_sample.shape
    assert ns == 1, "sample group is one new token per sequence"
    depth = norm_g.shape[0]
    past = page_table.shape[1] * PAGE_SIZE
    hp = x_prompt.reshape(b * t, d)
    hs = x_sample.reshape(bs, d)
    cos_p, sin_p = _rope_tables(jnp.tile(jnp.arange(t, dtype=jnp.int32), b))
    cos_s, sin_s = _rope_tables(jnp.full((bs,), past, jnp.int32))
    w_ffn_in = ffn_w_in.astype(BF16)
    w_ffn_out = ffn_w_out.astype(BF16)
    state_dil = (state_dil_0, state_dil_1, state_dil_2)
    nsa_p, nsa_s, win_p, win_s, diff_p, diff_s = [], [], [], [], [], []
    dil_p = [[] for _ in DIL_GROUPS]
    dil_s = [[] for _ in DIL_GROUPS]
    for layer in range(depth):
        g = norm_g[layer]
        hp = ffn_half(hp, g[0], g[1], w_ffn_in[layer, 0], w_ffn_out[layer, 0])
        hs = ffn_half(hs, g[0], g[1], w_ffn_in[layer, 0], w_ffn_out[layer, 0])
        if layer % 2 == 0:
            e = layer // 2
            lam_init = 0.8 - 0.6 * math.exp(-0.3 * layer)
            w_ab = _prep_w_ab(w_in_ab[e])
            cmp_prep = _prep_cmp(nsa_w_cmp[e], nsa_pe_cmp[e])
            mp, rn, rw, rd = ab_mix_prompt(hp, b, t, g[2], w_ab, cmp_prep, diff_lambda[e], diff_norm_g[e],
                                           lam_init, cos_p, sin_p)
            nsa_p.append(rn.reshape(b, t, 4, NSA_KV_HEADS, HEAD_DIM))
            win_p.append(rw.reshape(b, t, 2, NSA_KV_HEADS, HEAD_DIM)[:, -min(NSA_WINDOW, t):])
            diff_p.append(rd.reshape(b, t, 2, DIFF_HEADS, 2 * HEAD_DIM))
            ms, rn, rw, rd = ab_mix_sample(hs, g[2], w_ab, cmp_prep, diff_lambda[e], diff_norm_g[e], lam_init,
                                           cos_s, sin_s, cache_nsa[e], cache_diff[e], state_nsa_win[e], page_table)
            nsa_s.append(rn.reshape(bs, 1, 4, NSA_KV_HEADS, HEAD_DIM))
            win_full = jnp.concatenate([state_nsa_win[e], rw.reshape(bs, 1, 2, NSA_KV_HEADS, HEAD_DIM)], axis=1)
            win_s.append(win_full[:, -min(NSA_WINDOW, win_full.shape[1]):])
            diff_s.append(rd.reshape(bs, 1, 2, DIFF_HEADS, 2 * HEAD_DIM))
            w_o = w_out_ab[e].astype(BF16)
            hp = outproj(hp, mp, w_o, g[3])
            hs = outproj(hs, ms, w_o, g[3])
        else:
            o = layer // 2
            w_c = _prep_w_c(w_in_c[o])
            w_o = w_out_c[o].astype(BF16)
            os_, lses, rows = dil_mix_prompt(hp, b, t, g[2], w_c, cos_p, sin_p)
            hp = outproj_dil(hp, os_, lses, w_o, g[3])
            ms, news = dil_mix_sample(hs, g[2], w_c, cos_s, sin_s, [st[o] for st in state_dil])
            hs = outproj(hs, ms, w_o, g[3])
            for gi, (win, dil) in enumerate(DIL_GROUPS):
                dil_p[gi].append(rows[gi].reshape(b, t, 2, DIL_HEADS, HEAD_DIM)[:, -min(win, t):])
                full = jnp.concatenate([state_dil[gi][o], news[gi].reshape(bs, 1, 2, DIL_HEADS, HEAD_DIM)], axis=1)
                dil_s[gi].append(full[:, -min(win, full.shape[1]):])
        hp = ffn_half(hp, g[4], g[5], w_ffn_in[layer, 1], w_ffn_out[layer, 1])
        hs = ffn_half(hs, g[4], g[5], w_ffn_in[layer, 1], w_ffn_out[layer, 1])
    return (hp.reshape(b, t, d), hs.reshape(bs, 1, d), jnp.stack(nsa_p), jnp.stack(nsa_s), jnp.stack(win_p),
            jnp.stack(win_s), jnp.stack(diff_p), jnp.stack(diff_s), jnp.stack(dil_p[0]), jnp.stack(dil_s[0]),
            jnp.stack(dil_p[1]), jnp.stack(dil_s[1]), jnp.stack(dil_p[2]), jnp.stack(dil_s[2]))
```

```python
import functools
import math

import jax
import jax.numpy as jnp
import numpy as np
from jax import lax
from jax.experimental import pallas as pl
from jax.experimental.pallas import tpu as pltpu

F32 = jnp.float32
BF16 = jnp.bfloat16

LANES = 128
SUBLANES = 8
HEAD_DIM = 64
ROPE_THETA = 10000.0
NORM_EPS = 1e-6
PAGE_SIZE = 128
NSA_HEADS = 8
NSA_KV_HEADS = 2
NSA_GROUP = NSA_HEADS // NSA_KV_HEADS
CMP_LEN = 32
CMP_STRIDE = 16
SLC_BLOCK = 64
N_SEL = 16
NSA_WINDOW = 512
FORCED_BLOCK_SCORE = 1.0e4
DIFF_HEADS = 4
DIFF_NORM_EPS = 1e-5
DIL_GROUPS = ((128, 1), (512, 4), (2048, 16))
DIL_HEADS = 8
DIL_BAND = 128
NEG_BIG = -1.0e30
VMEM_LIMIT_BYTES = 56 * 1024 * 1024
TOKEN_TILE = 512


def _cparams(*sem):
    return pltpu.CompilerParams(dimension_semantics=sem, vmem_limit_bytes=VMEM_LIMIT_BYTES)


def _const_spec(shape):
    nd = len(shape)
    return pl.BlockSpec(shape, lambda *_: (0,) * nd, pipeline_mode=pl.Buffered(1))


def _rms(x, g, eps):
    return x * lax.rsqrt(jnp.mean(x * x, axis=-1, keepdims=True) + eps) * g


def _dot(a, b):
    return jnp.dot(a, b, preferred_element_type=F32)


def _dot_nt(a, b):
    return lax.dot_general(a, b, (((1,), (1,)), ((), ())), preferred_element_type=F32)


def _split3(x):
    hi = x.astype(BF16)
    r1 = x - hi.astype(F32)
    mid = r1.astype(BF16)
    lo = (r1 - mid.astype(F32)).astype(BF16)
    return hi, mid, lo


def _pad_rows(x, rows):
    return jnp.concatenate([x, jnp.zeros((rows - x.shape[0], x.shape[1]), x.dtype)], axis=0)


def _token_tile(n):
    return TOKEN_TILE if n % TOKEN_TILE == 0 else n


FFN_CHUNK = 256


def _ffn_kernel(x_ref, gpre_ref, gpost_ref, win_ref, wout_ref, o_ref, *, d_ff):
    x = x_ref[...]
    xn = _rms(x, gpre_ref[...], NORM_EPS).astype(BF16)
    acc = jnp.zeros(x.shape, F32)
    for c in range(d_ff // FFN_CHUNK):
        lo = c * FFN_CHUNK
        gate = _dot(xn, win_ref[:, lo:lo + FFN_CHUNK])
        up = _dot(xn, win_ref[:, d_ff + lo:d_ff + lo + FFN_CHUNK])
        act = (gate * jax.nn.sigmoid(gate) * up).astype(BF16)
        acc = acc + _dot(act, wout_ref[lo:lo + FFN_CHUNK, :])
    o_ref[...] = x + 0.5 * _rms(acc, gpost_ref[...], NORM_EPS)


def ffn_half(h, g_pre, g_post, w_in, w_out):
    n, d = h.shape
    d_ff = w_out.shape[0]
    tm = _token_tile(n)
    return pl.pallas_call(
        functools.partial(_ffn_kernel, d_ff=d_ff),
        grid=(n // tm,),
        in_specs=[pl.BlockSpec((tm, d), lambda i: (i, 0)),
                  _const_spec((1, d)), _const_spec((1, d)),
                  _const_spec(w_in.shape), _const_spec(w_out.shape)],
        out_specs=pl.BlockSpec((tm, d), lambda i: (i, 0)),
        out_shape=jax.ShapeDtypeStruct((n, d), F32),
        compiler_params=_cparams("parallel"),
        name="ffn_half",
    )(h, g_pre.reshape(1, d), g_post.reshape(1, d), w_in, w_out)


def _rope_slab(y, cos, sin):
    lane = lax.broadcasted_iota(jnp.int32, y.shape, 1)
    swapped = jnp.where(lane % HEAD_DIM < HEAD_DIM // 2,
                        pltpu.roll(y, LANES - HEAD_DIM // 2, 1),
                        pltpu.roll(y, HEAD_DIM // 2, 1))
    return y * cos + swapped * sin


def _proj_kernel(x_ref, g_ref, cos_ref, sin_ref, w_ref, *refs, plan, out_defs, first_tiles, tiles_per_b):
    out_refs = refs[:len(out_defs)]
    scr_ref = refs[len(out_defs)]
    tm = x_ref.shape[0]
    tile_in_b = pl.program_id(0) % tiles_per_b
    xn = _rms(x_ref[...], g_ref[...], NORM_EPS).astype(BF16)
    cos = cos_ref[...]
    sin = sin_ref[...]
    for col0, nslab, rope, scale, dests in plan:
        y = _dot(xn, w_ref[:, col0:col0 + nslab * LANES])
        for j in range(nslab):
            ys = y[:, j * LANES:(j + 1) * LANES]
            if rope:
                ys = _rope_slab(ys, cos, sin)
            if scale != 1.0:
                ys = ys * scale
            for out_idx, slab in dests[j]:
                ref = out_refs[out_idx]
                kind = out_defs[out_idx][0]
                cs = slice(slab * LANES, (slab + 1) * LANES)
                if kind == "N":
                    ref[:, cs] = ys.astype(ref.dtype)
                elif kind == "T":
                    def write_t(ref=ref, cs=cs, ys=ys):
                        ref[cs, :] = ys.T.astype(ref.dtype)
                    if first_tiles[out_idx] == 0:
                        write_t()
                    else:
                        pl.when(tile_in_b >= first_tiles[out_idx])(write_t)
                else:
                    dil = out_defs[out_idx][3]
                    if dil == 1:
                        ref[0, :, cs] = ys.astype(ref.dtype)
                    else:
                        scr_ref[...] = ys
                        for r in range(dil):
                            ref[r, :, cs] = scr_ref[pl.ds(r, tm // dil, stride=dil), :].astype(ref.dtype)


def project(h, b, g, cos, sin, w, plan, out_defs):
    n, d = h.shape
    t = n // b
    tm = _token_tile(t)
    tpb = t // tm
    specs, shapes, first_tiles = [], [], []
    for od in out_defs:
        kind, c, dt = od[:3]
        if kind == "N":
            specs.append(pl.BlockSpec((tm, c), lambda i: (i, 0)))
            shapes.append(jax.ShapeDtypeStruct((n, c), dt))
            first_tiles.append(0)
        elif kind == "T":
            keep = max(min(od[3], t), tm)
            ft = (t - keep) // tm
            specs.append(pl.BlockSpec((None, c, tm),
                                      functools.partial(lambda i, ft: (i // tpb, 0, jnp.maximum(i % tpb - ft, 0)),
                                                        ft=ft)))
            shapes.append(jax.ShapeDtypeStruct((b, c, keep), dt))
            first_tiles.append(ft)
        else:
            dil = od[3]
            specs.append(pl.BlockSpec((None, dil, tm // dil, c), lambda i: (i // tpb, 0, i % tpb, 0)))
            shapes.append(jax.ShapeDtypeStruct((b, dil, t // dil, c), dt))
            first_tiles.append(0)
    return pl.pallas_call(
        functools.partial(_proj_kernel, plan=plan, out_defs=out_defs, first_tiles=tuple(first_tiles),
                          tiles_per_b=tpb),
        grid=(n // tm,),
        in_specs=[pl.BlockSpec((tm, d), lambda i: (i, 0)), _const_spec((1, d)),
                  pl.BlockSpec((tm, LANES), lambda i: (i, 0)),
                  pl.BlockSpec((tm, LANES), lambda i: (i, 0)),
                  _const_spec(w.shape)],
        out_specs=specs,
        out_shape=shapes,
        scratch_shapes=[pltpu.VMEM((tm, LANES), F32)],
        compiler_params=_cparams("arbitrary"),
        name="project",
    )(h, g.reshape(1, d), cos, sin, w)


def _outproj_kernel(h_ref, *refs):
    w_ref, g_ref, o_ref = refs[-3:]
    y, row0 = None, 0
    for m_ref in refs[:-3]:
        c = m_ref.shape[1]
        part = _dot(m_ref[...], w_ref[row0:row0 + c, :])
        y = part if y is None else y + part
        row0 += c
    o_ref[...] = h_ref[...] + _rms(y, g_ref[...], NORM_EPS)


def outproj(h, ms, w, g):
    n, d = h.shape
    tm = _token_tile(n)
    return pl.pallas_call(
        _outproj_kernel,
        grid=(n // tm,),
        in_specs=[pl.BlockSpec((tm, d), lambda i: (i, 0))]
                 + [pl.BlockSpec((tm, m.shape[1]), lambda i: (i, 0)) for m in ms]
                 + [_const_spec(w.shape), _const_spec((1, d))],
        out_specs=pl.BlockSpec((tm, d), lambda i: (i, 0)),
        out_shape=jax.ShapeDtypeStruct((n, d), F32),
        compiler_params=_cparams("parallel"),
        name="outproj",
    )(h, *ms, w, g.reshape(1, d))


def _outproj_dil_kernel(h_ref, *refs):
    ng = len(DIL_GROUPS)
    w_ref, g_ref, o_ref, scr_ref = refs[2 * ng:]
    tm = h_ref.shape[0]
    nslab = DIL_HEADS * HEAD_DIM // LANES
    vals = []
    k = 0
    for gi, (_, dil) in enumerate(DIL_GROUPS):
        per_g = []
        for ref in (refs[2 * gi], refs[2 * gi + 1]):
            slabs = []
            for s in range(nslab):
                cs = slice(s * LANES, (s + 1) * LANES)
                if dil == 1:
                    slabs.append(ref[0, :, cs])
                else:
                    for r in range(dil):
                        scr_ref[k, pl.ds(r, tm // dil, stride=dil), :] = ref[r, :, cs]
                    slabs.append(scr_ref[k])
                    k += 1
            per_g.append(slabs)
        vals.append(per_g)
    mixed = []
    for s in range(nslab):
        l0, l1, l2 = vals[0][1][s], vals[1][1][s], vals[2][1][s]
        mx = jnp.maximum(jnp.maximum(l0, l1), l2)
        e0, e1, e2 = jnp.exp(l0 - mx), jnp.exp(l1 - mx), jnp.exp(l2 - mx)
        den = e0 + e1 + e2
        mixed.append(((e0 / den) * vals[0][0][s] + (e1 / den) * vals[1][0][s]
                      + (e2 / den) * vals[2][0][s]).astype(BF16))
    y = _dot(jnp.concatenate(mixed, axis=1), w_ref[...])
    o_ref[...] = h_ref[...] + _rms(y, g_ref[...], NORM_EPS)


def outproj_dil(h, b, outs, lses, w, g):
    n, d = h.shape
    t = n // b
    tm = _token_tile(t)
    tpb = t // tm
    c = DIL_HEADS * HEAD_DIM
    specs, args, n_scr = [], [], 0
    for (_, dil), o, l in zip(DIL_GROUPS, outs, lses):
        spec = pl.BlockSpec((None, dil, tm // dil, c), lambda i: (i // tpb, 0, i % tpb, 0))
        specs += [spec, spec]
        args += [o, l]
        if dil > 1:
            n_scr += 2 * (c // LANES)
    return pl.pallas_call(
        _outproj_dil_kernel,
        grid=(n // tm,),
        in_specs=[pl.BlockSpec((tm, d), lambda i: (i, 0))] + specs + [_const_spec(w.shape), _const_spec((1, d))],
        out_specs=pl.BlockSpec((tm, d), lambda i: (i, 0)),
        out_shape=jax.ShapeDtypeStruct((n, d), F32),
        scratch_shapes=[pltpu.VMEM((n_scr, tm, LANES), F32)],
        compiler_params=_cparams("parallel"),
        name="outproj_dil",
    )(h, *args, w, g.reshape(1, d))


def _compress_rows(k_ref, v_ref, pe_ref, wa_ref, wb_ref, nsub):
    half = CMP_LEN // 2
    acc_a = jnp.zeros((nsub, 2 * LANES), F32)
    acc_b = jnp.zeros((nsub, 2 * LANES), F32)
    for l in range(half):
        x = jnp.concatenate([k_ref[pl.ds(l, nsub, stride=CMP_STRIDE), :],
                             v_ref[pl.ds(l, nsub, stride=CMP_STRIDE), :]], axis=1)
        acc_a = acc_a + _dot((x + pe_ref[l:l + 1, :]).astype(BF16), wa_ref[l])
        acc_b = acc_b + _dot((x + pe_ref[half + l:half + l + 1, :]).astype(BF16), wb_ref[l])
    comp = acc_a + pltpu.roll(acc_b, nsub - 1, 0)
    row = lax.broadcasted_iota(jnp.int32, comp.shape, 0)
    return jnp.where(row < nsub - 1, comp, 0.0)


def _compress_kernel(k_ref, v_ref, pe_ref, wa_ref, wb_ref, o_ref, *, nsub):
    o_ref[...] = _compress_rows(k_ref, v_ref, pe_ref, wa_ref, wb_ref, nsub).astype(o_ref.dtype)


def nsa_compress_prompt(cmp_rows, pe4, wa, wb):
    b, t, _ = cmp_rows.shape
    nsub = t // CMP_STRIDE
    return pl.pallas_call(
        functools.partial(_compress_kernel, nsub=nsub),
        grid=(b,),
        in_specs=[pl.BlockSpec((None, t, LANES), lambda i: (i, 0, 0)),
                  pl.BlockSpec((None, t, LANES), lambda i: (i, 0, 1)),
                  _const_spec(pe4.shape), _const_spec(wa.shape), _const_spec(wb.shape)],
        out_specs=pl.BlockSpec((None, nsub, 2 * LANES), lambda i: (i, 0, 0)),
        out_shape=jax.ShapeDtypeStruct((b, nsub, 2 * LANES), BF16),
        compiler_params=_cparams("parallel"),
        name="nsa_compress_prompt",
    )(cmp_rows, cmp_rows, pe4, wa, wb)


NSA_TQ = 128
NSA_TK = 512


def _softmax_rows(s, mask):
    s = jnp.where(mask, s, -jnp.inf)
    m = jnp.max(s, axis=-1, keepdims=True)
    m = jnp.where(m == -jnp.inf, 0.0, m)
    e = jnp.exp(s - m)
    den = jnp.sum(e, axis=-1, keepdims=True)
    return e / jnp.where(den > 0, den, 1.0)


def _topk_mask_t(score_t, k):
    j_io = lax.broadcasted_iota(jnp.int32, score_t.shape, 0)
    nj = score_t.shape[0]
    sel = jnp.zeros(score_t.shape, F32)
    for _ in range(k):
        m = jnp.max(score_t, axis=0, keepdims=True)
        cand = jnp.where(score_t == m, j_io, nj)
        jmin = jnp.min(cand, axis=0, keepdims=True)
        hit = j_io == jmin
        sel = jnp.where(jnp.logical_and(hit, m > -jnp.inf), 1.0, sel)
        score_t = jnp.where(hit, -jnp.inf, score_t)
    return sel


def _flash_step(q, kt, vt, carry, mask):
    m, l, acc = carry
    sc = _dot(q, kt)
    if mask is not None:
        sc = jnp.where(mask, sc, NEG_BIG)
    m_new = jnp.maximum(m, jnp.max(sc, axis=-1, keepdims=True))
    alpha = jnp.exp(m - m_new)
    pe = jnp.exp(sc - m_new)
    l = alpha * l + jnp.sum(pe, axis=-1, keepdims=True)
    acc = alpha * acc + _dot_nt(pe.astype(BF16), vt)
    return m_new, l, acc


def _nsa_prompt_kernel(q_ref, gate_ref, cmp_ref, slc_ref, win_ref, eall_ref, selt_ref, o_ref):
    tq = NSA_TQ
    qs = pl.program_id(1) * tq
    rows = NSA_GROUP * tq
    qpos = qs + lax.broadcasted_iota(jnp.int32, (rows, 1), 0) % tq
    ncmp = cmp_ref.shape[0]
    gates = jax.nn.sigmoid(gate_ref[...])
    lane = lax.broadcasted_iota(jnp.int32, (tq, LANES), 1)
    head_out = []
    for kvh in range(NSA_KV_HEADS):
        q4 = jnp.concatenate(
            [q_ref[:, (NSA_GROUP * kvh + g) * LANES:(NSA_GROUP * kvh + g + 1) * LANES]
             for g in range(NSA_GROUP)], axis=0)
        s = _dot_nt(q4, cmp_ref[:, 0:LANES])
        cmp_end = lax.broadcasted_iota(jnp.int32, (1, ncmp), 1) * CMP_STRIDE + (CMP_LEN - 1)
        p = _softmax_rows(s, cmp_end <= qpos)
        o_cmp = _dot(p.astype(BF16), cmp_ref[:, LANES:2 * LANES])
        psum = p[0:tq] + p[tq:2 * tq] + p[2 * tq:3 * tq] + p[3 * tq:4 * tq]
        hi, mid, lo = _split3(psum)
        selt = selt_ref[...]
        imp_t = _dot_nt(selt, hi) + _dot_nt(selt, mid) + _dot_nt(selt, lo)
        j_io = lax.broadcasted_iota(jnp.int32, imp_t.shape, 0)
        cur = (qs + lax.broadcasted_iota(jnp.int32, imp_t.shape, 1)) // SLC_BLOCK
        forced = jnp.logical_or(j_io == 0, j_io == cur)
        score_t = jnp.where(j_io <= cur, jnp.where(forced, FORCED_BLOCK_SCORE, imp_t), -jnp.inf)
        sel_q = _topk_mask_t(score_t, N_SEL).T
        bias = jnp.where(sel_q > 0.5, 0.0, NEG_BIG).astype(BF16)
        qext = jnp.concatenate([q4, jnp.concatenate([bias] * NSA_GROUP, axis=0)], axis=1)

        def sel_step(t, carry, masked):
            ks = pl.multiple_of(t * NSA_TK, NSA_TK)
            kext = jnp.concatenate([slc_ref[0:LANES, pl.ds(ks, NSA_TK)], eall_ref[:, pl.ds(ks, NSA_TK)]], axis=0)
            mask = None
            if masked:
                mask = ks + lax.broadcasted_iota(jnp.int32, (1, NSA_TK), 1) <= qpos
            return _flash_step(qext, kext, slc_ref[LANES:2 * LANES, pl.ds(ks, NSA_TK)], carry, mask)

        n_full = qs // NSA_TK
        init = (jnp.full((rows, 1), NEG_BIG, F32), jnp.zeros((rows, 1), F32), jnp.zeros((rows, LANES), F32))
        carry = lax.fori_loop(0, n_full, functools.partial(sel_step, masked=False), init)
        _, l_sel, acc_sel = sel_step(n_full, carry, True)
        o_sel = acc_sel / l_sel
        wlen = NSA_WINDOW + tq
        ws = pl.multiple_of(jnp.maximum(qs - NSA_WINDOW, 0), tq)
        sw = _dot(q4, win_ref[0:LANES, pl.ds(ws, wlen)])
        dist = qpos - (ws + lax.broadcasted_iota(jnp.int32, (1, wlen), 1))
        pw = _softmax_rows(sw, jnp.logical_and(dist >= 0, dist < NSA_WINDOW))
        o_win = _dot_nt(pw.astype(BF16), win_ref[LANES:2 * LANES, pl.ds(ws, wlen)])
        for g in range(NSA_GROUP):
            h = NSA_GROUP * kvh + g
            r = slice(g * tq, (g + 1) * tq)
            head_out.append(gates[:, 3 * h:3 * h + 1] * o_cmp[r] + gates[:, 3 * h + 1:3 * h + 2] * o_sel[r]
                            + gates[:, 3 * h + 2:3 * h + 3] * o_win[r])
    for pair in range(NSA_HEADS // 2):
        a, b = head_out[2 * pair], head_out[2 * pair + 1]
        if (2 * pair) // NSA_GROUP == 0:
            slab = jnp.where(lane < HEAD_DIM, a, pltpu.roll(b, HEAD_DIM, 1))
        else:
            slab = jnp.where(lane < HEAD_DIM, pltpu.roll(a, HEAD_DIM, 1), b)
        o_ref[:, pair * LANES:(pair + 1) * LANES] = slab.astype(o_ref.dtype)


def nsa_attend_prompt(q_nsa, gate, cmp_kv, slc_t, win_t, eall_t, selt):
    b, t, _ = q_nsa.shape
    ncmp = cmp_kv.shape[1]
    return pl.pallas_call(
        _nsa_prompt_kernel,
        grid=(b, t // NSA_TQ),
        in_specs=[pl.BlockSpec((None, NSA_TQ, NSA_HEADS * LANES), lambda i, j: (i, j, 0)),
                  pl.BlockSpec((None, NSA_TQ, LANES), lambda i, j: (i, j, 0)),
                  pl.BlockSpec((None, ncmp, 2 * LANES), lambda i, j: (i, 0, 0)),
                  pl.BlockSpec((None, 2 * LANES, t), lambda i, j: (i, 0, 0)),
                  pl.BlockSpec((None, 2 * LANES, t), lambda i, j: (i, 0, 0)),
                  _const_spec(eall_t.shape), _const_spec(selt.shape)],
        out_specs=pl.BlockSpec((None, NSA_TQ, NSA_HEADS * HEAD_DIM), lambda i, j: (i, j, 0)),
        out_shape=jax.ShapeDtypeStruct((b, t, NSA_HEADS * HEAD_DIM), BF16),
        compiler_params=_cparams("parallel", "parallel"),
        name="nsa_attend_prompt",
    )(q_nsa, gate, cmp_kv, slc_t, win_t, eall_t, selt)


DIFF_TQ = 256
DIFF_TK = 512


def _diff_lambda(lam_ref, lam_init):
    lv = lam_ref[...]
    a = jnp.sum(lv[0:1] * lv[1:2], axis=-1, keepdims=True)
    b = jnp.sum(lv[2:3] * lv[3:4], axis=-1, keepdims=True)
    return jnp.exp(a) - jnp.exp(b) + lam_init


def _diff_prompt_kernel(q_ref, k_ref, v_ref, lam_ref, ng_ref, o_ref, *, lam_init):
    tq = DIFF_TQ
    qs = pl.program_id(2) * tq
    q2 = jnp.concatenate([q_ref[:, 0:LANES], q_ref[:, LANES:2 * LANES]], axis=0)
    qpos = qs + lax.broadcasted_iota(jnp.int32, (2 * tq, 1), 0) % tq

    def step(t, carry, masked):
        m, l, acc = carry
        ks = pl.multiple_of(t * DIFF_TK, DIFF_TK)
        sc = _dot_nt(q2, k_ref[pl.ds(ks, DIFF_TK), :])
        if masked:
            sc = jnp.where(ks + lax.broadcasted_iota(jnp.int32, (1, DIFF_TK), 1) <= qpos, sc, NEG_BIG)
        m_new = jnp.maximum(m, jnp.max(sc, axis=-1, keepdims=True))
        alpha = jnp.exp(m - m_new)
        pe = jnp.exp(sc - m_new)
        l = alpha * l + jnp.sum(pe, axis=-1, keepdims=True)
        acc = alpha * acc + _dot(pe.astype(BF16), v_ref[pl.ds(ks, DIFF_TK), :])
        return m_new, l, acc

    n_full = qs // DIFF_TK
    init = (jnp.full((2 * tq, 1), NEG_BIG, F32), jnp.zeros((2 * tq, 1), F32), jnp.zeros((2 * tq, LANES), F32))
    carry = lax.fori_loop(0, n_full, functools.partial(step, masked=False), init)
    _, l, acc = step(n_full, carry, True)
    o = acc / l
    lam = _diff_lambda(lam_ref, lam_init)
    o = o[0:tq] - lam * o[tq:2 * tq]
    o_ref[...] = (_rms(o, ng_ref[...], DIFF_NORM_EPS) * (1.0 - lam_init)).astype(o_ref.dtype)


def diff_attend_prompt(q_diff, kv_diff16, lam_vec, norm_g, lam_init):
    b, t, _ = q_diff.shape
    return pl.pallas_call(
        functools.partial(_diff_prompt_kernel, lam_init=lam_init),
        grid=(b, DIFF_HEADS, t // DIFF_TQ),
        in_specs=[pl.BlockSpec((None, DIFF_TQ, 2 * LANES), lambda i, h, j: (i, j, h)),
                  pl.BlockSpec((None, t, LANES), lambda i, h, j: (i, 0, h)),
                  pl.BlockSpec((None, t, LANES), lambda i, h, j: (i, 0, DIFF_HEADS + h)),
                  _const_spec(lam_vec.shape), _const_spec((1, LANES))],
        out_specs=pl.BlockSpec((None, DIFF_TQ, LANES), lambda i, h, j: (i, j, h)),
        out_shape=jax.ShapeDtypeStruct((b, t, DIFF_HEADS * LANES), BF16),
        compiler_params=_cparams("parallel", "parallel", "parallel"),
        name="diff_attend_prompt",
    )(q_diff, kv_diff16, kv_diff16, lam_vec, norm_g.reshape(1, LANES))


def _dil_prompt_kernel(q_ref, kvp_ref, kvc_ref, o_ref, lse_ref):
    band = DIL_BAND
    hw = DIL_HEADS * HEAD_DIM
    first = pl.program_id(2) == 0
    qi = lax.broadcasted_iota(jnp.int32, (band, 2 * band), 0) + band
    kj = lax.broadcasted_iota(jnp.int32, (band, 2 * band), 1)
    rel = qi - kj
    ok = jnp.logical_and(rel >= 0, rel <= band)
    ok = jnp.logical_and(ok, jnp.logical_not(jnp.logical_and(first, kj < band)))
    lane = lax.broadcasted_iota(jnp.int32, (band, LANES), 1)
    for pair in range(DIL_HEADS // 2):
        cs = slice(pair * LANES, (pair + 1) * LANES)
        vs = slice(hw + pair * LANES, hw + (pair + 1) * LANES)
        k2 = jnp.concatenate([kvp_ref[:, cs], kvc_ref[:, cs]], axis=0)
        v2 = jnp.concatenate([kvp_ref[:, vs], kvc_ref[:, vs]], axis=0)
        outs, lses = [], []
        for hh in range(2):
            h = 2 * pair + hh
            s = _dot_nt(q_ref[:, h * LANES:(h + 1) * LANES], k2)
            s = jnp.where(ok, s, -jnp.inf)
            m = jnp.max(s, axis=-1, keepdims=True)
            e = jnp.exp(s - m)
            den = jnp.sum(e, axis=-1, keepdims=True)
            outs.append(_dot((e / den).astype(BF16), v2))
            lses.append(jnp.log(den) + m)
        o_ref[:, cs] = jnp.where(lane < HEAD_DIM, outs[0], outs[1])
        lse_ref[:, cs] = jnp.where(lane < HEAD_DIM, lses[0], lses[1])


def dil_attend_prompt(q_r, kv_r):
    b, dil, n, _ = q_r.shape
    hw = DIL_HEADS * HEAD_DIM
    blk = lambda w, prev: pl.BlockSpec(
        (None, None, DIL_BAND, w), (lambda i, r, u: (i, r, jnp.maximum(u - 1, 0), 0)) if prev
        else (lambda i, r, u: (i, r, u, 0)))
    return pl.pallas_call(
        _dil_prompt_kernel,
        grid=(b, dil, n // DIL_BAND),
        in_specs=[blk(DIL_HEADS * LANES, False), blk(2 * hw, True), blk(2 * hw, False)],
        out_specs=[blk(hw, False), blk(hw, False)],
        out_shape=[jax.ShapeDtypeStruct((b, dil, n, hw), F32)] * 2,
        compiler_params=_cparams("parallel", "parallel", "parallel"),
        name="dil_attend_prompt",
    )(q_r, kv_r, kv_r)


def _page_copies(cache_ref, pt_ref, buf_ref, sem_ref, bi, slot, n_pages):
    return [pltpu.make_async_copy(cache_ref.at[pt_ref[bi, j], pl.ds(0, 2 * LANES), :],
                                  buf_ref.at[slot, j], sem_ref.at[slot]) for j in range(n_pages)]


def _compress_sample_kernel(pt_ref, cache_ref, pe_ref, wa_ref, wb_ref, o_ref, page_ref, kbuf_ref, vbuf_ref,
                            sem_ref, *, n_pages):
    i = pl.program_id(0)
    slot = i % 2
    nsub = n_pages * PAGE_SIZE // CMP_STRIDE
    copies = functools.partial(_page_copies, cache_ref, pt_ref, page_ref, sem_ref, n_pages=n_pages)

    @pl.when(i == 0)
    def _():
        for cp in copies(0, 0):
            cp.start()

    @pl.when(i + 1 < pl.num_programs(0))
    def _():
        for cp in copies(i + 1, 1 - slot):
            cp.start()

    for cp in copies(i, slot):
        cp.wait()

    def to_rows(j, carry):
        r0 = pl.multiple_of(j * PAGE_SIZE, PAGE_SIZE)
        kbuf_ref[pl.ds(r0, PAGE_SIZE), :] = page_ref[slot, j, 0:LANES, :].T
        vbuf_ref[pl.ds(r0, PAGE_SIZE), :] = page_ref[slot, j, LANES:2 * LANES, :].T
        return carry

    lax.fori_loop(0, n_pages, to_rows, 0)
    o_ref[...] = _compress_rows(kbuf_ref, vbuf_ref, pe_ref, wa_ref, wb_ref, nsub).astype(o_ref.dtype)


def nsa_compress_sample(cache_t, page_table, pe4, wa, wb):
    bs, n_pages = page_table.shape
    past = n_pages * PAGE_SIZE
    nsub = past // CMP_STRIDE
    grid_spec = pltpu.PrefetchScalarGridSpec(
        num_scalar_prefetch=1,
        grid=(bs,),
        in_specs=[pl.BlockSpec(memory_space=pl.ANY),
                  pl.BlockSpec(pe4.shape, lambda i, pt: (0, 0)),
                  pl.BlockSpec(wa.shape, lambda i, pt: (0, 0, 0)),
                  pl.BlockSpec(wb.shape, lambda i, pt: (0, 0, 0))],
        out_specs=pl.BlockSpec((None, nsub, 2 * LANES), lambda i, pt: (i, 0, 0)),
        scratch_shapes=[pltpu.VMEM((2, n_pages, 2 * LANES, PAGE_SIZE), F32),
                        pltpu.VMEM((past, LANES), F32), pltpu.VMEM((past, LANES), F32),
                        pltpu.SemaphoreType.DMA((2,))],
    )
    return pl.pallas_call(
        functools.partial(_compress_sample_kernel, n_pages=n_pages),
        grid_spec=grid_spec,
        out_shape=jax.ShapeDtypeStruct((bs, nsub, 2 * LANES), BF16),
        compiler_params=_cparams("arbitrary"),
        name="nsa_compress_sample",
    )(page_table, cache_t, pe4, wa, wb)


def _group_sum_rows(x):
    parts = [jnp.sum(x[NSA_GROUP * k:NSA_GROUP * (k + 1)], axis=0, keepdims=True) for k in range(NSA_KV_HEADS)]
    return _pad_rows(jnp.concatenate(parts, axis=0), x.shape[0])


def _nsa_sample_cmp_kernel(q_ref, cmp_ref, selt_ref, ocmp_ref, imp_ref, *, qpos):
    q8 = q_ref[...]
    ncmp = cmp_ref.shape[0]
    s = _dot_nt(q8, cmp_ref[:, 0:LANES])
    cmp_end = lax.broadcasted_iota(jnp.int32, (1, ncmp), 1) * CMP_STRIDE + (CMP_LEN - 1)
    p = _softmax_rows(s, cmp_end <= qpos)
    ocmp_ref[...] = _dot(p.astype(BF16), cmp_ref[:, LANES:2 * LANES])
    hi, mid, lo = _split3(_group_sum_rows(p))
    selt = selt_ref[...]
    imp_ref[...] = _dot_nt(hi, selt) + _dot_nt(mid, selt) + _dot_nt(lo, selt)


def nsa_sample_cmp(q8, cmp_kv, selt, qpos):
    bs = q8.shape[0]
    ncmp = cmp_kv.shape[1]
    blk = pl.BlockSpec((None, NSA_HEADS, LANES), lambda i: (i, 0, 0))
    return pl.pallas_call(
        functools.partial(_nsa_sample_cmp_kernel, qpos=qpos),
        grid=(bs,),
        in_specs=[blk, pl.BlockSpec((None, ncmp, 2 * LANES), lambda i: (i, 0, 0)), _const_spec(selt.shape)],
        out_specs=[blk, blk],
        out_shape=[jax.ShapeDtypeStruct((bs, NSA_HEADS, LANES), F32)] * 2,
        compiler_params=_cparams("parallel"),
        name="nsa_sample_cmp",
    )(q8, cmp_kv, selt)


def _topk_lanes_kernel(imp_ref, idx_ref, *, k):
    score = imp_ref[...]
    lane = lax.broadcasted_iota(jnp.int32, score.shape, 1)
    lane_f = lane.astype(F32)
    score = jnp.where(lane == 0, FORCED_BLOCK_SCORE, score)
    idx = jnp.zeros(score.shape, F32)
    for r in range(k):
        m = jnp.max(score, axis=-1, keepdims=True)
        jmin = jnp.min(jnp.where(score == m, lane_f, float(LANES)), axis=-1, keepdims=True)
        idx = jnp.where(lane == r, jmin, idx)
        score = jnp.where(lane_f == jmin, -jnp.inf, score)
    idx_ref[...] = idx.astype(jnp.int32)


def topk_lanes(imp, k):
    bs = imp.shape[0]
    x = imp.reshape(bs * NSA_HEADS, LANES)
    out = pl.pallas_call(
        functools.partial(_topk_lanes_kernel, k=k),
        out_shape=jax.ShapeDtypeStruct(x.shape, jnp.int32),
        name="topk_lanes",
    )(x)
    return out.reshape(bs, NSA_HEADS, LANES)


N_SEL_CACHE = N_SEL - 1


def _sel_copies(cache_ref, pt_ref, sel_ref, buf_ref, sem_ref, bi, slot):
    cps = []
    for kvh in range(NSA_KV_HEADS):
        for r in range(N_SEL_CACHE):
            j = sel_ref[bi, kvh * N_SEL_CACHE + r]
            cps.append(pltpu.make_async_copy(
                cache_ref.at[pt_ref[bi, j // 2], pl.ds(2 * LANES, 2 * LANES), :],
                buf_ref.at[slot, kvh, r], sem_ref.at[slot]))
    return cps


def _pick_gate(gates8, branch):
    row = lax.broadcasted_iota(jnp.int32, gates8.shape, 0)
    lane = lax.broadcasted_iota(jnp.int32, gates8.shape, 1)
    return jnp.sum(jnp.where(lane == 3 * row + branch, gates8, 0.0), axis=-1, keepdims=True)


def _nsa_sample_attend_kernel(pt_ref, sel_ref, q_ref, gate_ref, ocmp_ref, new_ref, win_ref, cache_ref,
                              o_ref, buf_ref, sem_ref):
    i = pl.program_id(0)
    slot = i % 2
    copies = functools.partial(_sel_copies, cache_ref, pt_ref, sel_ref, buf_ref, sem_ref)

    @pl.when(i == 0)
    def _():
        for cp in copies(0, 0):
            cp.start()

    @pl.when(i + 1 < pl.num_programs(0))
    def _():
        for cp in copies(i + 1, 1 - slot):
            cp.start()

    q8 = q_ref[...]
    q8f = q8.astype(F32)
    row = lax.broadcasted_iota(jnp.int32, (NSA_HEADS, 1), 0)
    new = new_ref[...]
    rnd = lambda x: x.astype(BF16).astype(F32)

    def probs_with_new_key(s, k_new, mask):
        s_new = jnp.sum(q8f * rnd(k_new), axis=-1, keepdims=True)
        s = jnp.where(mask, s, -jnp.inf)
        m = jnp.maximum(jnp.max(s, axis=-1, keepdims=True), s_new)
        e = jnp.exp(s - m)
        e_new = jnp.exp(s_new - m)
        den = jnp.sum(e, axis=-1, keepdims=True) + e_new
        return (e / den).astype(BF16), rnd(e_new / den)

    wb = win_ref.shape[1]
    widx = lax.broadcasted_iota(jnp.int32, (1, wb), 1)
    pw, pw_new = probs_with_new_key(_dot(q8, win_ref[0:LANES, :].astype(BF16)), new[:, 4 * LANES:5 * LANES],
                                    widx > wb - NSA_WINDOW)
    o_win = _dot_nt(pw, win_ref[LANES:2 * LANES, :].astype(BF16)) + pw_new * rnd(new[:, 5 * LANES:6 * LANES])
    for cp in copies(i, slot):
        cp.wait()
    lane = lax.broadcasted_iota(jnp.int32, (1, PAGE_SIZE), 1)
    o_sel = []
    for kvh in range(NSA_KV_HEADS):
        ss, masks = [], []
        for r in range(N_SEL_CACHE):
            ss.append(_dot(q8, buf_ref[slot, kvh, r, 0:LANES, :].astype(BF16)))
            masks.append(lane // SLC_BLOCK == sel_ref[i, kvh * N_SEL_CACHE + r] % 2)
        p, p_new = probs_with_new_key(jnp.concatenate(ss, axis=1), new[:, 2 * LANES:3 * LANES],
                                      jnp.concatenate(masks, axis=1))
        o = p_new * rnd(new[:, 3 * LANES:4 * LANES])
        for r in range(N_SEL_CACHE):
            o = o + _dot_nt(p[:, r * PAGE_SIZE:(r + 1) * PAGE_SIZE],
                            buf_ref[slot, kvh, r, LANES:2 * LANES, :].astype(BF16))
        o_sel.append(o)
    o_sel = jnp.where(row < NSA_GROUP, o_sel[0], o_sel[1])
    gates8 = jnp.broadcast_to(jax.nn.sigmoid(gate_ref[...]), (NSA_HEADS, LANES))
    o_ref[...] = (_pick_gate(gates8, 0) * ocmp_ref[...] + _pick_gate(gates8, 1) * o_sel
                  + _pick_gate(gates8, 2) * o_win)


def nsa_sample_attend(page_table, sel_idx, q8, gate, o_cmp, new_rows, win_t, cache_t):
    bs = q8.shape[0]
    wb = win_t.shape[2]
    blk = lambda w: pl.BlockSpec((None, NSA_HEADS, w), lambda i, pt, sel: (i, 0, 0))
    one = lambda w: pl.BlockSpec((None, 1, w), lambda i, pt, sel: (i, 0, 0))
    grid_spec = pltpu.PrefetchScalarGridSpec(
        num_scalar_prefetch=2,
        grid=(bs,),
        in_specs=[blk(LANES), one(LANES), blk(LANES), one(new_rows.shape[-1]),
                  pl.BlockSpec((None, 2 * LANES, wb), lambda i, pt, sel: (i, 0, 0)),
                  pl.BlockSpec(memory_space=pl.ANY)],
        out_specs=blk(LANES),
        scratch_shapes=[pltpu.VMEM((2, NSA_KV_HEADS, N_SEL_CACHE, 2 * LANES, PAGE_SIZE), F32),
                        pltpu.SemaphoreType.DMA((2,))],
    )
    return pl.pallas_call(
        _nsa_sample_attend_kernel,
        grid_spec=grid_spec,
        out_shape=jax.ShapeDtypeStruct((bs, NSA_HEADS, LANES), F32),
        compiler_params=_cparams("arbitrary"),
        name="nsa_sample_attend",
    )(page_table, sel_idx, q8, gate, o_cmp, new_rows, win_t, cache_t)


DIFF_PAGES_PER_STEP = 8
DIFF_ROW_STRIDE = 2 * DIFF_HEADS


def _lanes_to_heads(x, half):
    return jnp.concatenate([jnp.broadcast_to(x[:, DIFF_HEADS * half + h:DIFF_HEADS * half + h + 1], (1, LANES))
                            for h in range(DIFF_HEADS)], axis=1)


def _diff_sample_kernel(pt_ref, qt_ref, new_ref, lam_ref, ng_ref, *rest, lam_init):
    pages = rest[:DIFF_PAGES_PER_STEP]
    o_ref, m_ref, l_ref, acc_ref = rest[DIFF_PAGES_PER_STEP:]
    c = pl.program_id(1)
    hw = DIFF_HEADS * LANES
    qts = [_pad_rows(qt_ref[:, h * LANES:(h + 1) * LANES], LANES) for h in range(DIFF_HEADS)]
    r_io = lax.broadcasted_iota(jnp.int32, (LANES, hw), 0)
    c_io = lax.broadcasted_iota(jnp.int32, (LANES, hw), 1)
    expand = [(c_io // LANES == r_io - DIFF_HEADS * half).astype(BF16) for half in range(2)]

    @pl.when(c == 0)
    def _():
        s_new = jnp.zeros((SUBLANES, LANES), F32)
        for h in range(DIFF_HEADS):
            s_new = s_new + _dot_nt(_pad_rows(new_ref[:, h * LANES:(h + 1) * LANES], SUBLANES).astype(BF16), qts[h])
        m_ref[...] = s_new[0:1]
        l_ref[...] = jnp.ones((1, LANES), F32)
        v_new = _pad_rows(new_ref[:, hw:2 * hw].astype(BF16).astype(F32), SUBLANES)
        acc_ref[0] = v_new
        acc_ref[1] = v_new

    scores = []
    for page in pages:
        s = jnp.zeros((PAGE_SIZE, LANES), F32)
        for h in range(DIFF_HEADS):
            s = s + _dot_nt(page[pl.ds(h, PAGE_SIZE, stride=DIFF_ROW_STRIDE), :].astype(BF16), qts[h])
        scores.append(s)
    m_old = m_ref[...]
    m_new = m_old
    for s in scores:
        m_new = jnp.maximum(m_new, jnp.max(s, axis=0, keepdims=True))
    alpha = jnp.exp(m_old - m_new)
    l = l_ref[...] * alpha
    acc = [acc_ref[half] * _lanes_to_heads(alpha, half) for half in range(2)]
    for page, s in zip(pages, scores):
        e = jnp.exp(s - m_new)
        l = l + jnp.sum(e, axis=0, keepdims=True)
        eb = e.astype(BF16)
        v = jnp.concatenate([page[pl.ds(DIFF_HEADS + h, PAGE_SIZE, stride=DIFF_ROW_STRIDE), :]
                             for h in range(DIFF_HEADS)], axis=1)
        for half in range(2):
            w = _dot(eb, expand[half]) * v
            acc[half] = acc[half] + jnp.sum(w.reshape(PAGE_SIZE // SUBLANES, SUBLANES, hw), axis=0)
    m_ref[...] = m_new
    l_ref[...] = l
    acc_ref[0] = acc[0]
    acc_ref[1] = acc[1]

    @pl.when(c == pl.num_programs(1) - 1)
    def _():
        inv = 1.0 / l
        lam = _diff_lambda(lam_ref, lam_init)
        o = (jnp.sum(acc[0], axis=0, keepdims=True) * _lanes_to_heads(inv, 0)
             - lam * jnp.sum(acc[1], axis=0, keepdims=True) * _lanes_to_heads(inv, 1))
        for h in range(DIFF_HEADS):
            oh = o[:, h * LANES:(h + 1) * LANES]
            o_ref[:, h * LANES:(h + 1) * LANES] = _rms(oh, ng_ref[...], DIFF_NORM_EPS) * (1.0 - lam_init)


def diff_attend_sample(page_table, qt, new_rows, lam_vec, norm_g, cache_v, lam_init):
    bs, n_pages = page_table.shape
    p = DIFF_PAGES_PER_STEP
    hw = DIFF_HEADS * LANES
    page_specs = [pl.BlockSpec((None, PAGE_SIZE * DIFF_ROW_STRIDE, LANES),
                               functools.partial(lambda i, c, pt, k: (pt[i, c * p + k], 0, 0), k=k))
                  for k in range(p)]
    grid_spec = pltpu.PrefetchScalarGridSpec(
        num_scalar_prefetch=1,
        grid=(bs, n_pages // p),
        in_specs=[pl.BlockSpec((None, 2 * DIFF_HEADS, hw), lambda i, c, pt: (i, 0, 0)),
                  pl.BlockSpec((None, 1, 2 * hw), lambda i, c, pt: (i, 0, 0)),
                  pl.BlockSpec(lam_vec.shape, lambda i, c, pt: (0, 0)),
                  pl.BlockSpec((1, LANES), lambda i, c, pt: (0, 0))] + page_specs,
        out_specs=pl.BlockSpec((None, 1, hw), lambda i, c, pt: (i, 0, 0)),
        scratch_shapes=[pltpu.VMEM((1, LANES), F32), pltpu.VMEM((1, LANES), F32),
                        pltpu.VMEM((2, SUBLANES, hw), F32)],
    )
    return pl.pallas_call(
        functools.partial(_diff_sample_kernel, lam_init=lam_init),
        grid_spec=grid_spec,
        out_shape=jax.ShapeDtypeStruct((bs, 1, hw), F32),
        compiler_params=_cparams("parallel", "arbitrary"),
        name="diff_attend_sample",
    )(page_table, qt, new_rows, lam_vec, norm_g.reshape(1, LANES), *([cache_v] * p))


def _col_rep(row):
    x = jnp.broadcast_to(row, (LANES, row.shape[1]))
    return jnp.concatenate([x[:, c * LANES:(c + 1) * LANES].T for c in range(row.shape[1] // LANES)], axis=0)


def _head_sum(x):
    return jnp.sum(x.reshape(DIL_HEADS, HEAD_DIM, x.shape[1]), axis=1)


def _head_expand(x):
    return jnp.broadcast_to(x[:, None, :], (DIL_HEADS, HEAD_DIM, x.shape[1])).reshape(
        DIL_HEADS * HEAD_DIM, x.shape[1])


def _dil_sample_kernel(q_ref, new0_ref, new1_ref, new2_ref, st0_ref, st1_ref, st2_ref, o_ref):
    hw = DIL_HEADS * HEAD_DIM
    outs, lses = [], []
    for g, (new_ref, st_ref) in enumerate(((new0_ref, st0_ref), (new1_ref, st1_ref), (new2_ref, st2_ref))):
        win, dil = DIL_GROUPS[g]
        qc = _col_rep(q_ref[:, g * hw:(g + 1) * hw])
        kn = _col_rep(new_ref[:, 0:hw])
        vn = _col_rep(new_ref[:, hw:2 * hw])
        s_new = _head_sum(qc * kn)[:, 0:1]
        n_chunks = win // LANES
        s = jnp.concatenate([_head_sum(st_ref[0:hw, c * LANES:(c + 1) * LANES] * qc) for c in range(n_chunks)],
                            axis=1)
        lane = lax.broadcasted_iota(jnp.int32, s.shape, 1)
        s = jnp.where(lane % dil == 0, s, -jnp.inf)
        m = jnp.maximum(jnp.max(s, axis=-1, keepdims=True), s_new)
        e = jnp.exp(s - m)
        e_new = jnp.exp(s_new - m)
        den = jnp.sum(e, axis=-1, keepdims=True) + e_new
        p = e / den
        acc = _head_expand(jnp.broadcast_to(e_new / den, (DIL_HEADS, LANES))) * vn * (1.0 / LANES)
        for c in range(n_chunks):
            cs = slice(c * LANES, (c + 1) * LANES)
            acc = acc + st_ref[hw:2 * hw, cs] * _head_expand(p[:, cs])
        outs.append(jnp.sum(acc, axis=-1, keepdims=True))
        lses.append(jnp.log(den) + m)
    mx = jnp.maximum(jnp.maximum(lses[0], lses[1]), lses[2])
    es = [jnp.exp(l - mx) for l in lses]
    tot = es[0] + es[1] + es[2]
    mix = jnp.zeros((hw, LANES), F32)
    for g in range(len(DIL_GROUPS)):
        alpha = _head_expand(jnp.broadcast_to(es[g] / tot, (DIL_HEADS, LANES)))
        mix = mix + alpha * jnp.broadcast_to(outs[g], (hw, LANES))
    rows = jnp.concatenate([mix[c * LANES:(c + 1) * LANES, :].T for c in range(hw // LANES)], axis=1)
    o_ref[...] = rows[0:1]


def dil_attend_sample(q, news, states_t):
    bs = q.shape[0]
    hw = DIL_HEADS * HEAD_DIM
    st_specs = [pl.BlockSpec((None, 2 * hw, st.shape[2]), lambda i: (i, 0, 0)) for st in states_t]
    return pl.pallas_call(
        _dil_sample_kernel,
        grid=(bs,),
        in_specs=[pl.BlockSpec((None, 1, q.shape[2]), lambda i: (i, 0, 0))]
                 + [pl.BlockSpec((None, 1, 2 * hw), lambda i: (i, 0, 0))] * 3 + st_specs,
        out_specs=pl.BlockSpec((None, 1, hw), lambda i: (i, 0, 0)),
        out_shape=jax.ShapeDtypeStruct((bs, 1, hw), F32),
        compiler_params=_cparams("parallel"),
        name="dil_attend_sample",
    )(q, *news, *states_t)


AB_SIZES = (NSA_HEADS * HEAD_DIM, 6 * NSA_KV_HEADS * HEAD_DIM, 3 * NSA_HEADS,
            DIFF_HEADS * 2 * HEAD_DIM, DIFF_HEADS * 2 * HEAD_DIM, DIFF_HEADS * 2 * HEAD_DIM)
_QSCALE = HEAD_DIM ** -0.5


def _rope_tables(pos):
    half = HEAD_DIM // 2
    inv = ROPE_THETA ** (-jnp.arange(half, dtype=F32) / half)
    ang = pos.astype(F32)[:, None] * inv[None, :]
    c, s = jnp.cos(ang), jnp.sin(ang)
    return jnp.tile(c, (1, 4)), jnp.tile(jnp.concatenate([-s, s], axis=1), (1, 2))


def _pad_heads(w, offsets):
    z = jnp.zeros_like(w)
    lo = jnp.concatenate([w, z], axis=-1)
    hi = jnp.concatenate([z, w], axis=-1)
    at_lo = (np.asarray(offsets) == 0)[None, :, None]
    return jnp.where(at_lo, lo, hi).reshape(w.shape[0], -1)


def _prep_w_ab(w):
    d = w.shape[0]
    qa, kvb, gl, qd, kd, vd = jnp.split(w, np.cumsum(AB_SIZES)[:-1].tolist(), axis=1)
    qa = _pad_heads(qa.reshape(d, NSA_HEADS, HEAD_DIM), [(h // NSA_GROUP) * HEAD_DIM for h in range(NSA_HEADS)])
    kvb = kvb.reshape(d, 6, LANES)
    k3 = kvb[:, 0::2].reshape(d, 3 * LANES)
    v3 = kvb[:, 1::2].reshape(d, 3 * LANES)
    qd = _pad_heads(qd.reshape(d, 2 * DIFF_HEADS, HEAD_DIM), [(i % 2) * HEAD_DIM for i in range(2 * DIFF_HEADS)])
    gl = jnp.pad(gl, ((0, 0), (0, LANES - gl.shape[1])))
    return jnp.concatenate([qa, k3, v3, qd, kd, vd, gl], axis=1).astype(BF16)


def _ab_plan(dest):
    return (
        (0, 8, True, _QSCALE, tuple(dest("qa", j) for j in range(8))),
        (1024, 3, True, 1.0, (dest("k_cmp", 0), dest("k_slc", 0), dest("k_win", 0))),
        (1408, 3, False, 1.0, (dest("v_cmp", 0), dest("v_slc", 0), dest("v_win", 0))),
        (1792, 8, True, _QSCALE, tuple(dest("qd", j) for j in range(8))),
        (2816, 4, True, 1.0, tuple(dest("kd", j) for j in range(4))),
        (3328, 4, False, 1.0, tuple(dest("vd", j) for j in range(4))),
        (3840, 1, False, 1.0, (dest("gate", 0),)),
    )


def _ab_prompt_defs(t):
    defs = (("N", 1024, BF16), ("T", 512, F32, t), ("T", 256, BF16, t), ("N", 256, F32),
            ("T", 256, F32, min(NSA_WINDOW, t)), ("T", 256, BF16, t), ("N", 1024, BF16), ("N", 1024, F32),
            ("N", 1024, BF16), ("N", 128, F32))
    table = {
        "qa": lambda j: ((0, j),), "qd": lambda j: ((6, j),), "gate": lambda j: ((9, 0),),
        "k_cmp": lambda j: ((1, 0), (3, 0)), "v_cmp": lambda j: ((1, 1), (3, 1)),
        "k_slc": lambda j: ((1, 2), (2, 0)), "v_slc": lambda j: ((1, 3), (2, 1)),
        "k_win": lambda j: ((4, 0), (5, 0)), "v_win": lambda j: ((4, 1), (5, 1)),
        "kd": lambda j: ((7, j), (8, j)), "vd": lambda j: ((7, 4 + j), (8, 4 + j)),
    }
    return defs, _ab_plan(lambda name, j: table[name](j))


def _ab_sample_defs():
    defs = (("N", 1024, BF16), ("N", 512, F32), ("N", 256, F32), ("N", 1024, BF16), ("N", 1024, F32),
            ("N", 128, F32))
    table = {
        "qa": lambda j: ((0, j),), "qd": lambda j: ((3, j),), "gate": lambda j: ((5, 0),),
        "k_cmp": lambda j: ((1, 0),), "v_cmp": lambda j: ((1, 1),),
        "k_slc": lambda j: ((1, 2),), "v_slc": lambda j: ((1, 3),),
        "k_win": lambda j: ((2, 0),), "v_win": lambda j: ((2, 1),),
        "kd": lambda j: ((4, j),), "vd": lambda j: ((4, 4 + j),),
    }
    return defs, _ab_plan(lambda name, j: table[name](j))


def _prep_w_c(w):
    d = w.shape[0]
    ng = len(DIL_GROUPS)
    w = w.reshape(d, ng, 3, DIL_HEADS, HEAD_DIM)
    cols = []
    for g in range(ng):
        cols.append(_pad_heads(w[:, g, 0], [(h % 2) * HEAD_DIM for h in range(DIL_HEADS)]))
        cols.append(w[:, g, 1].reshape(d, -1))
        cols.append(w[:, g, 2].reshape(d, -1))
    return jnp.concatenate(cols, axis=1).astype(BF16)


def _c_prompt_defs(t):
    defs, plan = [], []
    for g, (win, dil) in enumerate(DIL_GROUPS):
        defs += [("R", 1024, BF16, dil), ("R", 1024, BF16, dil), ("T", 1024, F32, min(win, t))]
        plan += [
            (g * 2048, 8, True, _QSCALE, tuple(((3 * g, j),) for j in range(8))),
            (g * 2048 + 1024, 4, True, 1.0, tuple(((3 * g + 1, j), (3 * g + 2, j)) for j in range(4))),
            (g * 2048 + 1536, 4, False, 1.0, tuple(((3 * g + 1, 4 + j), (3 * g + 2, 4 + j)) for j in range(4))),
        ]
    return tuple(defs), tuple(plan)


def _c_sample_defs():
    defs = (("N", 1536, F32), ("N", 1024, F32), ("N", 1024, F32), ("N", 1024, F32))
    plan = []
    for g in range(len(DIL_GROUPS)):
        plan += [
            (g * 1536, 4, True, _QSCALE, tuple(((0, 4 * g + j),) for j in range(4))),
            (g * 1536 + 512, 4, True, 1.0, tuple(((1 + g, j),) for j in range(4))),
            (g * 1536 + 1024, 4, False, 1.0, tuple(((1 + g, 4 + j),) for j in range(4))),
        ]
    return defs, tuple(plan)


def _prep_cmp(w_cmp, pe_cmp):
    wk, wv = w_cmp[0], w_cmp[1]
    z = jnp.zeros_like(wk)
    w4 = jnp.concatenate([jnp.concatenate([wk, z, z, z], axis=-1), jnp.concatenate([z, wk, z, z], axis=-1),
                          jnp.concatenate([z, z, wv, z], axis=-1), jnp.concatenate([z, z, z, wv], axis=-1)],
                         axis=1).astype(BF16)
    pe4 = jnp.concatenate([pe_cmp[0], pe_cmp[0], pe_cmp[1], pe_cmp[1]], axis=-1)
    half = CMP_LEN // 2
    return pe4, w4[:half], w4[half:]


def _block_indicator_t(n_keys):
    return (jnp.arange(n_keys)[None, :] // SLC_BLOCK == jnp.arange(LANES)[:, None]).astype(BF16)


def _cmp_to_block(n_cmp):
    r = SLC_BLOCK // CMP_STRIDE
    return (jnp.arange(n_cmp)[None, :] // r == jnp.arange(LANES)[:, None]).astype(BF16)


def _rows_from_t(x_t, lead):
    b, _, r = x_t.shape
    nd = len(lead)
    return x_t.reshape((b,) + tuple(lead) + (HEAD_DIM, r)).transpose((0, nd + 2) + tuple(range(1, nd + 2)))


def _rows_to_t(x):
    b, r = x.shape[:2]
    nd = x.ndim
    return x.transpose((0,) + tuple(range(2, nd)) + (1,)).reshape(b, -1, r)


def ab_mix_prompt(h, b, t, g_in, w_ab, cmp_prep, lam_vec, dn_g, lam_init, cos, sin):
    defs, plan = _ab_prompt_defs(t)
    q_nsa, rows_nsa_t, slc_t, cmp_rows, rows_win_t, win_t, q_diff, rows_diff, kv_diff16, gate = project(
        h, b, g_in, cos, sin, w_ab, plan, defs)
    r3 = lambda x: x.reshape(b, t, x.shape[-1])
    pe4, wa, wb = cmp_prep
    cmp_kv = nsa_compress_prompt(r3(cmp_rows), pe4, wa, wb)
    o_nsa = nsa_attend_prompt(r3(q_nsa), r3(gate), cmp_kv, slc_t, win_t,
                              _block_indicator_t(t), _cmp_to_block(t // CMP_STRIDE))
    o_diff = diff_attend_prompt(r3(q_diff), r3(kv_diff16), lam_vec, dn_g, lam_init)
    mixed = [o_nsa.reshape(b * t, -1), o_diff.reshape(b * t, -1)]
    return mixed, rows_nsa_t, rows_win_t, r3(rows_diff)


def dil_mix_prompt(h, b, t, g_in, w_c, cos, sin):
    defs, plan = _c_prompt_defs(t)
    outs = project(h, b, g_in, cos, sin, w_c, plan, defs)
    os_, lses, rows_t = [], [], []
    for gi in range(len(DIL_GROUPS)):
        o, lse = dil_attend_prompt(outs[3 * gi], outs[3 * gi + 1])
        os_.append(o)
        lses.append(lse)
        rows_t.append(outs[3 * gi + 2])
    return os_, lses, rows_t


def _diff_qt(q_diff):
    bs = q_diff.shape[0]
    qd = q_diff.reshape(bs, DIFF_HEADS, 2, LANES).transpose(0, 2, 1, 3)
    eye = jnp.eye(DIFF_HEADS, dtype=q_diff.dtype)
    return (qd[:, :, :, None, :] * eye[None, None, :, :, None]).reshape(bs, 2 * DIFF_HEADS, DIFF_HEADS * LANES)


def ab_mix_sample(hs, g_in, w_ab, cmp_prep, lam_vec, dn_g, lam_init, cos, sin,
                  cache_nsa, cache_diff, win_state, page_table):
    bs = hs.shape[0]
    defs, plan = _ab_sample_defs()
    q_nsa, rows_nsa, rows_win, q_diff, rows_diff, gate = project(hs, 1, g_in, cos, sin, w_ab, plan, defs)
    n_pages = page_table.shape[1]
    past = n_pages * PAGE_SIZE
    assert past // SLC_BLOCK == LANES, "selection-block axis is laid out on the 128 lanes"
    n_pool = cache_nsa.shape[0]
    cache_t = _rows_to_t(cache_nsa)
    cache_v = cache_diff.reshape(n_pool, PAGE_SIZE * DIFF_ROW_STRIDE, LANES)
    pe4, wa, wb = cmp_prep
    cmp_kv = nsa_compress_sample(cache_t, page_table, pe4, wa, wb)
    q8 = q_nsa.reshape(bs, NSA_HEADS, LANES)
    o_cmp, imp = nsa_sample_cmp(q8, cmp_kv, _cmp_to_block(past // CMP_STRIDE), past)
    sel_idx = topk_lanes(imp, N_SEL_CACHE)[:, :NSA_KV_HEADS, :N_SEL_CACHE].reshape(bs, -1)
    new_rows = jnp.concatenate([rows_nsa, rows_win], axis=-1).reshape(bs, 1, -1)
    o8 = nsa_sample_attend(page_table, sel_idx, q8, gate.reshape(bs, 1, LANES), o_cmp, new_rows,
                           _rows_to_t(win_state), cache_t)
    o8 = o8.reshape(bs, NSA_HEADS, 2, HEAD_DIM)
    o_nsa = jnp.concatenate([o8[:, :NSA_GROUP, 0], o8[:, NSA_GROUP:, 1]], axis=1).reshape(bs, -1)
    o_diff = diff_attend_sample(page_table, _diff_qt(q_diff), rows_diff.reshape(bs, 1, -1), lam_vec, dn_g,
                                cache_v, lam_init).reshape(bs, -1)
    mixed = jnp.concatenate([o_nsa, o_diff], axis=-1).astype(BF16)
    return mixed, rows_nsa, rows_win, rows_diff


def dil_mix_sample(hs, g_in, w_c_rows, cos, sin, states):
    bs = hs.shape[0]
    defs, plan = _c_sample_defs()
    outs = project(hs, 1, g_in, cos, sin, w_c_rows, plan, defs)
    news = [x.reshape(bs, 1, -1) for x in outs[1:]]
    sts = []
    for (win, dil), st in zip(DIL_GROUPS, states):
        assert st.shape[1] == win and win == DIL_BAND * dil, "state buffer must hold the full dilated window"
        sts.append(_rows_to_t(st))
    o = dil_attend_sample(outs[0].reshape(bs, 1, -1), news, sts).reshape(bs, -1)
    return o.astype(BF16), outs[1:]


def kernel(x_prompt, x_sample, cache_nsa, cache_diff, state_nsa_win, state_dil_0, state_dil_1, state_dil_2,
           page_table, norm_g, ffn_w_in, ffn_w_out, w_in_ab, w_out_ab, nsa_w_cmp, nsa_pe_cmp, diff_lambda,
           diff_norm_g, w_in_c, w_out_c):
    b, t, d = x_prompt.shape
    bs, ns, _ = x_sample.shape
    assert ns == 1, "sample group is one new token per sequence"
    depth = norm_g.shape[0]
    past = page_table.shape[1] * PAGE_SIZE
    hp = x_prompt.reshape(b * t, d)
    hs = x_sample.reshape(bs, d)
    cos_p, sin_p = _rope_tables(jnp.tile(jnp.arange(t, dtype=jnp.int32), b))
    cos_s, sin_s = _rope_tables(jnp.full((bs,), past, jnp.int32))
    w_ffn_in = ffn_w_in.astype(BF16)
    w_ffn_out = ffn_w_out.astype(BF16)
    state_dil = (state_dil_0, state_dil_1, state_dil_2)
    nsa_p, nsa_s, win_p, win_s, diff_p, diff_s = [], [], [], [], [], []
    dil_p = [[] for _ in DIL_GROUPS]
    dil_s = [[] for _ in DIL_GROUPS]
    for layer in range(depth):
        g = norm_g[layer]
        hp = ffn_half(hp, g[0], g[1], w_ffn_in[layer, 0], w_ffn_out[layer, 0])
        hs = ffn_half(hs, g[0], g[1], w_ffn_in[layer, 0], w_ffn_out[layer, 0])
        if layer % 2 == 0:
            e = layer // 2
            lam_init = 0.8 - 0.6 * math.exp(-0.3 * layer)
            w_ab = _prep_w_ab(w_in_ab[e])
            cmp_prep = _prep_cmp(nsa_w_cmp[e], nsa_pe_cmp[e])
            mp, rn_t, rw_t, rd = ab_mix_prompt(hp, b, t, g[2], w_ab, cmp_prep, diff_lambda[e], diff_norm_g[e],
                                               lam_init, cos_p, sin_p)
            nsa_p.append(_rows_from_t(rn_t, (4, NSA_KV_HEADS)))
            win_p.append(_rows_from_t(rw_t[:, :, -min(NSA_WINDOW, t):], (2, NSA_KV_HEADS)))
            diff_p.append(rd.reshape(b, t, 2, DIFF_HEADS, 2 * HEAD_DIM))
            ms, rn, rw, rd = ab_mix_sample(hs, g[2], w_ab, cmp_prep, diff_lambda[e], diff_norm_g[e], lam_init,
                                           cos_s, sin_s, cache_nsa[e], cache_diff[e], state_nsa_win[e], page_table)
            nsa_s.append(rn.reshape(bs, 1, 4, NSA_KV_HEADS, HEAD_DIM))
            win_full = jnp.concatenate([state_nsa_win[e], rw.reshape(bs, 1, 2, NSA_KV_HEADS, HEAD_DIM)], axis=1)
            win_s.append(win_full[:, -min(NSA_WINDOW, win_full.shape[1]):])
            diff_s.append(rd.reshape(bs, 1, 2, DIFF_HEADS, 2 * HEAD_DIM))
            w_o = w_out_ab[e].astype(BF16)
            hp = outproj(hp, mp, w_o, g[3])
            hs = outproj(hs, [ms], w_o, g[3])
        else:
            o = layer // 2
            w_o = w_out_c[o].astype(BF16)
            os_, lses, rows_t = dil_mix_prompt(hp, b, t, g[2], _prep_w_c(w_in_c[o]), cos_p, sin_p)
            hp = outproj_dil(hp, b, os_, lses, w_o, g[3])
            ms, news = dil_mix_sample(hs, g[2], w_in_c[o].astype(BF16), cos_s, sin_s, [st[o] for st in state_dil])
            hs = outproj(hs, [ms], w_o, g[3])
            for gi, (win, dil) in enumerate(DIL_GROUPS):
                dil_p[gi].append(_rows_from_t(rows_t[gi][:, :, -min(win, t):], (2, DIL_HEADS)))
                full = jnp.concatenate([state_dil[gi][o], news[gi].reshape(bs, 1, 2, DIL_HEADS, HEAD_DIM)], axis=1)
                dil_s[gi].append(full[:, -min(win, full.shape[1]):])
        hp = ffn_half(hp, g[4], g[5], w_ffn_in[layer, 1], w_ffn_out[layer, 1])
        hs = ffn_half(hs, g[4], g[5], w_ffn_in[layer, 1], w_ffn_out[layer, 1])
    return (hp.reshape(b, t, d), hs.reshape(bs, 1, d), jnp.stack(nsa_p), jnp.stack(nsa_s), jnp.stack(win_p),
            jnp.stack(win_s), jnp.stack(diff_p), jnp.stack(diff_s), jnp.stack(dil_p[0]), jnp.stack(dil_s[0]),
            jnp.stack(dil_p[1]), jnp.stack(dil_s[1]), jnp.stack(dil_p[2]), jnp.stack(dil_s[2]))
```

```python
import functools
import math

import jax
import jax.numpy as jnp
import numpy as np
from jax import lax
from jax.experimental import pallas as pl
from jax.experimental.pallas import tpu as pltpu

F32 = jnp.float32
BF16 = jnp.bfloat16

LANES = 128
SUBLANES = 8
HEAD_DIM = 64
ROPE_THETA = 10000.0
NORM_EPS = 1e-6
PAGE_SIZE = 128
NSA_HEADS = 8
NSA_KV_HEADS = 2
NSA_GROUP = NSA_HEADS // NSA_KV_HEADS
CMP_LEN = 32
CMP_STRIDE = 16
SLC_BLOCK = 64
N_SEL = 16
NSA_WINDOW = 512
FORCED_BLOCK_SCORE = 1.0e4
DIFF_HEADS = 4
DIFF_NORM_EPS = 1e-5
DIL_GROUPS = ((128, 1), (512, 4), (2048, 16))
DIL_HEADS = 8
DIL_BAND = 128
NEG_BIG = -1.0e30
LN2 = math.log(2.0)
LOG2E = 1.0 / LN2
VMEM_LIMIT_BYTES = 56 * 1024 * 1024
TOKEN_TILE = 512


def _cparams(*sem):
    return pltpu.CompilerParams(dimension_semantics=sem, vmem_limit_bytes=VMEM_LIMIT_BYTES)


def _const_spec(shape):
    nd = len(shape)
    return pl.BlockSpec(shape, lambda *_: (0,) * nd, pipeline_mode=pl.Buffered(1))


def _rms(x, g, eps):
    return x * lax.rsqrt(jnp.mean(x * x, axis=-1, keepdims=True) + eps) * g


def _dot(a, b):
    return jnp.dot(a, b, preferred_element_type=F32)


def _dot_nt(a, b):
    return lax.dot_general(a, b, (((1,), (1,)), ((), ())), preferred_element_type=F32)


def _split3(x):
    hi = x.astype(BF16)
    r1 = x - hi.astype(F32)
    mid = r1.astype(BF16)
    lo = (r1 - mid.astype(F32)).astype(BF16)
    return hi, mid, lo


def _pad_rows(x, rows):
    return jnp.concatenate([x, jnp.zeros((rows - x.shape[0], x.shape[1]), x.dtype)], axis=0)


def _token_tile(n):
    return TOKEN_TILE if n % TOKEN_TILE == 0 else n


FFN_CHUNK = 256


def _ffn_kernel(x_ref, gpre_ref, gpost_ref, win_ref, wout_ref, o_ref, *, d_ff):
    x = x_ref[...]
    xn = _rms(x, gpre_ref[...], NORM_EPS).astype(BF16)
    acc = jnp.zeros(x.shape, F32)
    for c in range(d_ff // FFN_CHUNK):
        lo = c * FFN_CHUNK
        gate = _dot(xn, win_ref[:, lo:lo + FFN_CHUNK])
        up = _dot(xn, win_ref[:, d_ff + lo:d_ff + lo + FFN_CHUNK])
        act = (gate * jax.nn.sigmoid(gate) * up).astype(BF16)
        acc = acc + _dot(act, wout_ref[lo:lo + FFN_CHUNK, :])
    o_ref[...] = x + 0.5 * _rms(acc, gpost_ref[...], NORM_EPS)


def ffn_half(h, g_pre, g_post, w_in, w_out):
    n, d = h.shape
    d_ff = w_out.shape[0]
    tm = _token_tile(n)
    return pl.pallas_call(
        functools.partial(_ffn_kernel, d_ff=d_ff),
        grid=(n // tm,),
        in_specs=[pl.BlockSpec((tm, d), lambda i: (i, 0)),
                  _const_spec((1, d)), _const_spec((1, d)),
                  _const_spec(w_in.shape), _const_spec(w_out.shape)],
        out_specs=pl.BlockSpec((tm, d), lambda i: (i, 0)),
        out_shape=jax.ShapeDtypeStruct((n, d), F32),
        compiler_params=_cparams("parallel"),
        name="ffn_half",
    )(h, g_pre.reshape(1, d), g_post.reshape(1, d), w_in, w_out)


def _rope_slab(y, cos, sin):
    lane = lax.broadcasted_iota(jnp.int32, y.shape, 1)
    swapped = jnp.where(lane % HEAD_DIM < HEAD_DIM // 2,
                        pltpu.roll(y, LANES - HEAD_DIM // 2, 1),
                        pltpu.roll(y, HEAD_DIM // 2, 1))
    return y * cos + swapped * sin


def _proj_kernel(x_ref, g_ref, cos_ref, sin_ref, w_ref, *refs, plan, out_defs, first_tiles, tiles_per_b):
    out_refs = refs[:len(out_defs)]
    scr_ref = refs[len(out_defs)]
    tm = x_ref.shape[0]
    tile_in_b = pl.program_id(0) % tiles_per_b
    xn = _rms(x_ref[...], g_ref[...], NORM_EPS).astype(BF16)
    cos = cos_ref[...]
    sin = sin_ref[...]
    for col0, nslab, rope, scale, dests in plan:
        y = _dot(xn, w_ref[:, col0:col0 + nslab * LANES])
        for j in range(nslab):
            ys = y[:, j * LANES:(j + 1) * LANES]
            if rope:
                ys = _rope_slab(ys, cos, sin)
            if scale != 1.0:
                ys = ys * scale
            for out_idx, slab in dests[j]:
                ref = out_refs[out_idx]
                kind = out_defs[out_idx][0]
                cs = slice(slab * LANES, (slab + 1) * LANES)
                if kind == "N":
                    ref[:, cs] = ys.astype(ref.dtype)
                elif kind == "T":
                    def write_t(ref=ref, cs=cs, ys=ys):
                        ref[cs, :] = ys.T.astype(ref.dtype)
                    if first_tiles[out_idx] == 0:
                        write_t()
                    else:
                        pl.when(tile_in_b >= first_tiles[out_idx])(write_t)
                else:
                    dil = out_defs[out_idx][3]
                    if dil == 1:
                        ref[0, :, cs] = ys.astype(ref.dtype)
                    else:
                        scr_ref[...] = ys
                        for r in range(dil):
                            ref[r, :, cs] = scr_ref[pl.ds(r, tm // dil, stride=dil), :].astype(ref.dtype)


def project(h, b, g, cos, sin, w, plan, out_defs):
    n, d = h.shape
    t = n // b
    tm = _token_tile(t)
    tpb = t // tm
    specs, shapes, first_tiles = [], [], []
    for od in out_defs:
        kind, c, dt = od[:3]
        if kind == "N":
            specs.append(pl.BlockSpec((tm, c), lambda i: (i, 0)))
            shapes.append(jax.ShapeDtypeStruct((n, c), dt))
            first_tiles.append(0)
        elif kind == "T":
            keep = max(min(od[3], t), tm)
            ft = (t - keep) // tm
            specs.append(pl.BlockSpec((None, c, tm),
                                      functools.partial(lambda i, ft: (i // tpb, 0, jnp.maximum(i % tpb - ft, 0)),
                                                        ft=ft)))
            shapes.append(jax.ShapeDtypeStruct((b, c, keep), dt))
            first_tiles.append(ft)
        else:
            dil = od[3]
            specs.append(pl.BlockSpec((None, dil, tm // dil, c), lambda i: (i // tpb, 0, i % tpb, 0)))
            shapes.append(jax.ShapeDtypeStruct((b, dil, t // dil, c), dt))
            first_tiles.append(0)
    return pl.pallas_call(
        functools.partial(_proj_kernel, plan=plan, out_defs=out_defs, first_tiles=tuple(first_tiles),
                          tiles_per_b=tpb),
        grid=(n // tm,),
        in_specs=[pl.BlockSpec((tm, d), lambda i: (i, 0)), _const_spec((1, d)),
                  pl.BlockSpec((tm, LANES), lambda i: (i, 0)),
                  pl.BlockSpec((tm, LANES), lambda i: (i, 0)),
                  _const_spec(w.shape)],
        out_specs=specs,
        out_shape=shapes,
        scratch_shapes=[pltpu.VMEM((tm, LANES), F32)],
        compiler_params=_cparams("arbitrary"),
        name="project",
    )(h, g.reshape(1, d), cos, sin, w)


def _outproj_kernel(h_ref, *refs):
    w_ref, g_ref, o_ref = refs[-3:]
    y, row0 = None, 0
    for m_ref in refs[:-3]:
        c = m_ref.shape[1]
        part = _dot(m_ref[...], w_ref[row0:row0 + c, :])
        y = part if y is None else y + part
        row0 += c
    o_ref[...] = h_ref[...] + _rms(y, g_ref[...], NORM_EPS)


def outproj(h, ms, w, g):
    n, d = h.shape
    tm = _token_tile(n)
    return pl.pallas_call(
        _outproj_kernel,
        grid=(n // tm,),
        in_specs=[pl.BlockSpec((tm, d), lambda i: (i, 0))]
                 + [pl.BlockSpec((tm, m.shape[1]), lambda i: (i, 0)) for m in ms]
                 + [_const_spec(w.shape), _const_spec((1, d))],
        out_specs=pl.BlockSpec((tm, d), lambda i: (i, 0)),
        out_shape=jax.ShapeDtypeStruct((n, d), F32),
        compiler_params=_cparams("parallel"),
        name="outproj",
    )(h, *ms, w, g.reshape(1, d))


def _outproj_dil_kernel(h_ref, *refs):
    ng = len(DIL_GROUPS)
    w_ref, g_ref, o_ref, scr_ref = refs[2 * ng:]
    tm = h_ref.shape[0]
    nslab = DIL_HEADS * HEAD_DIM // LANES
    vals = []
    k = 0
    for gi, (_, dil) in enumerate(DIL_GROUPS):
        per_g = []
        for ref in (refs[2 * gi], refs[2 * gi + 1]):
            slabs = []
            for s in range(nslab):
                cs = slice(s * LANES, (s + 1) * LANES)
                if dil == 1:
                    slabs.append(ref[0, :, cs])
                else:
                    for r in range(dil):
                        scr_ref[k, pl.ds(r, tm // dil, stride=dil), :] = ref[r, :, cs]
                    slabs.append(scr_ref[k])
                    k += 1
            per_g.append(slabs)
        vals.append(per_g)
    mixed = []
    for s in range(nslab):
        l0, l1, l2 = vals[0][1][s], vals[1][1][s], vals[2][1][s]
        mx = jnp.maximum(jnp.maximum(l0, l1), l2)
        e0, e1, e2 = jnp.exp(l0 - mx), jnp.exp(l1 - mx), jnp.exp(l2 - mx)
        den = e0 + e1 + e2
        mixed.append(((e0 / den) * vals[0][0][s] + (e1 / den) * vals[1][0][s]
                      + (e2 / den) * vals[2][0][s]).astype(BF16))
    y = _dot(jnp.concatenate(mixed, axis=1), w_ref[...])
    o_ref[...] = h_ref[...] + _rms(y, g_ref[...], NORM_EPS)


def outproj_dil(h, b, outs, lses, w, g):
    n, d = h.shape
    t = n // b
    tm = _token_tile(t)
    tpb = t // tm
    c = DIL_HEADS * HEAD_DIM
    specs, args, n_scr = [], [], 0
    for (_, dil), o, l in zip(DIL_GROUPS, outs, lses):
        spec = pl.BlockSpec((None, dil, tm // dil, c), lambda i: (i // tpb, 0, i % tpb, 0))
        specs += [spec, spec]
        args += [o, l]
        if dil > 1:
            n_scr += 2 * (c // LANES)
    return pl.pallas_call(
        _outproj_dil_kernel,
        grid=(n // tm,),
        in_specs=[pl.BlockSpec((tm, d), lambda i: (i, 0))] + specs + [_const_spec(w.shape), _const_spec((1, d))],
        out_specs=pl.BlockSpec((tm, d), lambda i: (i, 0)),
        out_shape=jax.ShapeDtypeStruct((n, d), F32),
        scratch_shapes=[pltpu.VMEM((n_scr, tm, LANES), F32)],
        compiler_params=_cparams("parallel"),
        name="outproj_dil",
    )(h, *args, w, g.reshape(1, d))


def _compress_rows(k_ref, v_ref, pe_ref, wa_ref, wb_ref, nsub):
    half = CMP_LEN // 2
    acc_a = jnp.zeros((nsub, 2 * LANES), F32)
    acc_b = jnp.zeros((nsub, 2 * LANES), F32)
    for l in range(half):
        x = jnp.concatenate([k_ref[pl.ds(l, nsub, stride=CMP_STRIDE), :],
                             v_ref[pl.ds(l, nsub, stride=CMP_STRIDE), :]], axis=1)
        acc_a = acc_a + _dot((x + pe_ref[l:l + 1, :]).astype(BF16), wa_ref[l])
        acc_b = acc_b + _dot((x + pe_ref[half + l:half + l + 1, :]).astype(BF16), wb_ref[l])
    comp = acc_a + pltpu.roll(acc_b, nsub - 1, 0)
    row = lax.broadcasted_iota(jnp.int32, comp.shape, 0)
    return jnp.where(row < nsub - 1, comp, 0.0)


def _compress_kernel(k_ref, v_ref, pe_ref, wa_ref, wb_ref, k_out_ref, vt_out_ref, *, nsub):
    comp = _compress_rows(k_ref, v_ref, pe_ref, wa_ref, wb_ref, nsub)
    k_out_ref[...] = comp[:, 0:LANES].astype(k_out_ref.dtype)
    vt_out_ref[...] = comp[:, LANES:2 * LANES].T.astype(vt_out_ref.dtype)


def nsa_compress_prompt(cmp_rows, pe4, wa, wb):
    b, t, _ = cmp_rows.shape
    nsub = t // CMP_STRIDE
    return pl.pallas_call(
        functools.partial(_compress_kernel, nsub=nsub),
        grid=(b,),
        in_specs=[pl.BlockSpec((None, t, LANES), lambda i: (i, 0, 0)),
                  pl.BlockSpec((None, t, LANES), lambda i: (i, 0, 1)),
                  _const_spec(pe4.shape), _const_spec(wa.shape), _const_spec(wb.shape)],
        out_specs=[pl.BlockSpec((None, nsub, LANES), lambda i: (i, 0, 0)),
                   pl.BlockSpec((None, LANES, nsub), lambda i: (i, 0, 0))],
        out_shape=[jax.ShapeDtypeStruct((b, nsub, LANES), BF16), jax.ShapeDtypeStruct((b, LANES, nsub), BF16)],
        compiler_params=_cparams("parallel"),
        name="nsa_compress_prompt",
    )(cmp_rows, cmp_rows, pe4, wa, wb)


NSA_TQ = 256
NSA_TK = 512


def _softmax_terms(s, mask, exp_fn):
    s = jnp.where(mask, s, -jnp.inf)
    m = jnp.max(s, axis=-1, keepdims=True)
    m = jnp.where(m == -jnp.inf, 0.0, m)
    e = exp_fn(s - m)
    den = jnp.sum(e, axis=-1, keepdims=True)
    return e, 1.0 / jnp.where(den > 0, den, 1.0)


def _softmax_rows(s, mask, exp_fn=jnp.exp):
    e, inv = _softmax_terms(s, mask, exp_fn)
    return e * inv


def _topk_mask_t(score_t, k):
    j_io = lax.broadcasted_iota(jnp.int32, score_t.shape, 0)
    nj = score_t.shape[0]
    work = score_t
    for _ in range(k):
        m = jnp.max(work, axis=0, keepdims=True)
        jmin = jnp.min(jnp.where(work == m, j_io, nj), axis=0, keepdims=True)
        work = jnp.where(j_io == jmin, -jnp.inf, work)
    return jnp.logical_and(work == -jnp.inf, score_t > -jnp.inf)


SWEEP_CHUNKS = 2


def _causal_sweep_t(score_fn, pv_fn, mask_fn, n_full, cols):
    nc = SWEEP_CHUNKS
    cc = cols // nc

    def update(t, masked, state):
        scs = [score_fn(t, c) for c in range(nc)]
        out = []
        for c, (m, l, acc) in enumerate(state):
            sc = scs[c]
            if masked:
                sc = jnp.where(mask_fn(t, c), sc, NEG_BIG)
            m_new = jnp.maximum(m, jnp.max(sc, axis=0, keepdims=True))
            alpha = jnp.exp2(m - m_new)
            pe = jnp.exp2(sc - m_new)
            out.append((m_new, alpha * l + jnp.sum(pe, axis=0, keepdims=True),
                        alpha * acc + pv_fn(t, pe.astype(BF16))))
        return out

    init = [(jnp.full((1, cc), NEG_BIG, F32), jnp.zeros((1, cc), F32), jnp.zeros((LANES, cc), F32))
            for _ in range(nc)]
    state = lax.fori_loop(0, n_full, lambda t, st: update(t, False, st), init)
    state = update(n_full, True, state)
    return (jnp.concatenate([s[1] for s in state], axis=1), jnp.concatenate([s[2] for s in state], axis=1))


def _softmax_terms_t(s, mask):
    s = jnp.where(mask, s, -jnp.inf)
    m = jnp.max(s, axis=0, keepdims=True)
    m = jnp.where(m == -jnp.inf, 0.0, m)
    e = jnp.exp2(s - m)
    den = jnp.sum(e, axis=0, keepdims=True)
    return e, 1.0 / jnp.where(den > 0, den, 1.0)


def _nsa_prompt_kernel(qt_ref, gt_ref, kcmp_ref, vcmp_ref, kslc_ref, vslc_ref, kwin_ref, vwin_ref, eall_ref,
                       selt_ref, o_ref):
    tq = NSA_TQ
    qs = pl.program_id(1) * tq
    rows = NSA_GROUP * tq
    qpos = qs + lax.broadcasted_iota(jnp.int32, (1, rows), 1) % tq
    ncmp = kcmp_ref.shape[0]
    gates_t = jax.nn.sigmoid(gt_ref[...])
    head_out = []
    for kvh in range(NSA_KV_HEADS):
        q4t = jnp.concatenate(
            [qt_ref[(NSA_GROUP * kvh + g) * LANES:(NSA_GROUP * kvh + g + 1) * LANES, :]
             for g in range(NSA_GROUP)], axis=1)
        s = _dot(kcmp_ref[...], q4t)
        cmp_end = lax.broadcasted_iota(jnp.int32, (ncmp, 1), 0) * CMP_STRIDE + (CMP_LEN - 1)
        e, inv = _softmax_terms_t(s, cmp_end <= qpos)
        p = e * inv
        o_cmp_t = _dot(vcmp_ref[...], p.astype(BF16))
        psum = p[:, 0:tq] + p[:, tq:2 * tq] + p[:, 2 * tq:3 * tq] + p[:, 3 * tq:4 * tq]
        hi, mid, lo = _split3(psum)
        selt = selt_ref[...]
        imp_t = _dot(selt, hi) + _dot(selt, mid) + _dot(selt, lo)
        j_io = lax.broadcasted_iota(jnp.int32, imp_t.shape, 0)
        cur = (qs + lax.broadcasted_iota(jnp.int32, imp_t.shape, 1)) // SLC_BLOCK
        forced = jnp.logical_or(j_io == 0, j_io == cur)
        score_t = jnp.where(j_io <= cur, jnp.where(forced, FORCED_BLOCK_SCORE, imp_t), -jnp.inf)
        bias_t = jnp.where(_topk_mask_t(score_t, N_SEL), 0.0, NEG_BIG).astype(BF16)
        qext_t = jnp.concatenate([q4t, jnp.concatenate([bias_t] * NSA_GROUP, axis=1)], axis=0)

        def tile(t):
            return pl.ds(t * NSA_TK if isinstance(t, int) else pl.multiple_of(t * NSA_TK, NSA_TK), NSA_TK)

        cc = rows // SWEEP_CHUNKS

        def sel_scores(t, c, qext_t=qext_t):
            return _dot(jnp.concatenate([kslc_ref[tile(t), :], eall_ref[tile(t), :]], axis=1),
                        qext_t[:, c * cc:(c + 1) * cc])

        def sel_values(t, p):
            return _dot(vslc_ref[:, tile(t)], p)

        def sel_mask(t, c):
            kpos = t * NSA_TK + lax.broadcasted_iota(jnp.int32, (NSA_TK, 1), 0)
            return kpos <= qs + (c * cc + lax.broadcasted_iota(jnp.int32, (1, cc), 1)) % tq

        l_sel, acc_sel = _causal_sweep_t(sel_scores, sel_values, sel_mask, qs // NSA_TK, rows)
        o_sel_t = acc_sel / l_sel
        wlen = NSA_WINDOW + tq
        ws = pl.multiple_of(jnp.maximum(qs - NSA_WINDOW, 0), tq)
        sw = _dot(kwin_ref[pl.ds(ws, wlen), :], q4t)
        dist = qpos - (ws + lax.broadcasted_iota(jnp.int32, (wlen, 1), 0))
        ew, inv_w = _softmax_terms_t(sw, jnp.logical_and(dist >= 0, dist < NSA_WINDOW))
        o_win_t = _dot(vwin_ref[:, pl.ds(ws, wlen)], ew.astype(BF16)) * inv_w
        for g in range(NSA_GROUP):
            h = NSA_GROUP * kvh + g
            cs = slice(g * tq, (g + 1) * tq)
            mixed = (gates_t[3 * h:3 * h + 1, :] * o_cmp_t[:, cs] + gates_t[3 * h + 1:3 * h + 2, :] * o_sel_t[:, cs]
                     + gates_t[3 * h + 2:3 * h + 3, :] * o_win_t[:, cs])
            head_out.append(mixed[kvh * HEAD_DIM:(kvh + 1) * HEAD_DIM, :])
    for pair in range(NSA_HEADS // 2):
        slab_t = jnp.concatenate([head_out[2 * pair], head_out[2 * pair + 1]], axis=0)
        o_ref[:, pair * LANES:(pair + 1) * LANES] = slab_t.T.astype(o_ref.dtype)


def nsa_attend_prompt(q_nsa_t, gate_t, kcmp16, vcmp_t, kslc16, vslc_t, kwin16, vwin_t, eall, selt):
    b, _, t = q_nsa_t.shape
    ncmp = kcmp16.shape[1]
    rows = lambda n: pl.BlockSpec((None, n, LANES), lambda i, j: (i, 0, 0))
    cols = lambda n: pl.BlockSpec((None, LANES, n), lambda i, j: (i, 0, 0))
    return pl.pallas_call(
        _nsa_prompt_kernel,
        grid=(b, t // NSA_TQ),
        in_specs=[pl.BlockSpec((None, NSA_HEADS * LANES, NSA_TQ), lambda i, j: (i, 0, j)),
                  pl.BlockSpec((None, LANES, NSA_TQ), lambda i, j: (i, 0, j)),
                  rows(ncmp), cols(ncmp), rows(t), cols(t), rows(t), cols(t),
                  _const_spec(eall.shape), _const_spec(selt.shape)],
        out_specs=pl.BlockSpec((None, NSA_TQ, NSA_HEADS * HEAD_DIM), lambda i, j: (i, j, 0)),
        out_shape=jax.ShapeDtypeStruct((b, t, NSA_HEADS * HEAD_DIM), BF16),
        compiler_params=_cparams("parallel", "parallel"),
        name="nsa_attend_prompt",
    )(q_nsa_t, gate_t, kcmp16, vcmp_t, kslc16, vslc_t, kwin16, vwin_t, eall, selt)


DIFF_TQ = 512
DIFF_TK = 512


def _diff_lambda(lam_ref, lam_init):
    lv = lam_ref[...]
    a = jnp.sum(lv[0:1] * lv[1:2], axis=-1, keepdims=True)
    b = jnp.sum(lv[2:3] * lv[3:4], axis=-1, keepdims=True)
    return jnp.exp(a) - jnp.exp(b) + lam_init


def _diff_prompt_kernel(qt_ref, k_ref, vt_ref, lam_ref, ng_ref, o_ref, *, lam_init):
    tq = DIFF_TQ
    qs = pl.program_id(2) * tq
    q2t = jnp.concatenate([qt_ref[0:LANES, :], qt_ref[LANES:2 * LANES, :]], axis=1)
    cc = 2 * tq // SWEEP_CHUNKS

    def tile(t):
        return pl.ds(t * DIFF_TK if isinstance(t, int) else pl.multiple_of(t * DIFF_TK, DIFF_TK), DIFF_TK)

    def scores(t, c):
        return _dot(k_ref[tile(t), :], q2t[:, c * cc:(c + 1) * cc])

    def values(t, p):
        return _dot(vt_ref[:, tile(t)], p)

    def mask(t, c):
        kpos = t * DIFF_TK + lax.broadcasted_iota(jnp.int32, (DIFF_TK, 1), 0)
        qpos = qs + (c * cc + lax.broadcasted_iota(jnp.int32, (1, cc), 1)) % tq
        return kpos <= qpos

    l, acc = _causal_sweep_t(scores, values, mask, qs // DIFF_TK, 2 * tq)
    ot = acc / l
    lam = _diff_lambda(lam_ref, lam_init)
    ot = ot[:, 0:tq] - lam * ot[:, tq:2 * tq]
    o = ot.T
    o_ref[...] = (_rms(o, ng_ref[...], DIFF_NORM_EPS) * (1.0 - lam_init)).astype(o_ref.dtype)


def diff_attend_prompt(q_diff_t, k_diff16, v_diff_t, lam_vec, norm_g, lam_init):
    b, _, t = q_diff_t.shape
    return pl.pallas_call(
        functools.partial(_diff_prompt_kernel, lam_init=lam_init),
        grid=(b, DIFF_HEADS, t // DIFF_TQ),
        in_specs=[pl.BlockSpec((None, 2 * LANES, DIFF_TQ), lambda i, h, j: (i, h, j)),
                  pl.BlockSpec((None, t, LANES), lambda i, h, j: (i, 0, h)),
                  pl.BlockSpec((None, LANES, t), lambda i, h, j: (i, h, 0)),
                  _const_spec(lam_vec.shape), _const_spec((1, LANES))],
        out_specs=pl.BlockSpec((None, DIFF_TQ, LANES), lambda i, h, j: (i, j, h)),
        out_shape=jax.ShapeDtypeStruct((b, t, DIFF_HEADS * LANES), BF16),
        compiler_params=_cparams("parallel", "parallel", "parallel"),
        name="diff_attend_prompt",
    )(q_diff_t, k_diff16, v_diff_t, lam_vec, norm_g.reshape(1, LANES))


def _dil_prompt_kernel(q_ref, kvp_ref, kvc_ref, o_ref, lse_ref):
    band = DIL_BAND
    hw = DIL_HEADS * HEAD_DIM
    first = pl.program_id(2) == 0
    qi = lax.broadcasted_iota(jnp.int32, (band, 2 * band), 0) + band
    kj = lax.broadcasted_iota(jnp.int32, (band, 2 * band), 1)
    rel = qi - kj
    ok = jnp.logical_and(rel >= 0, rel <= band)
    ok = jnp.logical_and(ok, jnp.logical_not(jnp.logical_and(first, kj < band)))
    lane = lax.broadcasted_iota(jnp.int32, (band, LANES), 1)
    for pair in range(DIL_HEADS // 2):
        cs = slice(pair * LANES, (pair + 1) * LANES)
        vs = slice(hw + pair * LANES, hw + (pair + 1) * LANES)
        k2 = jnp.concatenate([kvp_ref[:, cs], kvc_ref[:, cs]], axis=0)
        v2 = jnp.concatenate([kvp_ref[:, vs], kvc_ref[:, vs]], axis=0)
        outs, lses = [], []
        for hh in range(2):
            h = 2 * pair + hh
            s = _dot_nt(q_ref[:, h * LANES:(h + 1) * LANES], k2)
            s = jnp.where(ok, s, -jnp.inf)
            m = jnp.max(s, axis=-1, keepdims=True)
            e = jnp.exp2(s - m)
            den = jnp.sum(e, axis=-1, keepdims=True)
            outs.append(_dot((e / den).astype(BF16), v2))
            lses.append(jnp.log(den) + m * LN2)
        o_ref[:, cs] = jnp.where(lane < HEAD_DIM, outs[0], outs[1])
        lse_ref[:, cs] = jnp.where(lane < HEAD_DIM, lses[0], lses[1])


def dil_attend_prompt(q_r, kv_r):
    b, dil, n, _ = q_r.shape
    hw = DIL_HEADS * HEAD_DIM
    blk = lambda w, prev: pl.BlockSpec(
        (None, None, DIL_BAND, w), (lambda i, r, u: (i, r, jnp.maximum(u - 1, 0), 0)) if prev
        else (lambda i, r, u: (i, r, u, 0)))
    return pl.pallas_call(
        _dil_prompt_kernel,
        grid=(b, dil, n // DIL_BAND),
        in_specs=[blk(DIL_HEADS * LANES, False), blk(2 * hw, True), blk(2 * hw, False)],
        out_specs=[blk(hw, False), blk(hw, False)],
        out_shape=[jax.ShapeDtypeStruct((b, dil, n, hw), F32)] * 2,
        compiler_params=_cparams("parallel", "parallel", "parallel"),
        name="dil_attend_prompt",
    )(q_r, kv_r, kv_r)


def _page_copies(cache_ref, pt_ref, buf_ref, sem_ref, bi, slot, n_pages):
    return [pltpu.make_async_copy(cache_ref.at[pt_ref[bi, j], pl.ds(0, 2 * LANES), :],
                                  buf_ref.at[slot, j], sem_ref.at[slot]) for j in range(n_pages)]


def _compress_sample_kernel(pt_ref, cache_ref, pe_ref, wa_ref, wb_ref, o_ref, page_ref, kbuf_ref, vbuf_ref,
                            sem_ref, *, n_pages):
    i = pl.program_id(0)
    slot = i % 2
    nsub = n_pages * PAGE_SIZE // CMP_STRIDE
    copies = functools.partial(_page_copies, cache_ref, pt_ref, page_ref, sem_ref, n_pages=n_pages)

    @pl.when(i == 0)
    def _():
        for cp in copies(0, 0):
            cp.start()

    @pl.when(i + 1 < pl.num_programs(0))
    def _():
        for cp in copies(i + 1, 1 - slot):
            cp.start()

    for cp in copies(i, slot):
        cp.wait()

    def to_rows(j, carry):
        r0 = pl.multiple_of(j * PAGE_SIZE, PAGE_SIZE)
        kbuf_ref[pl.ds(r0, PAGE_SIZE), :] = page_ref[slot, j, 0:LANES, :].T
        vbuf_ref[pl.ds(r0, PAGE_SIZE), :] = page_ref[slot, j, LANES:2 * LANES, :].T
        return carry

    lax.fori_loop(0, n_pages, to_rows, 0)
    o_ref[...] = _compress_rows(kbuf_ref, vbuf_ref, pe_ref, wa_ref, wb_ref, nsub).astype(o_ref.dtype)


def nsa_compress_sample(cache_t, page_table, pe4, wa, wb):
    bs, n_pages = page_table.shape
    past = n_pages * PAGE_SIZE
    nsub = past // CMP_STRIDE
    grid_spec = pltpu.PrefetchScalarGridSpec(
        num_scalar_prefetch=1,
        grid=(bs,),
        in_specs=[pl.BlockSpec(memory_space=pl.ANY),
                  pl.BlockSpec(pe4.shape, lambda i, pt: (0, 0)),
                  pl.BlockSpec(wa.shape, lambda i, pt: (0, 0, 0)),
                  pl.BlockSpec(wb.shape, lambda i, pt: (0, 0, 0))],
        out_specs=pl.BlockSpec((None, nsub, 2 * LANES), lambda i, pt: (i, 0, 0)),
        scratch_shapes=[pltpu.VMEM((2, n_pages, 2 * LANES, PAGE_SIZE), F32),
                        pltpu.VMEM((past, LANES), F32), pltpu.VMEM((past, LANES), F32),
                        pltpu.SemaphoreType.DMA((2,))],
    )
    return pl.pallas_call(
        functools.partial(_compress_sample_kernel, n_pages=n_pages),
        grid_spec=grid_spec,
        out_shape=jax.ShapeDtypeStruct((bs, nsub, 2 * LANES), BF16),
        compiler_params=_cparams("arbitrary"),
        name="nsa_compress_sample",
    )(page_table, cache_t, pe4, wa, wb)


def _group_sum_rows(x):
    parts = [jnp.sum(x[NSA_GROUP * k:NSA_GROUP * (k + 1)], axis=0, keepdims=True) for k in range(NSA_KV_HEADS)]
    return _pad_rows(jnp.concatenate(parts, axis=0), x.shape[0])


def _nsa_sample_cmp_kernel(q_ref, cmp_ref, selt_ref, ocmp_ref, imp_ref, *, qpos):
    q8 = q_ref[...]
    ncmp = cmp_ref.shape[0]
    s = _dot_nt(q8, cmp_ref[:, 0:LANES])
    cmp_end = lax.broadcasted_iota(jnp.int32, (1, ncmp), 1) * CMP_STRIDE + (CMP_LEN - 1)
    p = _softmax_rows(s, cmp_end <= qpos)
    ocmp_ref[...] = _dot(p.astype(BF16), cmp_ref[:, LANES:2 * LANES])
    hi, mid, lo = _split3(_group_sum_rows(p))
    selt = selt_ref[...]
    imp_ref[...] = _dot_nt(hi, selt) + _dot_nt(mid, selt) + _dot_nt(lo, selt)


def nsa_sample_cmp(q8, cmp_kv, selt, qpos):
    bs = q8.shape[0]
    ncmp = cmp_kv.shape[1]
    blk = pl.BlockSpec((None, NSA_HEADS, LANES), lambda i: (i, 0, 0))
    return pl.pallas_call(
        functools.partial(_nsa_sample_cmp_kernel, qpos=qpos),
        grid=(bs,),
        in_specs=[blk, pl.BlockSpec((None, ncmp, 2 * LANES), lambda i: (i, 0, 0)), _const_spec(selt.shape)],
        out_specs=[blk, blk],
        out_shape=[jax.ShapeDtypeStruct((bs, NSA_HEADS, LANES), F32)] * 2,
        compiler_params=_cparams("parallel"),
        name="nsa_sample_cmp",
    )(q8, cmp_kv, selt)


def _topk_lanes_kernel(imp_ref, idx_ref, *, k):
    score = imp_ref[...]
    lane = lax.broadcasted_iota(jnp.int32, score.shape, 1)
    lane_f = lane.astype(F32)
    score = jnp.where(lane == 0, FORCED_BLOCK_SCORE, score)
    idx = jnp.zeros(score.shape, F32)
    for r in range(k):
        m = jnp.max(score, axis=-1, keepdims=True)
        jmin = jnp.min(jnp.where(score == m, lane_f, float(LANES)), axis=-1, keepdims=True)
        idx = jnp.where(lane == r, jmin, idx)
        score = jnp.where(lane_f == jmin, -jnp.inf, score)
    idx_ref[...] = idx.astype(jnp.int32)


def topk_lanes(imp, k):
    bs = imp.shape[0]
    x = imp.reshape(bs * NSA_HEADS, LANES)
    out = pl.pallas_call(
        functools.partial(_topk_lanes_kernel, k=k),
        out_shape=jax.ShapeDtypeStruct(x.shape, jnp.int32),
        name="topk_lanes",
    )(x)
    return out.reshape(bs, NSA_HEADS, LANES)


N_SEL_CACHE = N_SEL - 1


def _sel_copies(cache_ref, pt_ref, sel_ref, buf_ref, sem_ref, bi, slot):
    cps = []
    for kvh in range(NSA_KV_HEADS):
        for r in range(N_SEL_CACHE):
            j = sel_ref[bi, kvh * N_SEL_CACHE + r]
            cps.append(pltpu.make_async_copy(
                cache_ref.at[pt_ref[bi, j // 2], pl.ds(2 * LANES, 2 * LANES), :],
                buf_ref.at[slot, kvh, r], sem_ref.at[slot]))
    return cps


def _pick_gate(gates8, branch):
    row = lax.broadcasted_iota(jnp.int32, gates8.shape, 0)
    lane = lax.broadcasted_iota(jnp.int32, gates8.shape, 1)
    return jnp.sum(jnp.where(lane == 3 * row + branch, gates8, 0.0), axis=-1, keepdims=True)


def _nsa_sample_attend_kernel(pt_ref, sel_ref, q_ref, gate_ref, ocmp_ref, new_ref, win_ref, cache_ref,
                              o_ref, buf_ref, sem_ref):
    i = pl.program_id(0)
    slot = i % 2
    copies = functools.partial(_sel_copies, cache_ref, pt_ref, sel_ref, buf_ref, sem_ref)

    @pl.when(i == 0)
    def _():
        for cp in copies(0, 0):
            cp.start()

    @pl.when(i + 1 < pl.num_programs(0))
    def _():
        for cp in copies(i + 1, 1 - slot):
            cp.start()

    q8 = q_ref[...]
    q8f = q8.astype(F32)
    row = lax.broadcasted_iota(jnp.int32, (NSA_HEADS, 1), 0)
    new = new_ref[...]
    rnd = lambda x: x.astype(BF16).astype(F32)

    def probs_with_new_key(s, k_new, mask):
        s_new = jnp.sum(q8f * rnd(k_new), axis=-1, keepdims=True)
        s = jnp.where(mask, s, -jnp.inf)
        m = jnp.maximum(jnp.max(s, axis=-1, keepdims=True), s_new)
        e = jnp.exp(s - m)
        e_new = jnp.exp(s_new - m)
        den = jnp.sum(e, axis=-1, keepdims=True) + e_new
        return (e / den).astype(BF16), rnd(e_new / den)

    wb = win_ref.shape[1]
    widx = lax.broadcasted_iota(jnp.int32, (1, wb), 1)
    pw, pw_new = probs_with_new_key(_dot(q8, win_ref[0:LANES, :].astype(BF16)), new[:, 4 * LANES:5 * LANES],
                                    widx > wb - NSA_WINDOW)
    o_win = _dot_nt(pw, win_ref[LANES:2 * LANES, :].astype(BF16)) + pw_new * rnd(new[:, 5 * LANES:6 * LANES])
    for cp in copies(i, slot):
        cp.wait()
    lane = lax.broadcasted_iota(jnp.int32, (1, PAGE_SIZE), 1)
    o_sel = []
    for kvh in range(NSA_KV_HEADS):
        ss, masks = [], []
        for r in range(N_SEL_CACHE):
            ss.append(_dot(q8, buf_ref[slot, kvh, r, 0:LANES, :].astype(BF16)))
            masks.append(lane // SLC_BLOCK == sel_ref[i, kvh * N_SEL_CACHE + r] % 2)
        p, p_new = probs_with_new_key(jnp.concatenate(ss, axis=1), new[:, 2 * LANES:3 * LANES],
                                      jnp.concatenate(masks, axis=1))
        o = p_new * rnd(new[:, 3 * LANES:4 * LANES])
        for r in range(N_SEL_CACHE):
            o = o + _dot_nt(p[:, r * PAGE_SIZE:(r + 1) * PAGE_SIZE],
                            buf_ref[slot, kvh, r, LANES:2 * LANES, :].astype(BF16))
        o_sel.append(o)
    o_sel = jnp.where(row < NSA_GROUP, o_sel[0], o_sel[1])
    gates8 = jnp.broadcast_to(jax.nn.sigmoid(gate_ref[...]), (NSA_HEADS, LANES))
    o_ref[...] = (_pick_gate(gates8, 0) * ocmp_ref[...] + _pick_gate(gates8, 1) * o_sel
                  + _pick_gate(gates8, 2) * o_win)


def nsa_sample_attend(page_table, sel_idx, q8, gate, o_cmp, new_rows, win_t, cache_t):
    bs = q8.shape[0]
    wb = win_t.shape[2]
    blk = lambda w: pl.BlockSpec((None, NSA_HEADS, w), lambda i, pt, sel: (i, 0, 0))
    one = lambda w: pl.BlockSpec((None, 1, w), lambda i, pt, sel: (i, 0, 0))
    grid_spec = pltpu.PrefetchScalarGridSpec(
        num_scalar_prefetch=2,
        grid=(bs,),
        in_specs=[blk(LANES), one(LANES), blk(LANES), one(new_rows.shape[-1]),
                  pl.BlockSpec((None, 2 * LANES, wb), lambda i, pt, sel: (i, 0, 0)),
                  pl.BlockSpec(memory_space=pl.ANY)],
        out_specs=blk(LANES),
        scratch_shapes=[pltpu.VMEM((2, NSA_KV_HEADS, N_SEL_CACHE, 2 * LANES, PAGE_SIZE), F32),
                        pltpu.SemaphoreType.DMA((2,))],
    )
    return pl.pallas_call(
        _nsa_sample_attend_kernel,
        grid_spec=grid_spec,
        out_shape=jax.ShapeDtypeStruct((bs, NSA_HEADS, LANES), F32),
        compiler_params=_cparams("arbitrary"),
        name="nsa_sample_attend",
    )(page_table, sel_idx, q8, gate, o_cmp, new_rows, win_t, cache_t)


DIFF_PAGES_PER_STEP = 8
DIFF_ROW_STRIDE = 2 * DIFF_HEADS


def _lanes_to_heads(x, half):
    return jnp.concatenate([jnp.broadcast_to(x[:, DIFF_HEADS * half + h:DIFF_HEADS * half + h + 1], (1, LANES))
                            for h in range(DIFF_HEADS)], axis=1)


def _diff_sample_kernel(pt_ref, qt_ref, new_ref, lam_ref, ng_ref, *rest, lam_init):
    pages = rest[:DIFF_PAGES_PER_STEP]
    o_ref, m_ref, l_ref, acc_ref = rest[DIFF_PAGES_PER_STEP:]
    c = pl.program_id(1)
    hw = DIFF_HEADS * LANES
    qts = [_pad_rows(qt_ref[:, h * LANES:(h + 1) * LANES], LANES) for h in range(DIFF_HEADS)]
    r_io = lax.broadcasted_iota(jnp.int32, (LANES, hw), 0)
    c_io = lax.broadcasted_iota(jnp.int32, (LANES, hw), 1)
    expand = [(c_io // LANES == r_io - DIFF_HEADS * half).astype(BF16) for half in range(2)]

    @pl.when(c == 0)
    def _():
        s_new = jnp.zeros((SUBLANES, LANES), F32)
        for h in range(DIFF_HEADS):
            s_new = s_new + _dot_nt(_pad_rows(new_ref[:, h * LANES:(h + 1) * LANES], SUBLANES).astype(BF16), qts[h])
        m_ref[...] = s_new[0:1]
        l_ref[...] = jnp.ones((1, LANES), F32)
        v_new = _pad_rows(new_ref[:, hw:2 * hw].astype(BF16).astype(F32), SUBLANES)
        acc_ref[0] = v_new
        acc_ref[1] = v_new

    scores = []
    for page in pages:
        s = jnp.zeros((PAGE_SIZE, LANES), F32)
        for h in range(DIFF_HEADS):
            s = s + _dot_nt(page[pl.ds(h, PAGE_SIZE, stride=DIFF_ROW_STRIDE), :].astype(BF16), qts[h])
        scores.append(s)
    m_old = m_ref[...]
    m_new = m_old
    for s in scores:
        m_new = jnp.maximum(m_new, jnp.max(s, axis=0, keepdims=True))
    alpha = jnp.exp(m_old - m_new)
    l = l_ref[...] * alpha
    acc = [acc_ref[half] * _lanes_to_heads(alpha, half) for half in range(2)]
    for page, s in zip(pages, scores):
        e = jnp.exp(s - m_new)
        l = l + jnp.sum(e, axis=0, keepdims=True)
        eb = e.astype(BF16)
        v = jnp.concatenate([page[pl.ds(DIFF_HEADS + h, PAGE_SIZE, stride=DIFF_ROW_STRIDE), :]
                             for h in range(DIFF_HEADS)], axis=1)
        for half in range(2):
            w = _dot(eb, expand[half]) * v
            acc[half] = acc[half] + jnp.sum(w.reshape(PAGE_SIZE // SUBLANES, SUBLANES, hw), axis=0)
    m_ref[...] = m_new
    l_ref[...] = l
    acc_ref[0] = acc[0]
    acc_ref[1] = acc[1]

    @pl.when(c == pl.num_programs(1) - 1)
    def _():
        inv = 1.0 / l
        lam = _diff_lambda(lam_ref, lam_init)
        o = (jnp.sum(acc[0], axis=0, keepdims=True) * _lanes_to_heads(inv, 0)
             - lam * jnp.sum(acc[1], axis=0, keepdims=True) * _lanes_to_heads(inv, 1))
        for h in range(DIFF_HEADS):
            oh = o[:, h * LANES:(h + 1) * LANES]
            o_ref[:, h * LANES:(h + 1) * LANES] = _rms(oh, ng_ref[...], DIFF_NORM_EPS) * (1.0 - lam_init)


def diff_attend_sample(page_table, qt, new_rows, lam_vec, norm_g, cache_v, lam_init):
    bs, n_pages = page_table.shape
    p = DIFF_PAGES_PER_STEP
    hw = DIFF_HEADS * LANES
    page_specs = [pl.BlockSpec((None, PAGE_SIZE * DIFF_ROW_STRIDE, LANES),
                               functools.partial(lambda i, c, pt, k: (pt[i, c * p + k], 0, 0), k=k))
                  for k in range(p)]
    grid_spec = pltpu.PrefetchScalarGridSpec(
        num_scalar_prefetch=1,
        grid=(bs, n_pages // p),
        in_specs=[pl.BlockSpec((None, 2 * DIFF_HEADS, hw), lambda i, c, pt: (i, 0, 0)),
                  pl.BlockSpec((None, 1, 2 * hw), lambda i, c, pt: (i, 0, 0)),
                  pl.BlockSpec(lam_vec.shape, lambda i, c, pt: (0, 0)),
                  pl.BlockSpec((1, LANES), lambda i, c, pt: (0, 0))] + page_specs,
        out_specs=pl.BlockSpec((None, 1, hw), lambda i, c, pt: (i, 0, 0)),
        scratch_shapes=[pltpu.VMEM((1, LANES), F32), pltpu.VMEM((1, LANES), F32),
                        pltpu.VMEM((2, SUBLANES, hw), F32)],
    )
    return pl.pallas_call(
        functools.partial(_diff_sample_kernel, lam_init=lam_init),
        grid_spec=grid_spec,
        out_shape=jax.ShapeDtypeStruct((bs, 1, hw), F32),
        compiler_params=_cparams("parallel", "arbitrary"),
        name="diff_attend_sample",
    )(page_table, qt, new_rows, lam_vec, norm_g.reshape(1, LANES), *([cache_v] * p))


def _col_rep(row):
    x = jnp.broadcast_to(row, (LANES, row.shape[1]))
    return jnp.concatenate([x[:, c * LANES:(c + 1) * LANES].T for c in range(row.shape[1] // LANES)], axis=0)


def _head_sum(x):
    return jnp.sum(x.reshape(DIL_HEADS, HEAD_DIM, x.shape[1]), axis=1)


def _head_expand(x):
    return jnp.broadcast_to(x[:, None, :], (DIL_HEADS, HEAD_DIM, x.shape[1])).reshape(
        DIL_HEADS * HEAD_DIM, x.shape[1])


def _dil_sample_kernel(q_ref, new0_ref, new1_ref, new2_ref, st0_ref, st1_ref, st2_ref, o_ref):
    hw = DIL_HEADS * HEAD_DIM
    outs, lses = [], []
    for g, (new_ref, st_ref) in enumerate(((new0_ref, st0_ref), (new1_ref, st1_ref), (new2_ref, st2_ref))):
        win, dil = DIL_GROUPS[g]
        qc = _col_rep(q_ref[:, g * hw:(g + 1) * hw])
        kn = _col_rep(new_ref[:, 0:hw])
        vn = _col_rep(new_ref[:, hw:2 * hw])
        s_new = _head_sum(qc * kn)[:, 0:1]
        n_chunks = win // LANES
        s = jnp.concatenate([_head_sum(st_ref[0:hw, c * LANES:(c + 1) * LANES] * qc) for c in range(n_chunks)],
                            axis=1)
        lane = lax.broadcasted_iota(jnp.int32, s.shape, 1)
        s = jnp.where(lane % dil == 0, s, -jnp.inf)
        m = jnp.maximum(jnp.max(s, axis=-1, keepdims=True), s_new)
        e = jnp.exp(s - m)
        e_new = jnp.exp(s_new - m)
        den = jnp.sum(e, axis=-1, keepdims=True) + e_new
        p = e / den
        acc = _head_expand(jnp.broadcast_to(e_new / den, (DIL_HEADS, LANES))) * vn * (1.0 / LANES)
        for c in range(n_chunks):
            cs = slice(c * LANES, (c + 1) * LANES)
            acc = acc + st_ref[hw:2 * hw, cs] * _head_expand(p[:, cs])
        outs.append(jnp.sum(acc, axis=-1, keepdims=True))
        lses.append(jnp.log(den) + m)
    mx = jnp.maximum(jnp.maximum(lses[0], lses[1]), lses[2])
    es = [jnp.exp(l - mx) for l in lses]
    tot = es[0] + es[1] + es[2]
    mix = jnp.zeros((hw, LANES), F32)
    for g in range(len(DIL_GROUPS)):
        alpha = _head_expand(jnp.broadcast_to(es[g] / tot, (DIL_HEADS, LANES)))
        mix = mix + alpha * jnp.broadcast_to(outs[g], (hw, LANES))
    rows = jnp.concatenate([mix[c * LANES:(c + 1) * LANES, :].T for c in range(hw // LANES)], axis=1)
    o_ref[...] = rows[0:1]


def dil_attend_sample(q, news, states_t):
    bs = q.shape[0]
    hw = DIL_HEADS * HEAD_DIM
    st_specs = [pl.BlockSpec((None, 2 * hw, st.shape[2]), lambda i: (i, 0, 0)) for st in states_t]
    return pl.pallas_call(
        _dil_sample_kernel,
        grid=(bs,),
        in_specs=[pl.BlockSpec((None, 1, q.shape[2]), lambda i: (i, 0, 0))]
                 + [pl.BlockSpec((None, 1, 2 * hw), lambda i: (i, 0, 0))] * 3 + st_specs,
        out_specs=pl.BlockSpec((None, 1, hw), lambda i: (i, 0, 0)),
        out_shape=jax.ShapeDtypeStruct((bs, 1, hw), F32),
        compiler_params=_cparams("parallel"),
        name="dil_attend_sample",
    )(q, *news, *states_t)


AB_SIZES = (NSA_HEADS * HEAD_DIM, 6 * NSA_KV_HEADS * HEAD_DIM, 3 * NSA_HEADS,
            DIFF_HEADS * 2 * HEAD_DIM, DIFF_HEADS * 2 * HEAD_DIM, DIFF_HEADS * 2 * HEAD_DIM)
_QSCALE = HEAD_DIM ** -0.5
_QSCALE2 = _QSCALE * LOG2E


def _rope_tables(pos):
    half = HEAD_DIM // 2
    inv = ROPE_THETA ** (-jnp.arange(half, dtype=F32) / half)
    ang = pos.astype(F32)[:, None] * inv[None, :]
    c, s = jnp.cos(ang), jnp.sin(ang)
    return jnp.tile(c, (1, 4)), jnp.tile(jnp.concatenate([-s, s], axis=1), (1, 2))


def _pad_heads(w, offsets):
    z = jnp.zeros_like(w)
    lo = jnp.concatenate([w, z], axis=-1)
    hi = jnp.concatenate([z, w], axis=-1)
    at_lo = (np.asarray(offsets) == 0)[None, :, None]
    return jnp.where(at_lo, lo, hi).reshape(w.shape[0], -1)


def _prep_w_ab(w):
    d = w.shape[0]
    qa, kvb, gl, qd, kd, vd = jnp.split(w, np.cumsum(AB_SIZES)[:-1].tolist(), axis=1)
    qa = _pad_heads(qa.reshape(d, NSA_HEADS, HEAD_DIM), [(h // NSA_GROUP) * HEAD_DIM for h in range(NSA_HEADS)])
    kvb = kvb.reshape(d, 6, LANES)
    k3 = kvb[:, 0::2].reshape(d, 3 * LANES)
    v3 = kvb[:, 1::2].reshape(d, 3 * LANES)
    qd = _pad_heads(qd.reshape(d, 2 * DIFF_HEADS, HEAD_DIM), [(i % 2) * HEAD_DIM for i in range(2 * DIFF_HEADS)])
    gl = jnp.pad(gl, ((0, 0), (0, LANES - gl.shape[1])))
    return jnp.concatenate([qa, k3, v3, qd, kd, vd, gl], axis=1).astype(BF16)


def _ab_plan(dest, qscale):
    return (
        (0, 8, True, qscale, tuple(dest("qa", j) for j in range(8))),
        (1024, 3, True, 1.0, (dest("k_cmp", 0), dest("k_slc", 0), dest("k_win", 0))),
        (1408, 3, False, 1.0, (dest("v_cmp", 0), dest("v_slc", 0), dest("v_win", 0))),
        (1792, 8, True, qscale, tuple(dest("qd", j) for j in range(8))),
        (2816, 4, True, 1.0, tuple(dest("kd", j) for j in range(4))),
        (3328, 4, False, 1.0, tuple(dest("vd", j) for j in range(4))),
        (3840, 1, False, 1.0, (dest("gate", 0),)),
    )


def _ab_prompt_defs(t):
    defs = (("T", 1024, BF16, t), ("T", 512, F32, t), ("N", 128, BF16), ("N", 256, F32),
            ("T", 256, F32, min(NSA_WINDOW, t)), ("N", 128, BF16), ("T", 1024, BF16, t), ("N", 1024, F32),
            ("N", 512, BF16), ("T", 128, F32, t), ("T", 128, BF16, t), ("T", 128, BF16, t), ("T", 512, BF16, t))
    table = {
        "qa": lambda j: ((0, j),), "qd": lambda j: ((6, j),), "gate": lambda j: ((9, 0),),
        "k_cmp": lambda j: ((1, 0), (3, 0)), "v_cmp": lambda j: ((1, 1), (3, 1)),
        "k_slc": lambda j: ((1, 2), (2, 0)), "v_slc": lambda j: ((1, 3), (11, 0)),
        "k_win": lambda j: ((4, 0), (5, 0)), "v_win": lambda j: ((4, 1), (10, 0)),
        "kd": lambda j: ((7, j), (8, j)), "vd": lambda j: ((7, 4 + j), (12, j)),
    }
    return defs, _ab_plan(lambda name, j: table[name](j), _QSCALE2)


def _ab_sample_defs():
    defs = (("N", 1024, BF16), ("N", 512, F32), ("N", 256, F32), ("N", 1024, BF16), ("N", 1024, F32),
            ("N", 128, F32))
    table = {
        "qa": lambda j: ((0, j),), "qd": lambda j: ((3, j),), "gate": lambda j: ((5, 0),),
        "k_cmp": lambda j: ((1, 0),), "v_cmp": lambda j: ((1, 1),),
        "k_slc": lambda j: ((1, 2),), "v_slc": lambda j: ((1, 3),),
        "k_win": lambda j: ((2, 0),), "v_win": lambda j: ((2, 1),),
        "kd": lambda j: ((4, j),), "vd": lambda j: ((4, 4 + j),),
    }
    return defs, _ab_plan(lambda name, j: table[name](j), _QSCALE)


def _prep_w_c(w):
    d = w.shape[0]
    ng = len(DIL_GROUPS)
    w = w.reshape(d, ng, 3, DIL_HEADS, HEAD_DIM)
    cols = []
    for g in range(ng):
        cols.append(_pad_heads(w[:, g, 0], [(h % 2) * HEAD_DIM for h in range(DIL_HEADS)]))
        cols.append(w[:, g, 1].reshape(d, -1))
        cols.append(w[:, g, 2].reshape(d, -1))
    return jnp.concatenate(cols, axis=1).astype(BF16)


def _c_prompt_defs(t):
    defs, plan = [], []
    for g, (win, dil) in enumerate(DIL_GROUPS):
        defs += [("R", 1024, BF16, dil), ("R", 1024, BF16, dil), ("T", 1024, F32, min(win, t))]
        plan += [
            (g * 2048, 8, True, _QSCALE2, tuple(((3 * g, j),) for j in range(8))),
            (g * 2048 + 1024, 4, True, 1.0, tuple(((3 * g + 1, j), (3 * g + 2, j)) for j in range(4))),
            (g * 2048 + 1536, 4, False, 1.0, tuple(((3 * g + 1, 4 + j), (3 * g + 2, 4 + j)) for j in range(4))),
        ]
    return tuple(defs), tuple(plan)


def _c_sample_defs():
    defs = (("N", 1536, F32), ("N", 1024, F32), ("N", 1024, F32), ("N", 1024, F32))
    plan = []
    for g in range(len(DIL_GROUPS)):
        plan += [
            (g * 1536, 4, True, _QSCALE, tuple(((0, 4 * g + j),) for j in range(4))),
            (g * 1536 + 512, 4, True, 1.0, tuple(((1 + g, j),) for j in range(4))),
            (g * 1536 + 1024, 4, False, 1.0, tuple(((1 + g, 4 + j),) for j in range(4))),
        ]
    return defs, tuple(plan)


def _prep_cmp(w_cmp, pe_cmp):
    wk, wv = w_cmp[0], w_cmp[1]
    z = jnp.zeros_like(wk)
    w4 = jnp.concatenate([jnp.concatenate([wk, z, z, z], axis=-1), jnp.concatenate([z, wk, z, z], axis=-1),
                          jnp.concatenate([z, z, wv, z], axis=-1), jnp.concatenate([z, z, z, wv], axis=-1)],
                         axis=1).astype(BF16)
    pe4 = jnp.concatenate([pe_cmp[0], pe_cmp[0], pe_cmp[1], pe_cmp[1]], axis=-1)
    half = CMP_LEN // 2
    return pe4, w4[:half], w4[half:]


def _block_indicator(n_keys):
    return (jnp.arange(n_keys)[:, None] // SLC_BLOCK == jnp.arange(LANES)[None, :]).astype(BF16)


def _cmp_to_block(n_cmp):
    r = SLC_BLOCK // CMP_STRIDE
    return (jnp.arange(n_cmp)[None, :] // r == jnp.arange(LANES)[:, None]).astype(BF16)


def _rows_from_t(x_t, lead):
    b, _, r = x_t.shape
    nd = len(lead)
    return x_t.reshape((b,) + tuple(lead) + (HEAD_DIM, r)).transpose((0, nd + 2) + tuple(range(1, nd + 2)))


def _rows_to_t(x):
    b, r = x.shape[:2]
    nd = x.ndim
    return x.transpose((0,) + tuple(range(2, nd)) + (1,)).reshape(b, -1, r)


def ab_mix_prompt(h, b, t, g_in, w_ab, cmp_prep, lam_vec, dn_g, lam_init, cos, sin):
    defs, plan = _ab_prompt_defs(t)
    (q_nsa_t, rows_nsa_t, kslc16, cmp_rows, rows_win_t, kwin16, q_diff_t, rows_diff, k_diff16, gate_t, vwin_t,
     vslc_t, v_diff_t) = project(h, b, g_in, cos, sin, w_ab, plan, defs)
    r3 = lambda x: x.reshape(b, t, x.shape[-1])
    pe4, wa, wb = cmp_prep
    kcmp16, vcmp_t = nsa_compress_prompt(r3(cmp_rows), pe4, wa, wb)
    o_nsa = nsa_attend_prompt(q_nsa_t, gate_t, kcmp16, vcmp_t, r3(kslc16), vslc_t, r3(kwin16), vwin_t,
                              _block_indicator(t), _cmp_to_block(t // CMP_STRIDE))
    o_diff = diff_attend_prompt(q_diff_t, r3(k_diff16), v_diff_t, lam_vec, dn_g, lam_init)
    mixed = [o_nsa.reshape(b * t, -1), o_diff.reshape(b * t, -1)]
    return mixed, rows_nsa_t, rows_win_t, r3(rows_diff)


def dil_mix_prompt(h, b, t, g_in, w_c, cos, sin):
    defs, plan = _c_prompt_defs(t)
    outs = project(h, b, g_in, cos, sin, w_c, plan, defs)
    os_, lses, rows_t = [], [], []
    for gi in range(len(DIL_GROUPS)):
        o, lse = dil_attend_prompt(outs[3 * gi], outs[3 * gi + 1])
        os_.append(o)
        lses.append(lse)
        rows_t.append(outs[3 * gi + 2])
    return os_, lses, rows_t


def _diff_qt(q_diff):
    bs = q_diff.shape[0]
    qd = q_diff.reshape(bs, DIFF_HEADS, 2, LANES).transpose(0, 2, 1, 3)
    eye = jnp.eye(DIFF_HEADS, dtype=q_diff.dtype)
    return (qd[:, :, :, None, :] * eye[None, None, :, :, None]).reshape(bs, 2 * DIFF_HEADS, DIFF_HEADS * LANES)


def ab_mix_sample(hs, g_in, w_ab, cmp_prep, lam_vec, dn_g, lam_init, cos, sin,
                  cache_nsa, cache_diff, win_state, page_table):
    bs = hs.shape[0]
    defs, plan = _ab_sample_defs()
    q_nsa, rows_nsa, rows_win, q_diff, rows_diff, gate = project(hs, 1, g_in, cos, sin, w_ab, plan, defs)
    n_pages = page_table.shape[1]
    past = n_pages * PAGE_SIZE
    assert past // SLC_BLOCK == LANES, "selection-block axis is laid out on the 128 lanes"
    n_pool = cache_nsa.shape[0]
    cache_t = _rows_to_t(cache_nsa)
    cache_v = cache_diff.reshape(n_pool, PAGE_SIZE * DIFF_ROW_STRIDE, LANES)
    pe4, wa, wb = cmp_prep
    cmp_kv = nsa_compress_sample(cache_t, page_table, pe4, wa, wb)
    q8 = q_nsa.reshape(bs, NSA_HEADS, LANES)
    o_cmp, imp = nsa_sample_cmp(q8, cmp_kv, _cmp_to_block(past // CMP_STRIDE), past)
    sel_idx = topk_lanes(imp, N_SEL_CACHE)[:, :NSA_KV_HEADS, :N_SEL_CACHE].reshape(bs, -1)
    new_rows = jnp.concatenate([rows_nsa, rows_win], axis=-1).reshape(bs, 1, -1)
    o8 = nsa_sample_attend(page_table, sel_idx, q8, gate.reshape(bs, 1, LANES), o_cmp, new_rows,
                           _rows_to_t(win_state), cache_t)
    o8 = o8.reshape(bs, NSA_HEADS, 2, HEAD_DIM)
    o_nsa = jnp.concatenate([o8[:, :NSA_GROUP, 0], o8[:, NSA_GROUP:, 1]], axis=1).reshape(bs, -1)
    o_diff = diff_attend_sample(page_table, _diff_qt(q_diff), rows_diff.reshape(bs, 1, -1), lam_vec, dn_g,
                                cache_v, lam_init).reshape(bs, -1)
    mixed = jnp.concatenate([o_nsa, o_diff], axis=-1).astype(BF16)
    return mixed, rows_nsa, rows_win, rows_diff


def dil_mix_sample(hs, g_in, w_c_rows, cos, sin, states):
    bs = hs.shape[0]
    defs, plan = _c_sample_defs()
    outs = project(hs, 1, g_in, cos, sin, w_c_rows, plan, defs)
    news = [x.reshape(bs, 1, -1) for x in outs[1:]]
    sts = []
    for (win, dil), st in zip(DIL_GROUPS, states):
        assert st.shape[1] == win and win == DIL_BAND * dil, "state buffer must hold the full dilated window"
        sts.append(_rows_to_t(st))
    o = dil_attend_sample(outs[0].reshape(bs, 1, -1), news, sts).reshape(bs, -1)
    return o.astype(BF16), outs[1:]


def kernel(x_prompt, x_sample, cache_nsa, cache_diff, state_nsa_win, state_dil_0, state_dil_1, state_dil_2,
           page_table, norm_g, ffn_w_in, ffn_w_out, w_in_ab, w_out_ab, nsa_w_cmp, nsa_pe_cmp, diff_lambda,
           diff_norm_g, w_in_c, w_out_c):
    b, t, d = x_prompt.shape
    bs, ns, _ = x_sample.shape
    assert ns == 1, "sample group is one new token per sequence"
    depth = norm_g.shape[0]
    past = page_table.shape[1] * PAGE_SIZE
    hp = x_prompt.reshape(b * t, d)
    hs = x_sample.reshape(bs, d)
    cos_p, sin_p = _rope_tables(jnp.tile(jnp.arange(t, dtype=jnp.int32), b))
    cos_s, sin_s = _rope_tables(jnp.full((bs,), past, jnp.int32))
    w_ffn_in = ffn_w_in.astype(BF16)
    w_ffn_out = ffn_w_out.astype(BF16)
    state_dil = (state_dil_0, state_dil_1, state_dil_2)
    nsa_p, nsa_s, win_p, win_s, diff_p, diff_s = [], [], [], [], [], []
    dil_p = [[] for _ in DIL_GROUPS]
    dil_s = [[] for _ in DIL_GROUPS]
    for layer in range(depth):
        g = norm_g[layer]
        hp = ffn_half(hp, g[0], g[1], w_ffn_in[layer, 0], w_ffn_out[layer, 0])
        hs = ffn_half(hs, g[0], g[1], w_ffn_in[layer, 0], w_ffn_out[layer, 0])
        if layer % 2 == 0:
            e = layer // 2
            lam_init = 0.8 - 0.6 * math.exp(-0.3 * layer)
            w_ab = _prep_w_ab(w_in_ab[e])
            cmp_prep = _prep_cmp(nsa_w_cmp[e], nsa_pe_cmp[e])
            mp, rn_t, rw_t, rd = ab_mix_prompt(hp, b, t, g[2], w_ab, cmp_prep, diff_lambda[e], diff_norm_g[e],
                                               lam_init, cos_p, sin_p)
            nsa_p.append(_rows_from_t(rn_t, (4, NSA_KV_HEADS)))
            win_p.append(_rows_from_t(rw_t[:, :, -min(NSA_WINDOW, t):], (2, NSA_KV_HEADS)))
            diff_p.append(rd.reshape(b, t, 2, DIFF_HEADS, 2 * HEAD_DIM))
            ms, rn, rw, rd = ab_mix_sample(hs, g[2], w_ab, cmp_prep, diff_lambda[e], diff_norm_g[e], lam_init,
                                           cos_s, sin_s, cache_nsa[e], cache_diff[e], state_nsa_win[e], page_table)
            nsa_s.append(rn.reshape(bs, 1, 4, NSA_KV_HEADS, HEAD_DIM))
            win_full = jnp.concatenate([state_nsa_win[e], rw.reshape(bs, 1, 2, NSA_KV_HEADS, HEAD_DIM)], axis=1)
            win_s.append(win_full[:, -min(NSA_WINDOW, win_full.shape[1]):])
            diff_s.append(rd.reshape(bs, 1, 2, DIFF_HEADS, 2 * HEAD_DIM))
            w_o = w_out_ab[e].astype(BF16)
            hp = outproj(hp, mp, w_o, g[3])
            hs = outproj(hs, [ms], w_o, g[3])
        else:
            o = layer // 2
            w_o = w_out_c[o].astype(BF16)
            os_, lses, rows_t = dil_mix_prompt(hp, b, t, g[2], _prep_w_c(w_in_c[o]), cos_p, sin_p)
            hp = outproj_dil(hp, b, os_, lses, w_o, g[3])
            ms, news = dil_mix_sample(hs, g[2], w_in_c[o].astype(BF16), cos_s, sin_s, [st[o] for st in state_dil])
            hs = outproj(hs, [ms], w_o, g[3])
            for gi, (win, dil) in enumerate(DIL_GROUPS):
                dil_p[gi].append(_rows_from_t(rows_t[gi][:, :, -min(win, t):], (2, DIL_HEADS)))
                full = jnp.concatenate([state_dil[gi][o], news[gi].reshape(bs, 1, 2, DIL_HEADS, HEAD_DIM)], axis=1)
                dil_s[gi].append(full[:, -min(win, full.shape[1]):])
        hp = ffn_half(hp, g[4], g[5], w_ffn_in[layer, 1], w_ffn_out[layer, 1])
        hs = ffn_half(hs, g[4], g[5], w_ffn_in[layer, 1], w_ffn_out[layer, 1])
    return (hp.reshape(b, t, d), hs.reshape(bs, 1, d), jnp.stack(nsa_p), jnp.stack(nsa_s), jnp.stack(win_p),
            jnp.stack(win_s), jnp.stack(diff_p), jnp.stack(diff_s), jnp.stack(dil_p[0]), jnp.stack(dil_s[0]),
            jnp.stack(dil_p[1]), jnp.stack(dil_s[1]), jnp.stack(dil_p[2]), jnp.stack(dil_s[2]))
```

```python
import functools
import math

import jax
import jax.numpy as jnp
import numpy as np
from jax import lax
from jax.experimental import pallas as pl
from jax.experimental.pallas import tpu as pltpu

F32 = jnp.float32
BF16 = jnp.bfloat16

LANES = 128
SUBLANES = 8
HEAD_DIM = 64
ROPE_THETA = 10000.0
NORM_EPS = 1e-6
PAGE_SIZE = 128
NSA_HEADS = 8
NSA_KV_HEADS = 2
NSA_GROUP = NSA_HEADS // NSA_KV_HEADS
CMP_LEN = 32
CMP_STRIDE = 16
SLC_BLOCK = 64
N_SEL = 16
NSA_WINDOW = 512
FORCED_BLOCK_SCORE = 1.0e4
DIFF_HEADS = 4
DIFF_NORM_EPS = 1e-5
DIL_GROUPS = ((128, 1), (512, 4), (2048, 16))
DIL_HEADS = 8
DIL_BAND = 128
NEG_BIG = -1.0e30
LN2 = math.log(2.0)
LOG2E = 1.0 / LN2
VMEM_LIMIT_BYTES = 56 * 1024 * 1024
TOKEN_TILE = 512


def _cparams(*sem):
    return pltpu.CompilerParams(dimension_semantics=sem, vmem_limit_bytes=VMEM_LIMIT_BYTES)


def _const_spec(shape):
    nd = len(shape)
    return pl.BlockSpec(shape, lambda *_: (0,) * nd, pipeline_mode=pl.Buffered(1))


def _rms(x, g, eps):
    return x * lax.rsqrt(jnp.mean(x * x, axis=-1, keepdims=True) + eps) * g


def _dot(a, b):
    return jnp.dot(a, b, preferred_element_type=F32)


def _dot_nt(a, b):
    return lax.dot_general(a, b, (((1,), (1,)), ((), ())), preferred_element_type=F32)


def _split3(x):
    hi = x.astype(BF16)
    r1 = x - hi.astype(F32)
    mid = r1.astype(BF16)
    lo = (r1 - mid.astype(F32)).astype(BF16)
    return hi, mid, lo


def _pad_rows(x, rows):
    return jnp.concatenate([x, jnp.zeros((rows - x.shape[0], x.shape[1]), x.dtype)], axis=0)


def _token_tile(n):
    return TOKEN_TILE if n % TOKEN_TILE == 0 else n


FFN_CHUNK = 256


def _ffn_kernel(x_ref, gpre_ref, gpost_ref, win_ref, wout_ref, o_ref, *, d_ff):
    x = x_ref[...]
    xn = _rms(x, gpre_ref[...], NORM_EPS).astype(BF16)
    acc = jnp.zeros(x.shape, F32)
    for c in range(d_ff // FFN_CHUNK):
        lo = c * FFN_CHUNK
        gate = _dot(xn, win_ref[:, lo:lo + FFN_CHUNK])
        up = _dot(xn, win_ref[:, d_ff + lo:d_ff + lo + FFN_CHUNK])
        act = (gate * jax.nn.sigmoid(gate) * up).astype(BF16)
        acc = acc + _dot(act, wout_ref[lo:lo + FFN_CHUNK, :])
    o_ref[...] = x + 0.5 * _rms(acc, gpost_ref[...], NORM_EPS)


def ffn_half(h, g_pre, g_post, w_in, w_out):
    n, d = h.shape
    d_ff = w_out.shape[0]
    tm = _token_tile(n)
    return pl.pallas_call(
        functools.partial(_ffn_kernel, d_ff=d_ff),
        grid=(n // tm,),
        in_specs=[pl.BlockSpec((tm, d), lambda i: (i, 0)),
                  _const_spec((1, d)), _const_spec((1, d)),
                  _const_spec(w_in.shape), _const_spec(w_out.shape)],
        out_specs=pl.BlockSpec((tm, d), lambda i: (i, 0)),
        out_shape=jax.ShapeDtypeStruct((n, d), F32),
        compiler_params=_cparams("parallel"),
        name="ffn_half",
    )(h, g_pre.reshape(1, d), g_post.reshape(1, d), w_in, w_out)


def _rope_slab(y, cos, sin):
    lane = lax.broadcasted_iota(jnp.int32, y.shape, 1)
    swapped = jnp.where(lane % HEAD_DIM < HEAD_DIM // 2,
                        pltpu.roll(y, LANES - HEAD_DIM // 2, 1),
                        pltpu.roll(y, HEAD_DIM // 2, 1))
    return y * cos + swapped * sin


def _proj_kernel(x_ref, g_ref, cos_ref, sin_ref, w_ref, *refs, plan, out_defs, first_tiles, tiles_per_b):
    out_refs = refs[:len(out_defs)]
    scr_ref = refs[len(out_defs)]
    tm = x_ref.shape[0]
    tile_in_b = pl.program_id(0) % tiles_per_b
    xn = _rms(x_ref[...], g_ref[...], NORM_EPS).astype(BF16)
    cos = cos_ref[...]
    sin = sin_ref[...]
    lane = lax.broadcasted_iota(jnp.int32, (tm, LANES), 1)

    def emit(val, val_t, out_idx, slab):
        ref = out_refs[out_idx]
        kind = out_defs[out_idx][0]
        cs = slice(slab * LANES, (slab + 1) * LANES)
        if kind == "N":
            ref[:, cs] = val().astype(ref.dtype)
        elif kind == "I":
            ref[pl.ds(slab, tm, stride=out_defs[out_idx][1]), :] = val().astype(ref.dtype)
        elif kind == "T":
            def write_t():
                ref[cs, :] = val_t().astype(ref.dtype)
            if first_tiles[out_idx] == 0:
                write_t()
            else:
                pl.when(tile_in_b >= first_tiles[out_idx])(write_t)
        else:
            dil = out_defs[out_idx][3]
            if dil == 1:
                ref[0, :, cs] = val().astype(ref.dtype)
            else:
                scr_ref[...] = val()
                for r in range(dil):
                    ref[r, :, cs] = scr_ref[pl.ds(r, tm // dil, stride=dil), :].astype(ref.dtype)

    for col0, nslab, rope, scale, pad, dests in plan:
        y = _dot(xn, w_ref[:, col0:col0 + nslab * LANES])
        for j in range(nslab):
            ys = y[:, j * LANES:(j + 1) * LANES]
            if rope:
                ys = _rope_slab(ys, cos, sin)
            if scale != 1.0:
                ys = ys * scale
            if pad is None:
                for out_idx, slab in dests[j]:
                    emit(lambda ys=ys: ys, lambda ys=ys: ys.T, out_idx, slab)
                continue
            transposed = []

            def ys_t(ys=ys, transposed=transposed):
                if not transposed:
                    transposed.append(ys.T)
                return transposed[0]

            for hh in range(2):
                head = 2 * j + hh
                at_hi = hh == 1 if pad == "pair" else head >= nslab

                def val(ys=ys, hh=hh, at_hi=at_hi):
                    v = ys if at_hi == (hh == 1) else pltpu.roll(ys, HEAD_DIM, 1)
                    return jnp.where(lane >= HEAD_DIM if at_hi else lane < HEAD_DIM, v, 0.0)

                def val_t(ys_t=ys_t, hh=hh, at_hi=at_hi):
                    rows = ys_t()[hh * HEAD_DIM:(hh + 1) * HEAD_DIM, :]
                    zero = jnp.zeros_like(rows)
                    return jnp.concatenate([zero, rows] if at_hi else [rows, zero], axis=0)

                for out_idx, slab in dests[head]:
                    emit(val, val_t, out_idx, slab)


def project(h, b, g, cos, sin, w, plan, out_defs):
    n, d = h.shape
    t = n // b
    tm = _token_tile(t)
    tpb = t // tm
    specs, shapes, first_tiles = [], [], []
    for od in out_defs:
        kind, c, dt = od[:3]
        if kind == "N":
            specs.append(pl.BlockSpec((tm, c), lambda i: (i, 0)))
            shapes.append(jax.ShapeDtypeStruct((n, c), dt))
            first_tiles.append(0)
        elif kind == "I":
            specs.append(pl.BlockSpec((tm * c, LANES), lambda i: (i, 0)))
            shapes.append(jax.ShapeDtypeStruct((n * c, LANES), dt))
            first_tiles.append(0)
        elif kind == "T":
            keep = max(min(od[3], t), tm)
            ft = (t - keep) // tm
            specs.append(pl.BlockSpec((None, c, tm),
                                      functools.partial(lambda i, ft: (i // tpb, 0, jnp.maximum(i % tpb - ft, 0)),
                                                        ft=ft)))
            shapes.append(jax.ShapeDtypeStruct((b, c, keep), dt))
            first_tiles.append(ft)
        else:
            dil = od[3]
            specs.append(pl.BlockSpec((None, dil, tm // dil, c), lambda i: (i // tpb, 0, i % tpb, 0)))
            shapes.append(jax.ShapeDtypeStruct((b, dil, t // dil, c), dt))
            first_tiles.append(0)
    return pl.pallas_call(
        functools.partial(_proj_kernel, plan=plan, out_defs=out_defs, first_tiles=tuple(first_tiles),
                          tiles_per_b=tpb),
        grid=(n // tm,),
        in_specs=[pl.BlockSpec((tm, d), lambda i: (i, 0)), _const_spec((1, d)),
                  pl.BlockSpec((tm, LANES), lambda i: (i, 0)),
                  pl.BlockSpec((tm, LANES), lambda i: (i, 0)),
                  _const_spec(w.shape)],
        out_specs=specs,
        out_shape=shapes,
        scratch_shapes=[pltpu.VMEM((tm, LANES), F32)],
        compiler_params=_cparams("arbitrary"),
        name="project",
    )(h, g.reshape(1, d), cos, sin, w)


def _outproj_kernel(h_ref, *refs):
    w_ref, g_ref, o_ref = refs[-3:]
    y, row0 = None, 0
    for m_ref in refs[:-3]:
        c = m_ref.shape[1]
        part = _dot(m_ref[...], w_ref[row0:row0 + c, :])
        y = part if y is None else y + part
        row0 += c
    o_ref[...] = h_ref[...] + _rms(y, g_ref[...], NORM_EPS)


def outproj(h, ms, w, g):
    n, d = h.shape
    tm = _token_tile(n)
    return pl.pallas_call(
        _outproj_kernel,
        grid=(n // tm,),
        in_specs=[pl.BlockSpec((tm, d), lambda i: (i, 0))]
                 + [pl.BlockSpec((tm, m.shape[1]), lambda i: (i, 0)) for m in ms]
                 + [_const_spec(w.shape), _const_spec((1, d))],
        out_specs=pl.BlockSpec((tm, d), lambda i: (i, 0)),
        out_shape=jax.ShapeDtypeStruct((n, d), F32),
        compiler_params=_cparams("parallel"),
        name="outproj",
    )(h, *ms, w, g.reshape(1, d))


def _outproj_dil_kernel(h_ref, *refs):
    ng = len(DIL_GROUPS)
    w_ref, g_ref, o_ref, scr_ref = refs[2 * ng:]
    tm = h_ref.shape[0]
    nslab = DIL_HEADS * HEAD_DIM // LANES
    vals = []
    k = 0
    for gi, (_, dil) in enumerate(DIL_GROUPS):
        per_g = []
        for ref in (refs[2 * gi], refs[2 * gi + 1]):
            slabs = []
            for s in range(nslab):
                cs = slice(s * LANES, (s + 1) * LANES)
                if dil == 1:
                    slabs.append(ref[0, :, cs])
                else:
                    for r in range(dil):
                        scr_ref[k, pl.ds(r, tm // dil, stride=dil), :] = ref[r, :, cs]
                    slabs.append(scr_ref[k])
                    k += 1
            per_g.append(slabs)
        vals.append(per_g)
    mixed = []
    for s in range(nslab):
        l0, l1, l2 = vals[0][1][s], vals[1][1][s], vals[2][1][s]
        mx = jnp.maximum(jnp.maximum(l0, l1), l2)
        e0, e1, e2 = jnp.exp(l0 - mx), jnp.exp(l1 - mx), jnp.exp(l2 - mx)
        den = e0 + e1 + e2
        mixed.append(((e0 / den) * vals[0][0][s] + (e1 / den) * vals[1][0][s]
                      + (e2 / den) * vals[2][0][s]).astype(BF16))
    y = _dot(jnp.concatenate(mixed, axis=1), w_ref[...])
    o_ref[...] = h_ref[...] + _rms(y, g_ref[...], NORM_EPS)


def outproj_dil(h, b, outs, lses, w, g):
    n, d = h.shape
    t = n // b
    tm = _token_tile(t)
    tpb = t // tm
    c = DIL_HEADS * HEAD_DIM
    specs, args, n_scr = [], [], 0
    for (_, dil), o, l in zip(DIL_GROUPS, outs, lses):
        spec = pl.BlockSpec((None, dil, tm // dil, c), lambda i: (i // tpb, 0, i % tpb, 0))
        specs += [spec, spec]
        args += [o, l]
        if dil > 1:
            n_scr += 2 * (c // LANES)
    return pl.pallas_call(
        _outproj_dil_kernel,
        grid=(n // tm,),
        in_specs=[pl.BlockSpec((tm, d), lambda i: (i, 0))] + specs + [_const_spec(w.shape), _const_spec((1, d))],
        out_specs=pl.BlockSpec((tm, d), lambda i: (i, 0)),
        out_shape=jax.ShapeDtypeStruct((n, d), F32),
        scratch_shapes=[pltpu.VMEM((n_scr, tm, LANES), F32)],
        compiler_params=_cparams("parallel"),
        name="outproj_dil",
    )(h, *args, w, g.reshape(1, d))


def _compress_rows(k_ref, v_ref, pe_ref, wa_ref, wb_ref, nsub):
    half = CMP_LEN // 2
    acc_a = jnp.zeros((nsub, 2 * LANES), F32)
    acc_b = jnp.zeros((nsub, 2 * LANES), F32)
    for l in range(half):
        x = jnp.concatenate([k_ref[pl.ds(l, nsub, stride=CMP_STRIDE), :],
                             v_ref[pl.ds(l, nsub, stride=CMP_STRIDE), :]], axis=1)
        acc_a = acc_a + _dot((x + pe_ref[l:l + 1, :]).astype(BF16), wa_ref[l])
        acc_b = acc_b + _dot((x + pe_ref[half + l:half + l + 1, :]).astype(BF16), wb_ref[l])
    comp = acc_a + pltpu.roll(acc_b, nsub - 1, 0)
    row = lax.broadcasted_iota(jnp.int32, comp.shape, 0)
    return jnp.where(row < nsub - 1, comp, 0.0)


def _compress_kernel(k_ref, v_ref, pe_ref, wa_ref, wb_ref, k_out_ref, vt_out_ref, *, nsub):
    comp = _compress_rows(k_ref, v_ref, pe_ref, wa_ref, wb_ref, nsub)
    k_out_ref[...] = comp[:, 0:LANES].astype(k_out_ref.dtype)
    vt_out_ref[...] = comp[:, LANES:2 * LANES].T.astype(vt_out_ref.dtype)


def nsa_compress_prompt(cmp_rows, pe4, wa, wb):
    b, t, _ = cmp_rows.shape
    nsub = t // CMP_STRIDE
    return pl.pallas_call(
        functools.partial(_compress_kernel, nsub=nsub),
        grid=(b,),
        in_specs=[pl.BlockSpec((None, t, LANES), lambda i: (i, 0, 0)),
                  pl.BlockSpec((None, t, LANES), lambda i: (i, 0, 1)),
                  _const_spec(pe4.shape), _const_spec(wa.shape), _const_spec(wb.shape)],
        out_specs=[pl.BlockSpec((None, nsub, LANES), lambda i: (i, 0, 0)),
                   pl.BlockSpec((None, LANES, nsub), lambda i: (i, 0, 0))],
        out_shape=[jax.ShapeDtypeStruct((b, nsub, LANES), BF16), jax.ShapeDtypeStruct((b, LANES, nsub), BF16)],
        compiler_params=_cparams("parallel"),
        name="nsa_compress_prompt",
    )(cmp_rows, cmp_rows, pe4, wa, wb)


NSA_TQ = 256
NSA_TK = 512


def _softmax_terms(s, mask, exp_fn):
    s = jnp.where(mask, s, -jnp.inf)
    m = jnp.max(s, axis=-1, keepdims=True)
    m = jnp.where(m == -jnp.inf, 0.0, m)
    e = exp_fn(s - m)
    den = jnp.sum(e, axis=-1, keepdims=True)
    return e, 1.0 / jnp.where(den > 0, den, 1.0)


def _softmax_rows(s, mask, exp_fn=jnp.exp):
    e, inv = _softmax_terms(s, mask, exp_fn)
    return e * inv


def _topk_mask_t(score_t, k):
    j_io = lax.broadcasted_iota(jnp.int32, score_t.shape, 0)
    nj = score_t.shape[0]
    work = score_t
    for _ in range(k):
        m = jnp.max(work, axis=0, keepdims=True)
        jmin = jnp.min(jnp.where(work == m, j_io, nj), axis=0, keepdims=True)
        work = jnp.where(j_io == jmin, -jnp.inf, work)
    return jnp.logical_and(work == -jnp.inf, score_t > -jnp.inf)


SWEEP_CHUNKS = 2


def _causal_sweep_t(score_fn, pv_fn, mask_fn, n_full, cols):
    nc = SWEEP_CHUNKS
    cc = cols // nc

    def update(t, masked, state):
        scs = [score_fn(t, c) for c in range(nc)]
        out = []
        for c, (m, l, acc) in enumerate(state):
            sc = scs[c]
            if masked:
                sc = jnp.where(mask_fn(t, c), sc, NEG_BIG)
            m_new = jnp.maximum(m, jnp.max(sc, axis=0, keepdims=True))
            alpha = jnp.exp2(m - m_new)
            pe = jnp.exp2(sc - m_new)
            out.append((m_new, alpha * l + jnp.sum(pe, axis=0, keepdims=True),
                        alpha * acc + pv_fn(t, pe.astype(BF16))))
        return out

    init = [(jnp.full((1, cc), NEG_BIG, F32), jnp.zeros((1, cc), F32), jnp.zeros((LANES, cc), F32))
            for _ in range(nc)]
    state = lax.fori_loop(0, n_full, lambda t, st: update(t, False, st), init)
    state = update(n_full, True, state)
    return (jnp.concatenate([s[1] for s in state], axis=1), jnp.concatenate([s[2] for s in state], axis=1))


def _softmax_terms_t(s, mask):
    s = jnp.where(mask, s, -jnp.inf)
    m = jnp.max(s, axis=0, keepdims=True)
    m = jnp.where(m == -jnp.inf, 0.0, m)
    e = jnp.exp2(s - m)
    den = jnp.sum(e, axis=0, keepdims=True)
    return e, 1.0 / jnp.where(den > 0, den, 1.0)


def _nsa_prompt_kernel(qt_ref, gt_ref, kcmp_ref, vcmp_ref, kslc_ref, vslc_ref, kwin_ref, vwin_ref, eall_ref,
                       selt_ref, o_ref):
    tq = NSA_TQ
    qs = pl.program_id(1) * tq
    rows = NSA_GROUP * tq
    qpos = qs + lax.broadcasted_iota(jnp.int32, (1, rows), 1) % tq
    ncmp = kcmp_ref.shape[0]
    gates_t = jax.nn.sigmoid(gt_ref[...])
    head_out = []
    for kvh in range(NSA_KV_HEADS):
        q4t = jnp.concatenate(
            [qt_ref[(NSA_GROUP * kvh + g) * LANES:(NSA_GROUP * kvh + g + 1) * LANES, :]
             for g in range(NSA_GROUP)], axis=1)
        s = _dot(kcmp_ref[...], q4t)
        wlen = NSA_WINDOW + tq
        ws = pl.multiple_of(jnp.maximum(qs - NSA_WINDOW, 0), tq)
        sw = _dot(kwin_ref[pl.ds(ws, wlen), :], q4t)
        cmp_end = lax.broadcasted_iota(jnp.int32, (ncmp, 1), 0) * CMP_STRIDE + (CMP_LEN - 1)
        e, inv = _softmax_terms_t(s, cmp_end <= qpos)
        p = e * inv
        o_cmp_t = _dot(vcmp_ref[...], p.astype(BF16))
        psum = p[:, 0:tq] + p[:, tq:2 * tq] + p[:, 2 * tq:3 * tq] + p[:, 3 * tq:4 * tq]
        hi, mid, lo = _split3(psum)
        selt = selt_ref[...]
        imp_t = _dot(selt, hi) + _dot(selt, mid) + _dot(selt, lo)
        j_io = lax.broadcasted_iota(jnp.int32, imp_t.shape, 0)
        cur = (qs + lax.broadcasted_iota(jnp.int32, imp_t.shape, 1)) // SLC_BLOCK
        forced = jnp.logical_or(j_io == 0, j_io == cur)
        score_t = jnp.where(j_io <= cur, jnp.where(forced, FORCED_BLOCK_SCORE, imp_t), -jnp.inf)
        bias_t = jnp.where(_topk_mask_t(score_t, N_SEL), 0.0, NEG_BIG).astype(BF16)
        qext_t = jnp.concatenate([q4t, jnp.concatenate([bias_t] * NSA_GROUP, axis=1)], axis=0)

        def tile(t):
            return pl.ds(t * NSA_TK if isinstance(t, int) else pl.multiple_of(t * NSA_TK, NSA_TK), NSA_TK)

        cc = rows // SWEEP_CHUNKS

        def sel_scores(t, c, qext_t=qext_t):
            return _dot(jnp.concatenate([kslc_ref[tile(t), :], eall_ref[tile(t), :]], axis=1),
                        qext_t[:, c * cc:(c + 1) * cc])

        def sel_values(t, p):
            return _dot(vslc_ref[:, tile(t)], p)

        def sel_mask(t, c):
            kpos = t * NSA_TK + lax.broadcasted_iota(jnp.int32, (NSA_TK, 1), 0)
            return kpos <= qs + (c * cc + lax.broadcasted_iota(jnp.int32, (1, cc), 1)) % tq

        l_sel, acc_sel = _causal_sweep_t(sel_scores, sel_values, sel_mask, qs // NSA_TK, rows)
        o_sel_t = acc_sel / l_sel
        dist = qpos - (ws + lax.broadcasted_iota(jnp.int32, (wlen, 1), 0))
        ew, inv_w = _softmax_terms_t(sw, jnp.logical_and(dist >= 0, dist < NSA_WINDOW))
        o_win_t = _dot(vwin_ref[:, pl.ds(ws, wlen)], ew.astype(BF16)) * inv_w
        for g in range(NSA_GROUP):
            h = NSA_GROUP * kvh + g
            cs = slice(g * tq, (g + 1) * tq)
            mixed = (gates_t[3 * h:3 * h + 1, :] * o_cmp_t[:, cs] + gates_t[3 * h + 1:3 * h + 2, :] * o_sel_t[:, cs]
                     + gates_t[3 * h + 2:3 * h + 3, :] * o_win_t[:, cs])
            head_out.append(mixed[kvh * HEAD_DIM:(kvh + 1) * HEAD_DIM, :])
    for pair in range(NSA_HEADS // 2):
        slab_t = jnp.concatenate([head_out[2 * pair], head_out[2 * pair + 1]], axis=0)
        o_ref[:, pair * LANES:(pair + 1) * LANES] = slab_t.T.astype(o_ref.dtype)


def nsa_attend_prompt(q_nsa_t, gate_t, kcmp16, vcmp_t, kslc16, vslc_t, kwin16, vwin_t, eall, selt):
    b, _, t = q_nsa_t.shape
    ncmp = kcmp16.shape[1]
    rows = lambda n: pl.BlockSpec((None, n, LANES), lambda i, j: (i, 0, 0))
    cols = lambda n: pl.BlockSpec((None, LANES, n), lambda i, j: (i, 0, 0))
    return pl.pallas_call(
        _nsa_prompt_kernel,
        grid=(b, t // NSA_TQ),
        in_specs=[pl.BlockSpec((None, NSA_HEADS * LANES, NSA_TQ), lambda i, j: (i, 0, j)),
                  pl.BlockSpec((None, LANES, NSA_TQ), lambda i, j: (i, 0, j)),
                  rows(ncmp), cols(ncmp), rows(t), cols(t), rows(t), cols(t),
                  _const_spec(eall.shape), _const_spec(selt.shape)],
        out_specs=pl.BlockSpec((None, NSA_TQ, NSA_HEADS * HEAD_DIM), lambda i, j: (i, j, 0)),
        out_shape=jax.ShapeDtypeStruct((b, t, NSA_HEADS * HEAD_DIM), BF16),
        compiler_params=_cparams("parallel", "parallel"),
        name="nsa_attend_prompt",
    )(q_nsa_t, gate_t, kcmp16, vcmp_t, kslc16, vslc_t, kwin16, vwin_t, eall, selt)


DIFF_TQ = 512
DIFF_TK = 512


def _diff_lambda(lam_ref, lam_init):
    lv = lam_ref[...]
    a = jnp.sum(lv[0:1] * lv[1:2], axis=-1, keepdims=True)
    b = jnp.sum(lv[2:3] * lv[3:4], axis=-1, keepdims=True)
    return jnp.exp(a) - jnp.exp(b) + lam_init


def _diff_prompt_kernel(qt_ref, k_ref, vt_ref, lam_ref, ng_ref, o_ref, *, lam_init):
    tq = DIFF_TQ
    qs = pl.program_id(2) * tq
    q2t = jnp.concatenate([qt_ref[0:LANES, :], qt_ref[LANES:2 * LANES, :]], axis=1)
    cc = 2 * tq // SWEEP_CHUNKS

    def tile(t):
        return pl.ds(t * DIFF_TK if isinstance(t, int) else pl.multiple_of(t * DIFF_TK, DIFF_TK), DIFF_TK)

    def scores(t, c):
        return _dot(k_ref[tile(t), :], q2t[:, c * cc:(c + 1) * cc])

    def values(t, p):
        return _dot(vt_ref[:, tile(t)], p)

    def mask(t, c):
        kpos = t * DIFF_TK + lax.broadcasted_iota(jnp.int32, (DIFF_TK, 1), 0)
        qpos = qs + (c * cc + lax.broadcasted_iota(jnp.int32, (1, cc), 1)) % tq
        return kpos <= qpos

    l, acc = _causal_sweep_t(scores, values, mask, qs // DIFF_TK, 2 * tq)
    ot = acc / l
    lam = _diff_lambda(lam_ref, lam_init)
    ot = ot[:, 0:tq] - lam * ot[:, tq:2 * tq]
    o = ot.T
    o_ref[...] = (_rms(o, ng_ref[...], DIFF_NORM_EPS) * (1.0 - lam_init)).astype(o_ref.dtype)


def diff_attend_prompt(q_diff_t, k_diff16, v_diff_t, lam_vec, norm_g, lam_init):
    b, _, t = q_diff_t.shape
    return pl.pallas_call(
        functools.partial(_diff_prompt_kernel, lam_init=lam_init),
        grid=(b, DIFF_HEADS, t // DIFF_TQ),
        in_specs=[pl.BlockSpec((None, 2 * LANES, DIFF_TQ), lambda i, h, j: (i, h, j)),
                  pl.BlockSpec((None, t, LANES), lambda i, h, j: (i, 0, h)),
                  pl.BlockSpec((None, LANES, t), lambda i, h, j: (i, h, 0)),
                  _const_spec(lam_vec.shape), _const_spec((1, LANES))],
        out_specs=pl.BlockSpec((None, DIFF_TQ, LANES), lambda i, h, j: (i, j, h)),
        out_shape=jax.ShapeDtypeStruct((b, t, DIFF_HEADS * LANES), BF16),
        compiler_params=_cparams("parallel", "parallel", "parallel"),
        name="diff_attend_prompt",
    )(q_diff_t, k_diff16, v_diff_t, lam_vec, norm_g.reshape(1, LANES))


def _dil_prompt_kernel(q_ref, kvp_ref, kvc_ref, o_ref, lse_ref):
    band = DIL_BAND
    hw = DIL_HEADS * HEAD_DIM
    first = pl.program_id(2) == 0
    qi = lax.broadcasted_iota(jnp.int32, (band, 2 * band), 0) + band
    kj = lax.broadcasted_iota(jnp.int32, (band, 2 * band), 1)
    rel = qi - kj
    ok = jnp.logical_and(rel >= 0, rel <= band)
    ok = jnp.logical_and(ok, jnp.logical_not(jnp.logical_and(first, kj < band)))
    lane = lax.broadcasted_iota(jnp.int32, (band, LANES), 1)
    scores = []
    for h in range(DIL_HEADS):
        cs = slice((h // 2) * LANES, (h // 2 + 1) * LANES)
        k2 = jnp.concatenate([kvp_ref[:, cs], kvc_ref[:, cs]], axis=0)
        scores.append(_dot_nt(q_ref[:, h * LANES:(h + 1) * LANES], k2))
    for pair in range(DIL_HEADS // 2):
        cs = slice(pair * LANES, (pair + 1) * LANES)
        vs = slice(hw + pair * LANES, hw + (pair + 1) * LANES)
        v2 = jnp.concatenate([kvp_ref[:, vs], kvc_ref[:, vs]], axis=0)
        outs, lses = [], []
        for hh in range(2):
            s = jnp.where(ok, scores[2 * pair + hh], -jnp.inf)
            m = jnp.max(s, axis=-1, keepdims=True)
            e = jnp.exp2(s - m)
            den = jnp.sum(e, axis=-1, keepdims=True)
            outs.append(_dot(e.astype(BF16), v2) * (1.0 / den))
            lses.append(jnp.log(den) + m * LN2)
        o_ref[:, cs] = jnp.where(lane < HEAD_DIM, outs[0], outs[1])
        lse_ref[:, cs] = jnp.where(lane < HEAD_DIM, lses[0], lses[1])


def dil_attend_prompt(q_r, kv_r):
    b, dil, n, _ = q_r.shape
    hw = DIL_HEADS * HEAD_DIM
    blk = lambda w, prev: pl.BlockSpec(
        (None, None, DIL_BAND, w), (lambda i, r, u: (i, r, jnp.maximum(u - 1, 0), 0)) if prev
        else (lambda i, r, u: (i, r, u, 0)))
    return pl.pallas_call(
        _dil_prompt_kernel,
        grid=(b, dil, n // DIL_BAND),
        in_specs=[blk(DIL_HEADS * LANES, False), blk(2 * hw, True), blk(2 * hw, False)],
        out_specs=[blk(hw, False), blk(hw, False)],
        out_shape=[jax.ShapeDtypeStruct((b, dil, n, hw), F32)] * 2,
        compiler_params=_cparams("parallel", "parallel", "parallel"),
        name="dil_attend_prompt",
    )(q_r, kv_r, kv_r)


TRANSPOSE_UNROLL = 8


def _page_copies(cache_ref, pt_ref, buf_ref, sem_ref, bi, slot, n_pages):
    return [pltpu.make_async_copy(cache_ref.at[pt_ref[bi, j], pl.ds(0, 2 * LANES), :],
                                  buf_ref.at[slot, j], sem_ref.at[slot]) for j in range(n_pages)]


def _compress_sample_kernel(pt_ref, cache_ref, pe_ref, wa_ref, wb_ref, o_ref, page_ref, kbuf_ref, vbuf_ref,
                            sem_ref, *, n_pages):
    i = pl.program_id(0)
    slot = i % 2
    nsub = n_pages * PAGE_SIZE // CMP_STRIDE
    copies = functools.partial(_page_copies, cache_ref, pt_ref, page_ref, sem_ref, n_pages=n_pages)

    @pl.when(i == 0)
    def _():
        for cp in copies(0, 0):
            cp.start()

    @pl.when(i + 1 < pl.num_programs(0))
    def _():
        for cp in copies(i + 1, 1 - slot):
            cp.start()

    for cp in copies(i, slot):
        cp.wait()

    def to_rows(jj, carry):
        for u in range(TRANSPOSE_UNROLL):
            j = jj * TRANSPOSE_UNROLL + u
            r0 = pl.multiple_of(j * PAGE_SIZE, PAGE_SIZE)
            kbuf_ref[pl.ds(r0, PAGE_SIZE), :] = page_ref[slot, j, 0:LANES, :].T
            vbuf_ref[pl.ds(r0, PAGE_SIZE), :] = page_ref[slot, j, LANES:2 * LANES, :].T
        return carry

    lax.fori_loop(0, n_pages // TRANSPOSE_UNROLL, to_rows, 0)
    o_ref[...] = _compress_rows(kbuf_ref, vbuf_ref, pe_ref, wa_ref, wb_ref, nsub).astype(o_ref.dtype)


def nsa_compress_sample(cache_t, page_table, pe4, wa, wb):
    bs, n_pages = page_table.shape
    past = n_pages * PAGE_SIZE
    nsub = past // CMP_STRIDE
    grid_spec = pltpu.PrefetchScalarGridSpec(
        num_scalar_prefetch=1,
        grid=(bs,),
        in_specs=[pl.BlockSpec(memory_space=pl.ANY),
                  pl.BlockSpec(pe4.shape, lambda i, pt: (0, 0)),
                  pl.BlockSpec(wa.shape, lambda i, pt: (0, 0, 0)),
                  pl.BlockSpec(wb.shape, lambda i, pt: (0, 0, 0))],
        out_specs=pl.BlockSpec((None, nsub, 2 * LANES), lambda i, pt: (i, 0, 0)),
        scratch_shapes=[pltpu.VMEM((2, n_pages, 2 * LANES, PAGE_SIZE), F32),
                        pltpu.VMEM((past, LANES), F32), pltpu.VMEM((past, LANES), F32),
                        pltpu.SemaphoreType.DMA((2,))],
    )
    return pl.pallas_call(
        functools.partial(_compress_sample_kernel, n_pages=n_pages),
        grid_spec=grid_spec,
        out_shape=jax.ShapeDtypeStruct((bs, nsub, 2 * LANES), BF16),
        compiler_params=_cparams("arbitrary"),
        name="nsa_compress_sample",
    )(page_table, cache_t, pe4, wa, wb)


def _group_sum_rows(x):
    parts = [jnp.sum(x[NSA_GROUP * k:NSA_GROUP * (k + 1)], axis=0, keepdims=True) for k in range(NSA_KV_HEADS)]
    return _pad_rows(jnp.concatenate(parts, axis=0), x.shape[0])


def _nsa_sample_cmp_kernel(q_ref, cmp_ref, selt_ref, ocmp_ref, imp_ref, *, qpos):
    q8 = q_ref[...]
    ncmp = cmp_ref.shape[0]
    s = _dot_nt(q8, cmp_ref[:, 0:LANES])
    cmp_end = lax.broadcasted_iota(jnp.int32, (1, ncmp), 1) * CMP_STRIDE + (CMP_LEN - 1)
    p = _softmax_rows(s, cmp_end <= qpos)
    ocmp_ref[...] = _dot(p.astype(BF16), cmp_ref[:, LANES:2 * LANES])
    hi, mid, lo = _split3(_group_sum_rows(p))
    selt = selt_ref[...]
    imp_ref[...] = _dot_nt(hi, selt) + _dot_nt(mid, selt) + _dot_nt(lo, selt)


def nsa_sample_cmp(q8, cmp_kv, selt, qpos):
    bs = q8.shape[0]
    ncmp = cmp_kv.shape[1]
    blk = pl.BlockSpec((None, NSA_HEADS, LANES), lambda i: (i, 0, 0))
    return pl.pallas_call(
        functools.partial(_nsa_sample_cmp_kernel, qpos=qpos),
        grid=(bs,),
        in_specs=[blk, pl.BlockSpec((None, ncmp, 2 * LANES), lambda i: (i, 0, 0)), _const_spec(selt.shape)],
        out_specs=[blk, blk],
        out_shape=[jax.ShapeDtypeStruct((bs, NSA_HEADS, LANES), F32)] * 2,
        compiler_params=_cparams("parallel"),
        name="nsa_sample_cmp",
    )(q8, cmp_kv, selt)


def _topk_lanes_kernel(imp_ref, idx_ref, *, k):
    score = imp_ref[...]
    lane = lax.broadcasted_iota(jnp.int32, score.shape, 1)
    lane_f = lane.astype(F32)
    score = jnp.where(lane == 0, FORCED_BLOCK_SCORE, score)
    idx = jnp.zeros(score.shape, F32)
    for r in range(k):
        m = jnp.max(score, axis=-1, keepdims=True)
        jmin = jnp.min(jnp.where(score == m, lane_f, float(LANES)), axis=-1, keepdims=True)
        idx = jnp.where(lane == r, jmin, idx)
        score = jnp.where(lane_f == jmin, -jnp.inf, score)
    idx_ref[...] = idx.astype(jnp.int32)


def topk_lanes(imp, k):
    bs = imp.shape[0]
    x = imp.reshape(bs * NSA_HEADS, LANES)
    out = pl.pallas_call(
        functools.partial(_topk_lanes_kernel, k=k),
        out_shape=jax.ShapeDtypeStruct(x.shape, jnp.int32),
        name="topk_lanes",
    )(x)
    return out.reshape(bs, NSA_HEADS, LANES)


N_SEL_CACHE = N_SEL - 1


def _sel_copies(cache_ref, pt_ref, sel_ref, buf_ref, sem_ref, bi, slot):
    cps = []
    for kvh in range(NSA_KV_HEADS):
        for r in range(N_SEL_CACHE):
            j = sel_ref[bi, kvh * N_SEL_CACHE + r]
            cps.append(pltpu.make_async_copy(
                cache_ref.at[pt_ref[bi, j // 2], pl.ds(2 * LANES, 2 * LANES), :],
                buf_ref.at[slot, kvh, r], sem_ref.at[slot]))
    return cps


def _pick_gate(gates8, branch):
    row = lax.broadcasted_iota(jnp.int32, gates8.shape, 0)
    lane = lax.broadcasted_iota(jnp.int32, gates8.shape, 1)
    return jnp.sum(jnp.where(lane == 3 * row + branch, gates8, 0.0), axis=-1, keepdims=True)


def _nsa_sample_attend_kernel(pt_ref, sel_ref, q_ref, gate_ref, ocmp_ref, new_ref, win_ref, cache_ref,
                              o_ref, buf_ref, sem_ref):
    i = pl.program_id(0)
    slot = i % 2
    copies = functools.partial(_sel_copies, cache_ref, pt_ref, sel_ref, buf_ref, sem_ref)

    @pl.when(i == 0)
    def _():
        for cp in copies(0, 0):
            cp.start()

    @pl.when(i + 1 < pl.num_programs(0))
    def _():
        for cp in copies(i + 1, 1 - slot):
            cp.start()

    q8 = q_ref[...]
    q8f = q8.astype(F32)
    row = lax.broadcasted_iota(jnp.int32, (NSA_HEADS, 1), 0)
    new = new_ref[...]
    rnd = lambda x: x.astype(BF16).astype(F32)

    def probs_with_new_key(s, k_new, mask):
        s_new = jnp.sum(q8f * rnd(k_new), axis=-1, keepdims=True)
        s = jnp.where(mask, s, -jnp.inf)
        m = jnp.maximum(jnp.max(s, axis=-1, keepdims=True), s_new)
        e = jnp.exp(s - m)
        e_new = jnp.exp(s_new - m)
        den = jnp.sum(e, axis=-1, keepdims=True) + e_new
        return (e / den).astype(BF16), rnd(e_new / den)

    wb = win_ref.shape[1]
    widx = lax.broadcasted_iota(jnp.int32, (1, wb), 1)
    pw, pw_new = probs_with_new_key(_dot(q8, win_ref[0:LANES, :].astype(BF16)), new[:, 4 * LANES:5 * LANES],
                                    widx > wb - NSA_WINDOW)
    o_win = _dot_nt(pw, win_ref[LANES:2 * LANES, :].astype(BF16)) + pw_new * rnd(new[:, 5 * LANES:6 * LANES])
    for cp in copies(i, slot):
        cp.wait()
    lane = lax.broadcasted_iota(jnp.int32, (1, PAGE_SIZE), 1)
    o_sel = []
    for kvh in range(NSA_KV_HEADS):
        ss, masks = [], []
        for r in range(N_SEL_CACHE):
            ss.append(_dot(q8, buf_ref[slot, kvh, r, 0:LANES, :].astype(BF16)))
            masks.append(lane // SLC_BLOCK == sel_ref[i, kvh * N_SEL_CACHE + r] % 2)
        p, p_new = probs_with_new_key(jnp.concatenate(ss, axis=1), new[:, 2 * LANES:3 * LANES],
                                      jnp.concatenate(masks, axis=1))
        o = p_new * rnd(new[:, 3 * LANES:4 * LANES])
        for r in range(N_SEL_CACHE):
            o = o + _dot_nt(p[:, r * PAGE_SIZE:(r + 1) * PAGE_SIZE],
                            buf_ref[slot, kvh, r, LANES:2 * LANES, :].astype(BF16))
        o_sel.append(o)
    o_sel = jnp.where(row < NSA_GROUP, o_sel[0], o_sel[1])
    gates8 = jnp.broadcast_to(jax.nn.sigmoid(gate_ref[...]), (NSA_HEADS, LANES))
    o_ref[...] = (_pick_gate(gates8, 0) * ocmp_ref[...] + _pick_gate(gates8, 1) * o_sel
                  + _pick_gate(gates8, 2) * o_win)


def nsa_sample_attend(page_table, sel_idx, q8, gate, o_cmp, new_rows, win_t, cache_t):
    bs = q8.shape[0]
    wb = win_t.shape[2]
    blk = lambda w: pl.BlockSpec((None, NSA_HEADS, w), lambda i, pt, sel: (i, 0, 0))
    one = lambda w: pl.BlockSpec((None, 1, w), lambda i, pt, sel: (i, 0, 0))
    grid_spec = pltpu.PrefetchScalarGridSpec(
        num_scalar_prefetch=2,
        grid=(bs,),
        in_specs=[blk(LANES), one(LANES), blk(LANES), one(new_rows.shape[-1]),
                  pl.BlockSpec((None, 2 * LANES, wb), lambda i, pt, sel: (i, 0, 0)),
                  pl.BlockSpec(memory_space=pl.ANY)],
        out_specs=blk(LANES),
        scratch_shapes=[pltpu.VMEM((2, NSA_KV_HEADS, N_SEL_CACHE, 2 * LANES, PAGE_SIZE), F32),
                        pltpu.SemaphoreType.DMA((2,))],
    )
    return pl.pallas_call(
        _nsa_sample_attend_kernel,
        grid_spec=grid_spec,
        out_shape=jax.ShapeDtypeStruct((bs, NSA_HEADS, LANES), F32),
        compiler_params=_cparams("arbitrary"),
        name="nsa_sample_attend",
    )(page_table, sel_idx, q8, gate, o_cmp, new_rows, win_t, cache_t)


DIFF_PAGES_PER_STEP = 8
DIFF_ROW_STRIDE = 2 * DIFF_HEADS


def _diff_sample_kernel(pt_ref, qt_ref, new_ref, lam_ref, ng_ref, *rest, lam_init):
    pages = rest[:DIFF_PAGES_PER_STEP]
    o_ref, m_ref, l_ref, acc_ref = rest[DIFF_PAGES_PER_STEP:]
    c = pl.program_id(1)
    hw = DIFF_HEADS * LANES
    qt = qt_ref[...]
    rnd = lambda x: x.astype(BF16).astype(F32)

    def heads(page, which):
        return jnp.concatenate(
            [page[pl.ds(which * DIFF_HEADS + h, PAGE_SIZE, stride=DIFF_ROW_STRIDE), :] for h in range(DIFF_HEADS)],
            axis=1).astype(BF16)

    @pl.when(c == 0)
    def _():
        s_new = jnp.sum(qt.astype(F32) * rnd(new_ref[:, 0:hw]), axis=-1, keepdims=True)
        m_ref[...] = jnp.broadcast_to(s_new, m_ref.shape)
        l_ref[...] = jnp.ones(l_ref.shape, F32)
        acc_ref[...] = jnp.broadcast_to(rnd(new_ref[:, hw:2 * hw]), acc_ref.shape)

    s = jnp.concatenate([_dot_nt(qt, heads(page, 0)) for page in pages], axis=1)
    m_old = m_ref[:, 0:1]
    m_new = jnp.maximum(m_old, jnp.max(s, axis=-1, keepdims=True))
    alpha = jnp.exp(m_old - m_new)
    e = jnp.exp(s - m_new)
    l = l_ref[:, 0:1] * alpha + jnp.sum(e, axis=-1, keepdims=True)
    acc = acc_ref[...] * alpha
    for i, page in enumerate(pages):
        acc = acc + _dot(e[:, i * PAGE_SIZE:(i + 1) * PAGE_SIZE].astype(BF16), heads(page, 1))
    m_ref[...] = jnp.broadcast_to(m_new, m_ref.shape)
    l_ref[...] = jnp.broadcast_to(l, l_ref.shape)
    acc_ref[...] = acc

    @pl.when(c == pl.num_programs(1) - 1)
    def _():
        o = acc / l
        lam = _diff_lambda(lam_ref, lam_init)
        for h in range(DIFF_HEADS):
            cs = slice(h * LANES, (h + 1) * LANES)
            oh = o[h:h + 1, cs] - lam * o[DIFF_HEADS + h:DIFF_HEADS + h + 1, cs]
            o_ref[:, cs] = _rms(oh, ng_ref[...], DIFF_NORM_EPS) * (1.0 - lam_init)


def diff_attend_sample(page_table, qt, new_rows, lam_vec, norm_g, cache_v, lam_init):
    bs, n_pages = page_table.shape
    p = DIFF_PAGES_PER_STEP
    hw = DIFF_HEADS * LANES
    page_specs = [pl.BlockSpec((None, PAGE_SIZE * DIFF_ROW_STRIDE, LANES),
                               functools.partial(lambda i, c, pt, k: (pt[i, c * p + k], 0, 0), k=k))
                  for k in range(p)]
    grid_spec = pltpu.PrefetchScalarGridSpec(
        num_scalar_prefetch=1,
        grid=(bs, n_pages // p),
        in_specs=[pl.BlockSpec((None, 2 * DIFF_HEADS, hw), lambda i, c, pt: (i, 0, 0)),
                  pl.BlockSpec((None, 1, 2 * hw), lambda i, c, pt: (i, 0, 0)),
                  pl.BlockSpec(lam_vec.shape, lambda i, c, pt: (0, 0)),
                  pl.BlockSpec((1, LANES), lambda i, c, pt: (0, 0))] + page_specs,
        out_specs=pl.BlockSpec((None, 1, hw), lambda i, c, pt: (i, 0, 0)),
        scratch_shapes=[pltpu.VMEM((2 * DIFF_HEADS, LANES), F32), pltpu.VMEM((2 * DIFF_HEADS, LANES), F32),
                        pltpu.VMEM((2 * DIFF_HEADS, hw), F32)],
    )
    return pl.pallas_call(
        functools.partial(_diff_sample_kernel, lam_init=lam_init),
        grid_spec=grid_spec,
        out_shape=jax.ShapeDtypeStruct((bs, 1, hw), F32),
        compiler_params=_cparams("parallel", "arbitrary"),
        name="diff_attend_sample",
    )(page_table, qt, new_rows, lam_vec, norm_g.reshape(1, LANES), *([cache_v] * p))


def _col_rep(row):
    x = jnp.broadcast_to(row, (LANES, row.shape[1]))
    return jnp.concatenate([x[:, c * LANES:(c + 1) * LANES].T for c in range(row.shape[1] // LANES)], axis=0)


def _head_sum(x):
    return jnp.sum(x.reshape(DIL_HEADS, HEAD_DIM, x.shape[1]), axis=1)


def _head_expand(x):
    return jnp.broadcast_to(x[:, None, :], (DIL_HEADS, HEAD_DIM, x.shape[1])).reshape(
        DIL_HEADS * HEAD_DIM, x.shape[1])


def _dil_sample_kernel(q_ref, new0_ref, new1_ref, new2_ref, st0_ref, st1_ref, st2_ref, o_ref):
    hw = DIL_HEADS * HEAD_DIM
    outs, lses = [], []
    for g, (new_ref, st_ref) in enumerate(((new0_ref, st0_ref), (new1_ref, st1_ref), (new2_ref, st2_ref))):
        win, dil = DIL_GROUPS[g]
        qc = _col_rep(q_ref[:, g * hw:(g + 1) * hw])
        kn = _col_rep(new_ref[:, 0:hw])
        vn = _col_rep(new_ref[:, hw:2 * hw])
        s_new = _head_sum(qc * kn)[:, 0:1]
        n_chunks = win // LANES
        s = jnp.concatenate([_head_sum(st_ref[0:hw, c * LANES:(c + 1) * LANES] * qc) for c in range(n_chunks)],
                            axis=1)
        lane = lax.broadcasted_iota(jnp.int32, s.shape, 1)
        s = jnp.where(lane % dil == 0, s, -jnp.inf)
        m = jnp.maximum(jnp.max(s, axis=-1, keepdims=True), s_new)
        e = jnp.exp(s - m)
        e_new = jnp.exp(s_new - m)
        den = jnp.sum(e, axis=-1, keepdims=True) + e_new
        p = e / den
        acc = _head_expand(jnp.broadcast_to(e_new / den, (DIL_HEADS, LANES))) * vn * (1.0 / LANES)
        for c in range(n_chunks):
            cs = slice(c * LANES, (c + 1) * LANES)
            acc = acc + st_ref[hw:2 * hw, cs] * _head_expand(p[:, cs])
        outs.append(jnp.sum(acc, axis=-1, keepdims=True))
        lses.append(jnp.log(den) + m)
    mx = jnp.maximum(jnp.maximum(lses[0], lses[1]), lses[2])
    es = [jnp.exp(l - mx) for l in lses]
    tot = es[0] + es[1] + es[2]
    mix = jnp.zeros((hw, LANES), F32)
    for g in range(len(DIL_GROUPS)):
        alpha = _head_expand(jnp.broadcast_to(es[g] / tot, (DIL_HEADS, LANES)))
        mix = mix + alpha * jnp.broadcast_to(outs[g], (hw, LANES))
    rows = jnp.concatenate([mix[c * LANES:(c + 1) * LANES, :].T for c in range(hw // LANES)], axis=1)
    o_ref[...] = rows[0:1]


def dil_attend_sample(q, news, states_t):
    bs = q.shape[0]
    hw = DIL_HEADS * HEAD_DIM
    st_specs = [pl.BlockSpec((None, 2 * hw, st.shape[2]), lambda i: (i, 0, 0)) for st in states_t]
    return pl.pallas_call(
        _dil_sample_kernel,
        grid=(bs,),
        in_specs=[pl.BlockSpec((None, 1, q.shape[2]), lambda i: (i, 0, 0))]
                 + [pl.BlockSpec((None, 1, 2 * hw), lambda i: (i, 0, 0))] * 3 + st_specs,
        out_specs=pl.BlockSpec((None, 1, hw), lambda i: (i, 0, 0)),
        out_shape=jax.ShapeDtypeStruct((bs, 1, hw), F32),
        compiler_params=_cparams("parallel"),
        name="dil_attend_sample",
    )(q, *news, *states_t)


AB_SIZES = (NSA_HEADS * HEAD_DIM, 6 * NSA_KV_HEADS * HEAD_DIM, 3 * NSA_HEADS,
            DIFF_HEADS * 2 * HEAD_DIM, DIFF_HEADS * 2 * HEAD_DIM, DIFF_HEADS * 2 * HEAD_DIM)
_QSCALE = HEAD_DIM ** -0.5
_QSCALE2 = _QSCALE * LOG2E


def _rope_tables(pos):
    half = HEAD_DIM // 2
    inv = ROPE_THETA ** (-jnp.arange(half, dtype=F32) / half)
    ang = pos.astype(F32)[:, None] * inv[None, :]
    c, s = jnp.cos(ang), jnp.sin(ang)
    return jnp.tile(c, (1, 4)), jnp.tile(jnp.concatenate([-s, s], axis=1), (1, 2))


def _prep_w_ab(w):
    d = w.shape[0]
    qa, kvb, gl, qd, kd, vd = jnp.split(w, np.cumsum(AB_SIZES)[:-1].tolist(), axis=1)
    kvb = kvb.reshape(d, 6, LANES)
    k3 = kvb[:, 0::2].reshape(d, 3 * LANES)
    v3 = kvb[:, 1::2].reshape(d, 3 * LANES)
    gl = jnp.pad(gl, ((0, 0), (0, LANES - gl.shape[1])))
    return jnp.concatenate([qa, k3, v3, qd, kd, vd, gl], axis=1).astype(BF16)


def _ab_plan(dest, qscale):
    return (
        (0, 4, True, qscale, "half", tuple(dest("qa", j) for j in range(8))),
        (512, 3, True, 1.0, None, (dest("k_cmp", 0), dest("k_slc", 0), dest("k_win", 0))),
        (896, 3, False, 1.0, None, (dest("v_cmp", 0), dest("v_slc", 0), dest("v_win", 0))),
        (1280, 4, True, qscale, "pair", tuple(dest("qd", j) for j in range(8))),
        (1792, 4, True, 1.0, None, tuple(dest("kd", j) for j in range(4))),
        (2304, 4, False, 1.0, None, tuple(dest("vd", j) for j in range(4))),
        (2816, 1, False, 1.0, None, (dest("gate", 0),)),
    )


def _ab_prompt_defs(t):
    defs = (("T", 1024, BF16, t), ("T", 512, F32, t), ("N", 128, BF16), ("N", 256, F32),
            ("T", 256, F32, min(NSA_WINDOW, t)), ("N", 128, BF16), ("T", 1024, BF16, t), ("I", 8, F32),
            ("N", 512, BF16), ("T", 128, F32, t), ("T", 128, BF16, t), ("T", 128, BF16, t), ("T", 512, BF16, t))
    table = {
        "qa": lambda j: ((0, j),), "qd": lambda j: ((6, j),), "gate": lambda j: ((9, 0),),
        "k_cmp": lambda j: ((1, 0), (3, 0)), "v_cmp": lambda j: ((1, 1), (3, 1)),
        "k_slc": lambda j: ((1, 2), (2, 0)), "v_slc": lambda j: ((1, 3), (11, 0)),
        "k_win": lambda j: ((4, 0), (5, 0)), "v_win": lambda j: ((4, 1), (10, 0)),
        "kd": lambda j: ((7, j), (8, j)), "vd": lambda j: ((7, 4 + j), (12, j)),
    }
    return defs, _ab_plan(lambda name, j: table[name](j), _QSCALE2)


def _ab_sample_defs():
    defs = (("N", 1024, BF16), ("N", 512, F32), ("N", 256, F32), ("N", 1024, BF16), ("N", 1024, F32),
            ("N", 128, F32))
    table = {
        "qa": lambda j: ((0, j),), "qd": lambda j: ((3, j),), "gate": lambda j: ((5, 0),),
        "k_cmp": lambda j: ((1, 0),), "v_cmp": lambda j: ((1, 1),),
        "k_slc": lambda j: ((1, 2),), "v_slc": lambda j: ((1, 3),),
        "k_win": lambda j: ((2, 0),), "v_win": lambda j: ((2, 1),),
        "kd": lambda j: ((4, j),), "vd": lambda j: ((4, 4 + j),),
    }
    return defs, _ab_plan(lambda name, j: table[name](j), _QSCALE)


def _c_prompt_defs(t):
    defs, plan = [], []
    for g, (win, dil) in enumerate(DIL_GROUPS):
        defs += [("R", 1024, BF16, dil), ("R", 1024, BF16, dil), ("T", 1024, F32, min(win, t))]
        plan += [
            (g * 1536, 4, True, _QSCALE2, "pair", tuple(((3 * g, j),) for j in range(8))),
            (g * 1536 + 512, 4, True, 1.0, None, tuple(((3 * g + 1, j), (3 * g + 2, j)) for j in range(4))),
            (g * 1536 + 1024, 4, False, 1.0, None,
             tuple(((3 * g + 1, 4 + j), (3 * g + 2, 4 + j)) for j in range(4))),
        ]
    return tuple(defs), tuple(plan)


def _c_sample_defs():
    defs = (("N", 1536, F32), ("N", 1024, F32), ("N", 1024, F32), ("N", 1024, F32))
    plan = []
    for g in range(len(DIL_GROUPS)):
        plan += [
            (g * 1536, 4, True, _QSCALE, None, tuple(((0, 4 * g + j),) for j in range(4))),
            (g * 1536 + 512, 4, True, 1.0, None, tuple(((1 + g, j),) for j in range(4))),
            (g * 1536 + 1024, 4, False, 1.0, None, tuple(((1 + g, 4 + j),) for j in range(4))),
        ]
    return defs, tuple(plan)


def _prep_cmp(w_cmp, pe_cmp):
    wk, wv = w_cmp[0], w_cmp[1]
    z = jnp.zeros_like(wk)
    w4 = jnp.concatenate([jnp.concatenate([wk, z, z, z], axis=-1), jnp.concatenate([z, wk, z, z], axis=-1),
                          jnp.concatenate([z, z, wv, z], axis=-1), jnp.concatenate([z, z, z, wv], axis=-1)],
                         axis=1).astype(BF16)
    pe4 = jnp.concatenate([pe_cmp[0], pe_cmp[0], pe_cmp[1], pe_cmp[1]], axis=-1)
    half = CMP_LEN // 2
    return pe4, w4[:half], w4[half:]


def _block_indicator(n_keys):
    return (jnp.arange(n_keys)[:, None] // SLC_BLOCK == jnp.arange(LANES)[None, :]).astype(BF16)


def _cmp_to_block(n_cmp):
    r = SLC_BLOCK // CMP_STRIDE
    return (jnp.arange(n_cmp)[None, :] // r == jnp.arange(LANES)[:, None]).astype(BF16)


def _rows_from_t(x_t, lead):
    b, _, r = x_t.shape
    nd = len(lead)
    return x_t.reshape((b,) + tuple(lead) + (HEAD_DIM, r)).transpose((0, nd + 2) + tuple(range(1, nd + 2)))


def _rows_to_t(x):
    b, r = x.shape[:2]
    nd = x.ndim
    return x.transpose((0,) + tuple(range(2, nd)) + (1,)).reshape(b, -1, r)


def ab_mix_prompt(h, b, t, g_in, w_ab, cmp_prep, lam_vec, dn_g, lam_init, cos, sin):
    defs, plan = _ab_prompt_defs(t)
    (q_nsa_t, rows_nsa_t, kslc16, cmp_rows, rows_win_t, kwin16, q_diff_t, rows_diff, k_diff16, gate_t, vwin_t,
     vslc_t, v_diff_t) = project(h, b, g_in, cos, sin, w_ab, plan, defs)
    r3 = lambda x: x.reshape(b, t, x.shape[-1])
    pe4, wa, wb = cmp_prep
    kcmp16, vcmp_t = nsa_compress_prompt(r3(cmp_rows), pe4, wa, wb)
    o_nsa = nsa_attend_prompt(q_nsa_t, gate_t, kcmp16, vcmp_t, r3(kslc16), vslc_t, r3(kwin16), vwin_t,
                              _block_indicator(t), _cmp_to_block(t // CMP_STRIDE))
    o_diff = diff_attend_prompt(q_diff_t, r3(k_diff16), v_diff_t, lam_vec, dn_g, lam_init)
    mixed = [o_nsa.reshape(b * t, -1), o_diff.reshape(b * t, -1)]
    return mixed, rows_nsa_t, rows_win_t, rows_diff.reshape(b, t, 2, DIFF_HEADS, 2 * HEAD_DIM)


def dil_mix_prompt(h, b, t, g_in, w_c, cos, sin):
    defs, plan = _c_prompt_defs(t)
    outs = project(h, b, g_in, cos, sin, w_c, plan, defs)
    os_, lses, rows_t = [], [], []
    for gi in range(len(DIL_GROUPS)):
        o, lse = dil_attend_prompt(outs[3 * gi], outs[3 * gi + 1])
        os_.append(o)
        lses.append(lse)
        rows_t.append(outs[3 * gi + 2])
    return os_, lses, rows_t


def _diff_qt(q_diff):
    bs = q_diff.shape[0]
    qd = q_diff.reshape(bs, DIFF_HEADS, 2, LANES).transpose(0, 2, 1, 3)
    eye = jnp.eye(DIFF_HEADS, dtype=q_diff.dtype)
    return (qd[:, :, :, None, :] * eye[None, None, :, :, None]).reshape(bs, 2 * DIFF_HEADS, DIFF_HEADS * LANES)


def ab_mix_sample(hs, g_in, w_ab, cmp_prep, lam_vec, dn_g, lam_init, cos, sin,
                  cache_nsa, cache_diff, win_state, page_table):
    bs = hs.shape[0]
    defs, plan = _ab_sample_defs()
    q_nsa, rows_nsa, rows_win, q_diff, rows_diff, gate = project(hs, 1, g_in, cos, sin, w_ab, plan, defs)
    n_pages = page_table.shape[1]
    past = n_pages * PAGE_SIZE
    assert past // SLC_BLOCK == LANES, "selection-block axis is laid out on the 128 lanes"
    n_pool = cache_nsa.shape[0]
    cache_t = _rows_to_t(cache_nsa)
    cache_v = cache_diff.reshape(n_pool, PAGE_SIZE * DIFF_ROW_STRIDE, LANES)
    pe4, wa, wb = cmp_prep
    cmp_kv = nsa_compress_sample(cache_t, page_table, pe4, wa, wb)
    q8 = q_nsa.reshape(bs, NSA_HEADS, LANES)
    o_cmp, imp = nsa_sample_cmp(q8, cmp_kv, _cmp_to_block(past // CMP_STRIDE), past)
    sel_idx = topk_lanes(imp, N_SEL_CACHE)[:, :NSA_KV_HEADS, :N_SEL_CACHE].reshape(bs, -1)
    new_rows = jnp.concatenate([rows_nsa, rows_win], axis=-1).reshape(bs, 1, -1)
    o8 = nsa_sample_attend(page_table, sel_idx, q8, gate.reshape(bs, 1, LANES), o_cmp, new_rows,
                           _rows_to_t(win_state), cache_t)
    o8 = o8.reshape(bs, NSA_HEADS, 2, HEAD_DIM)
    o_nsa = jnp.concatenate([o8[:, :NSA_GROUP, 0], o8[:, NSA_GROUP:, 1]], axis=1).reshape(bs, -1)
    o_diff = diff_attend_sample(page_table, _diff_qt(q_diff), rows_diff.reshape(bs, 1, -1), lam_vec, dn_g,
                                cache_v, lam_init).reshape(bs, -1)
    mixed = jnp.concatenate([o_nsa, o_diff], axis=-1).astype(BF16)
    return mixed, rows_nsa, rows_win, rows_diff


def dil_mix_sample(hs, g_in, w_c_rows, cos, sin, states):
    bs = hs.shape[0]
    defs, plan = _c_sample_defs()
    outs = project(hs, 1, g_in, cos, sin, w_c_rows, plan, defs)
    news = [x.reshape(bs, 1, -1) for x in outs[1:]]
    sts = []
    for (win, dil), st in zip(DIL_GROUPS, states):
        assert st.shape[1] == win and win == DIL_BAND * dil, "state buffer must hold the full dilated window"
        sts.append(_rows_to_t(st))
    o = dil_attend_sample(outs[0].reshape(bs, 1, -1), news, sts).reshape(bs, -1)
    return o.astype(BF16), outs[1:]


def kernel(x_prompt, x_sample, cache_nsa, cache_diff, state_nsa_win, state_dil_0, state_dil_1, state_dil_2,
           page_table, norm_g, ffn_w_in, ffn_w_out, w_in_ab, w_out_ab, nsa_w_cmp, nsa_pe_cmp, diff_lambda,
           diff_norm_g, w_in_c, w_out_c):
    b, t, d = x_prompt.shape
    bs, ns, _ = x_sample.shape
    assert ns == 1, "sample group is one new token per sequence"
    depth = norm_g.shape[0]
    past = page_table.shape[1] * PAGE_SIZE
    hp = x_prompt.reshape(b * t, d)
    hs = x_sample.reshape(bs, d)
    cos_p, sin_p = _rope_tables(jnp.tile(jnp.arange(t, dtype=jnp.int32), b))
    cos_s, sin_s = _rope_tables(jnp.full((bs,), past, jnp.int32))
    w_ffn_in = ffn_w_in.astype(BF16)
    w_ffn_out = ffn_w_out.astype(BF16)
    state_dil = (state_dil_0, state_dil_1, state_dil_2)
    nsa_p, nsa_s, win_p, win_s, diff_p, diff_s = [], [], [], [], [], []
    dil_p = [[] for _ in DIL_GROUPS]
    dil_s = [[] for _ in DIL_GROUPS]
    for layer in range(depth):
        g = norm_g[layer]
        hp = ffn_half(hp, g[0], g[1], w_ffn_in[layer, 0], w_ffn_out[layer, 0])
        hs = ffn_half(hs, g[0], g[1], w_ffn_in[layer, 0], w_ffn_out[layer, 0])
        if layer % 2 == 0:
            e = layer // 2
            lam_init = 0.8 - 0.6 * math.exp(-0.3 * layer)
            w_ab = _prep_w_ab(w_in_ab[e])
            cmp_prep = _prep_cmp(nsa_w_cmp[e], nsa_pe_cmp[e])
            mp, rn_t, rw_t, rd = ab_mix_prompt(hp, b, t, g[2], w_ab, cmp_prep, diff_lambda[e], diff_norm_g[e],
                                               lam_init, cos_p, sin_p)
            nsa_p.append(_rows_from_t(rn_t, (4, NSA_KV_HEADS)))
            win_p.append(_rows_from_t(rw_t[:, :, -min(NSA_WINDOW, t):], (2, NSA_KV_HEADS)))
            diff_p.append(rd.reshape(b, t, 2, DIFF_HEADS, 2 * HEAD_DIM))
            ms, rn, rw, rd = ab_mix_sample(hs, g[2], w_ab, cmp_prep, diff_lambda[e], diff_norm_g[e], lam_init,
                                           cos_s, sin_s, cache_nsa[e], cache_diff[e], state_nsa_win[e], page_table)
            nsa_s.append(rn.reshape(bs, 1, 4, NSA_KV_HEADS, HEAD_DIM))
            win_full = jnp.concatenate([state_nsa_win[e], rw.reshape(bs, 1, 2, NSA_KV_HEADS, HEAD_DIM)], axis=1)
            win_s.append(win_full[:, -min(NSA_WINDOW, win_full.shape[1]):])
            diff_s.append(rd.reshape(bs, 1, 2, DIFF_HEADS, 2 * HEAD_DIM))
            w_o = w_out_ab[e].astype(BF16)
            hp = outproj(hp, mp, w_o, g[3])
            hs = outproj(hs, [ms], w_o, g[3])
        else:
            o = layer // 2
            w_o = w_out_c[o].astype(BF16)
            w_c = w_in_c[o].astype(BF16)
            os_, lses, rows_t = dil_mix_prompt(hp, b, t, g[2], w_c, cos_p, sin_p)
            hp = outproj_dil(hp, b, os_, lses, w_o, g[3])
            ms, news = dil_mix_sample(hs, g[2], w_c, cos_s, sin_s, [st[o] for st in state_dil])
            hs = outproj(hs, [ms], w_o, g[3])
            for gi, (win, dil) in enumerate(DIL_GROUPS):
                dil_p[gi].append(_rows_from_t(rows_t[gi][:, :, -min(win, t):], (2, DIL_HEADS)))
                full = jnp.concatenate([state_dil[gi][o], news[gi].reshape(bs, 1, 2, DIL_HEADS, HEAD_DIM)], axis=1)
                dil_s[gi].append(full[:, -min(win, full.shape[1]):])
        hp = ffn_half(hp, g[4], g[5], w_ffn_in[layer, 1], w_ffn_out[layer, 1])
        hs = ffn_half(hs, g[4], g[5], w_ffn_in[layer, 1], w_ffn_out[layer, 1])
    return (hp.reshape(b, t, d), hs.reshape(bs, 1, d), jnp.stack(nsa_p), jnp.stack(nsa_s), jnp.stack(win_p),
            jnp.stack(win_s), jnp.stack(diff_p), jnp.stack(diff_s), jnp.stack(dil_p[0]), jnp.stack(dil_s[0]),
            jnp.stack(dil_p[1]), jnp.stack(dil_s[1]), jnp.stack(dil_p[2]), jnp.stack(dil_s[2]))
```

```python
import functools
import math

import jax
import jax.numpy as jnp
import numpy as np
from jax import lax
from jax.experimental import pallas as pl
from jax.experimental.pallas import tpu as pltpu

F32 = jnp.float32
BF16 = jnp.bfloat16

LANES = 128
SUBLANES = 8
HEAD_DIM = 64
ROPE_THETA = 10000.0
NORM_EPS = 1e-6
PAGE_SIZE = 128
NSA_HEADS = 8
NSA_KV_HEADS = 2
NSA_GROUP = NSA_HEADS // NSA_KV_HEADS
CMP_LEN = 32
CMP_STRIDE = 16
SLC_BLOCK = 64
N_SEL = 16
NSA_WINDOW = 512
FORCED_BLOCK_SCORE = 1.0e4
DIFF_HEADS = 4
DIFF_NORM_EPS = 1e-5
DIL_GROUPS = ((128, 1), (512, 4), (2048, 16))
DIL_HEADS = 8
DIL_BAND = 128
NEG_BIG = -1.0e30
LN2 = math.log(2.0)
LOG2E = 1.0 / LN2
VMEM_LIMIT_BYTES = 56 * 1024 * 1024
TOKEN_TILE = 512


def _cparams(*sem):
    return pltpu.CompilerParams(dimension_semantics=sem, vmem_limit_bytes=VMEM_LIMIT_BYTES)


def _const_spec(shape):
    nd = len(shape)
    return pl.BlockSpec(shape, lambda *_: (0,) * nd, pipeline_mode=pl.Buffered(1))


def _rms(x, g, eps):
    return x * lax.rsqrt(jnp.mean(x * x, axis=-1, keepdims=True) + eps) * g


def _dot(a, b):
    return jnp.dot(a, b, preferred_element_type=F32)


def _dot_nt(a, b):
    return lax.dot_general(a, b, (((1,), (1,)), ((), ())), preferred_element_type=F32)


def _split3(x):
    hi = x.astype(BF16)
    r1 = x - hi.astype(F32)
    mid = r1.astype(BF16)
    lo = (r1 - mid.astype(F32)).astype(BF16)
    return hi, mid, lo


def _pad_rows(x, rows):
    return jnp.concatenate([x, jnp.zeros((rows - x.shape[0], x.shape[1]), x.dtype)], axis=0)


def _token_tile(n):
    return TOKEN_TILE if n % TOKEN_TILE == 0 else n


FFN_CHUNK = 256


def _ffn_kernel(x_ref, gpre_ref, gpost_ref, win_ref, wout_ref, o_ref, *, d_ff):
    x = x_ref[...]
    xn = _rms(x, gpre_ref[...], NORM_EPS).astype(BF16)
    acc = jnp.zeros(x.shape, F32)
    for c in range(d_ff // FFN_CHUNK):
        lo = c * FFN_CHUNK
        gate = _dot(xn, win_ref[:, lo:lo + FFN_CHUNK])
        up = _dot(xn, win_ref[:, d_ff + lo:d_ff + lo + FFN_CHUNK])
        act = (gate * jax.nn.sigmoid(gate) * up).astype(BF16)
        acc = acc + _dot(act, wout_ref[lo:lo + FFN_CHUNK, :])
    o_ref[...] = x + 0.5 * _rms(acc, gpost_ref[...], NORM_EPS)


def ffn_half(h, g_pre, g_post, w_in, w_out):
    n, d = h.shape
    d_ff = w_out.shape[0]
    tm = _token_tile(n)
    return pl.pallas_call(
        functools.partial(_ffn_kernel, d_ff=d_ff),
        grid=(n // tm,),
        in_specs=[pl.BlockSpec((tm, d), lambda i: (i, 0)),
                  _const_spec((1, d)), _const_spec((1, d)),
                  _const_spec(w_in.shape), _const_spec(w_out.shape)],
        out_specs=pl.BlockSpec((tm, d), lambda i: (i, 0)),
        out_shape=jax.ShapeDtypeStruct((n, d), F32),
        compiler_params=_cparams("parallel"),
        name="ffn_half",
    )(h, g_pre.reshape(1, d), g_post.reshape(1, d), w_in, w_out)


def _rope_slab(y, cos, sin):
    lane = lax.broadcasted_iota(jnp.int32, y.shape, 1)
    swapped = jnp.where(lane % HEAD_DIM < HEAD_DIM // 2,
                        pltpu.roll(y, LANES - HEAD_DIM // 2, 1),
                        pltpu.roll(y, HEAD_DIM // 2, 1))
    return y * cos + swapped * sin


def _proj_kernel(x_ref, g_ref, cos_ref, sin_ref, w_ref, *refs, plan, out_defs, first_tiles, tiles_per_b):
    out_refs = refs[:len(out_defs)]
    scr_ref = refs[len(out_defs)]
    tm = x_ref.shape[0]
    tile_in_b = pl.program_id(0) % tiles_per_b
    xn = _rms(x_ref[...], g_ref[...], NORM_EPS).astype(BF16)
    cos = cos_ref[...]
    sin = sin_ref[...]
    lane = lax.broadcasted_iota(jnp.int32, (tm, LANES), 1)

    def emit(val, val_t, out_idx, slab):
        ref = out_refs[out_idx]
        kind = out_defs[out_idx][0]
        cs = slice(slab * LANES, (slab + 1) * LANES)
        if kind == "N":
            ref[:, cs] = val().astype(ref.dtype)
        elif kind == "I":
            ref[pl.ds(slab, tm, stride=out_defs[out_idx][1]), :] = val().astype(ref.dtype)
        elif kind == "T":
            def write_t():
                ref[cs, :] = val_t().astype(ref.dtype)
            if first_tiles[out_idx] == 0:
                write_t()
            else:
                pl.when(tile_in_b >= first_tiles[out_idx])(write_t)
        else:
            dil = out_defs[out_idx][3]
            if dil == 1:
                ref[0, :, cs] = val().astype(ref.dtype)
            else:
                scr_ref[...] = val()
                for r in range(dil):
                    ref[r, :, cs] = scr_ref[pl.ds(r, tm // dil, stride=dil), :].astype(ref.dtype)

    for col0, nslab, rope, scale, pad, dests in plan:
        y = _dot(xn, w_ref[:, col0:col0 + nslab * LANES])
        for j in range(nslab):
            ys = y[:, j * LANES:(j + 1) * LANES]
            if rope:
                ys = _rope_slab(ys, cos, sin)
            if scale != 1.0:
                ys = ys * scale
            if pad is None:
                for out_idx, slab in dests[j]:
                    emit(lambda ys=ys: ys, lambda ys=ys: ys.T, out_idx, slab)
                continue
            transposed = []

            def ys_t(ys=ys, transposed=transposed):
                if not transposed:
                    transposed.append(ys.T)
                return transposed[0]

            for hh in range(2):
                head = 2 * j + hh
                at_hi = hh == 1 if pad == "pair" else head >= nslab

                def val(ys=ys, hh=hh, at_hi=at_hi):
                    v = ys if at_hi == (hh == 1) else pltpu.roll(ys, HEAD_DIM, 1)
                    return jnp.where(lane >= HEAD_DIM if at_hi else lane < HEAD_DIM, v, 0.0)

                def val_t(ys_t=ys_t, hh=hh, at_hi=at_hi):
                    rows = ys_t()[hh * HEAD_DIM:(hh + 1) * HEAD_DIM, :]
                    zero = jnp.zeros_like(rows)
                    return jnp.concatenate([zero, rows] if at_hi else [rows, zero], axis=0)

                for out_idx, slab in dests[head]:
                    emit(val, val_t, out_idx, slab)


def project(h, b, g, cos, sin, w, plan, out_defs):
    n, d = h.shape
    t = n // b
    tm = _token_tile(t)
    tpb = t // tm
    specs, shapes, first_tiles = [], [], []
    for od in out_defs:
        kind, c, dt = od[:3]
        if kind == "N":
            specs.append(pl.BlockSpec((tm, c), lambda i: (i, 0)))
            shapes.append(jax.ShapeDtypeStruct((n, c), dt))
            first_tiles.append(0)
        elif kind == "I":
            specs.append(pl.BlockSpec((tm * c, LANES), lambda i: (i, 0)))
            shapes.append(jax.ShapeDtypeStruct((n * c, LANES), dt))
            first_tiles.append(0)
        elif kind == "T":
            keep = max(min(od[3], t), tm)
            ft = (t - keep) // tm
            specs.append(pl.BlockSpec((None, c, tm),
                                      functools.partial(lambda i, ft: (i // tpb, 0, jnp.maximum(i % tpb - ft, 0)),
                                                        ft=ft)))
            shapes.append(jax.ShapeDtypeStruct((b, c, keep), dt))
            first_tiles.append(ft)
        else:
            dil = od[3]
            specs.append(pl.BlockSpec((None, dil, tm // dil, c), lambda i: (i // tpb, 0, i % tpb, 0)))
            shapes.append(jax.ShapeDtypeStruct((b, dil, t // dil, c), dt))
            first_tiles.append(0)
    return pl.pallas_call(
        functools.partial(_proj_kernel, plan=plan, out_defs=out_defs, first_tiles=tuple(first_tiles),
                          tiles_per_b=tpb),
        grid=(n // tm,),
        in_specs=[pl.BlockSpec((tm, d), lambda i: (i, 0)), _const_spec((1, d)),
                  pl.BlockSpec((tm, LANES), lambda i: (i, 0)),
                  pl.BlockSpec((tm, LANES), lambda i: (i, 0)),
                  _const_spec(w.shape)],
        out_specs=specs,
        out_shape=shapes,
        scratch_shapes=[pltpu.VMEM((tm, LANES), F32)],
        compiler_params=_cparams("arbitrary"),
        name="project",
    )(h, g.reshape(1, d), cos, sin, w)


def _outproj_kernel(h_ref, *refs):
    w_ref, g_ref, o_ref = refs[-3:]
    y, row0 = None, 0
    for m_ref in refs[:-3]:
        c = m_ref.shape[1]
        part = _dot(m_ref[...], w_ref[row0:row0 + c, :])
        y = part if y is None else y + part
        row0 += c
    o_ref[...] = h_ref[...] + _rms(y, g_ref[...], NORM_EPS)


def outproj(h, ms, w, g):
    n, d = h.shape
    tm = _token_tile(n)
    return pl.pallas_call(
        _outproj_kernel,
        grid=(n // tm,),
        in_specs=[pl.BlockSpec((tm, d), lambda i: (i, 0))]
                 + [pl.BlockSpec((tm, m.shape[1]), lambda i: (i, 0)) for m in ms]
                 + [_const_spec(w.shape), _const_spec((1, d))],
        out_specs=pl.BlockSpec((tm, d), lambda i: (i, 0)),
        out_shape=jax.ShapeDtypeStruct((n, d), F32),
        compiler_params=_cparams("parallel"),
        name="outproj",
    )(h, *ms, w, g.reshape(1, d))


def _outproj_dil_kernel(h_ref, *refs):
    ng = len(DIL_GROUPS)
    w_ref, g_ref, o_ref, scr_ref = refs[2 * ng:]
    tm = h_ref.shape[0]
    nslab = DIL_HEADS * HEAD_DIM // LANES
    vals = []
    k = 0
    for gi, (_, dil) in enumerate(DIL_GROUPS):
        per_g = []
        for ref in (refs[2 * gi], refs[2 * gi + 1]):
            slabs = []
            for s in range(nslab):
                cs = slice(s * LANES, (s + 1) * LANES)
                if dil == 1:
                    slabs.append(ref[0, :, cs])
                else:
                    for r in range(dil):
                        scr_ref[k, pl.ds(r, tm // dil, stride=dil), :] = ref[r, :, cs]
                    slabs.append(scr_ref[k])
                    k += 1
            per_g.append(slabs)
        vals.append(per_g)
    mixed = []
    for s in range(nslab):
        l0, l1, l2 = vals[0][1][s], vals[1][1][s], vals[2][1][s]
        mx = jnp.maximum(jnp.maximum(l0, l1), l2)
        e0, e1, e2 = jnp.exp(l0 - mx), jnp.exp(l1 - mx), jnp.exp(l2 - mx)
        den = e0 + e1 + e2
        mixed.append(((e0 / den) * vals[0][0][s] + (e1 / den) * vals[1][0][s]
                      + (e2 / den) * vals[2][0][s]).astype(BF16))
    y = _dot(jnp.concatenate(mixed, axis=1), w_ref[...])
    o_ref[...] = h_ref[...] + _rms(y, g_ref[...], NORM_EPS)


def outproj_dil(h, b, outs, lses, w, g):
    n, d = h.shape
    t = n // b
    tm = _token_tile(t)
    tpb = t // tm
    c = DIL_HEADS * HEAD_DIM
    specs, args, n_scr = [], [], 0
    for (_, dil), o, l in zip(DIL_GROUPS, outs, lses):
        spec = pl.BlockSpec((None, dil, tm // dil, c), lambda i: (i // tpb, 0, i % tpb, 0))
        specs += [spec, spec]
        args += [o, l]
        if dil > 1:
            n_scr += 2 * (c // LANES)
    return pl.pallas_call(
        _outproj_dil_kernel,
        grid=(n // tm,),
        in_specs=[pl.BlockSpec((tm, d), lambda i: (i, 0))] + specs + [_const_spec(w.shape), _const_spec((1, d))],
        out_specs=pl.BlockSpec((tm, d), lambda i: (i, 0)),
        out_shape=jax.ShapeDtypeStruct((n, d), F32),
        scratch_shapes=[pltpu.VMEM((n_scr, tm, LANES), F32)],
        compiler_params=_cparams("parallel"),
        name="outproj_dil",
    )(h, *args, w, g.reshape(1, d))


def _compress_rows(k_ref, v_ref, pe_ref, wa_ref, wb_ref, nsub):
    half = CMP_LEN // 2
    acc_a = jnp.zeros((nsub, 2 * LANES), F32)
    acc_b = jnp.zeros((nsub, 2 * LANES), F32)
    for l in range(half):
        x = jnp.concatenate([k_ref[pl.ds(l, nsub, stride=CMP_STRIDE), :],
                             v_ref[pl.ds(l, nsub, stride=CMP_STRIDE), :]], axis=1)
        acc_a = acc_a + _dot((x + pe_ref[l:l + 1, :]).astype(BF16), wa_ref[l])
        acc_b = acc_b + _dot((x + pe_ref[half + l:half + l + 1, :]).astype(BF16), wb_ref[l])
    comp = acc_a + pltpu.roll(acc_b, nsub - 1, 0)
    row = lax.broadcasted_iota(jnp.int32, comp.shape, 0)
    return jnp.where(row < nsub - 1, comp, 0.0)


def _compress_kernel(k_ref, v_ref, pe_ref, wa_ref, wb_ref, k_out_ref, vt_out_ref, *, nsub):
    comp = _compress_rows(k_ref, v_ref, pe_ref, wa_ref, wb_ref, nsub)
    k_out_ref[...] = comp[:, 0:LANES].astype(k_out_ref.dtype)
    vt_out_ref[...] = comp[:, LANES:2 * LANES].T.astype(vt_out_ref.dtype)


def nsa_compress_prompt(cmp_rows, pe4, wa, wb):
    b, t, _ = cmp_rows.shape
    nsub = t // CMP_STRIDE
    return pl.pallas_call(
        functools.partial(_compress_kernel, nsub=nsub),
        grid=(b,),
        in_specs=[pl.BlockSpec((None, t, LANES), lambda i: (i, 0, 0)),
                  pl.BlockSpec((None, t, LANES), lambda i: (i, 0, 1)),
                  _const_spec(pe4.shape), _const_spec(wa.shape), _const_spec(wb.shape)],
        out_specs=[pl.BlockSpec((None, nsub, LANES), lambda i: (i, 0, 0)),
                   pl.BlockSpec((None, LANES, nsub), lambda i: (i, 0, 0))],
        out_shape=[jax.ShapeDtypeStruct((b, nsub, LANES), BF16), jax.ShapeDtypeStruct((b, LANES, nsub), BF16)],
        compiler_params=_cparams("parallel"),
        name="nsa_compress_prompt",
    )(cmp_rows, cmp_rows, pe4, wa, wb)


NSA_TQ = 256
NSA_TK = 512


def _softmax_terms(s, mask, exp_fn):
    s = jnp.where(mask, s, -jnp.inf)
    m = jnp.max(s, axis=-1, keepdims=True)
    m = jnp.where(m == -jnp.inf, 0.0, m)
    e = exp_fn(s - m)
    den = jnp.sum(e, axis=-1, keepdims=True)
    return e, 1.0 / jnp.where(den > 0, den, 1.0)


def _softmax_rows(s, mask, exp_fn=jnp.exp):
    e, inv = _softmax_terms(s, mask, exp_fn)
    return e * inv


def _topk_mask_t(score_t, k):
    j_io = lax.broadcasted_iota(jnp.int32, score_t.shape, 0)
    nj = score_t.shape[0]
    work = score_t
    for _ in range(k):
        m = jnp.max(work, axis=0, keepdims=True)
        jmin = jnp.min(jnp.where(work == m, j_io, nj), axis=0, keepdims=True)
        work = jnp.where(j_io == jmin, -jnp.inf, work)
    return jnp.logical_and(work == -jnp.inf, score_t > -jnp.inf)


SWEEP_CHUNKS = 2


def _causal_sweep_t(score_fn, pv_fn, mask_fn, n_full, cols):
    nc = SWEEP_CHUNKS
    cc = cols // nc

    def update(tiles, masked, state):
        scs = [[score_fn(t, c) for t in tiles] for c in range(nc)]
        out = []
        for c, (m, l, acc) in enumerate(state):
            sc = scs[c]
            if masked:
                sc = [jnp.where(mask_fn(t, c), s, NEG_BIG) for t, s in zip(tiles, sc)]
            m_new = m
            for s in sc:
                m_new = jnp.maximum(m_new, jnp.max(s, axis=0, keepdims=True))
            alpha = jnp.exp2(m - m_new)
            l = alpha * l
            acc = alpha * acc
            for t, s in zip(tiles, sc):
                pe = jnp.exp2(s - m_new)
                l = l + jnp.sum(pe, axis=0, keepdims=True)
                acc = acc + pv_fn(t, pe.astype(BF16))
            out.append((m_new, l, acc))
        return out

    init = [(jnp.full((1, cc), NEG_BIG, F32), jnp.zeros((1, cc), F32), jnp.zeros((LANES, cc), F32))
            for _ in range(nc)]
    state = lax.fori_loop(0, n_full // 4, lambda u, st: update([4 * u + k for k in range(4)], False, st), init)
    base2 = (n_full // 4) * 4
    state = lax.fori_loop(0, (n_full % 4) // 2, lambda _, st: update([base2, base2 + 1], False, st), state)
    state = lax.fori_loop(0, n_full % 2, lambda _, st: update([n_full - 1], False, st), state)
    state = update([n_full], True, state)
    return (jnp.concatenate([s[1] for s in state], axis=1), jnp.concatenate([s[2] for s in state], axis=1))


def _softmax_terms_t(s, mask):
    s = jnp.where(mask, s, -jnp.inf)
    m = jnp.max(s, axis=0, keepdims=True)
    m = jnp.where(m == -jnp.inf, 0.0, m)
    e = jnp.exp2(s - m)
    den = jnp.sum(e, axis=0, keepdims=True)
    return e, 1.0 / jnp.where(den > 0, den, 1.0)


def _nsa_prompt_kernel(qt_ref, gt_ref, kcmp_ref, vcmp_ref, kslc_ref, vslc_ref, kwin_ref, vwin_ref, eall_ref,
                       selt_ref, o_ref):
    tq = NSA_TQ
    qs = pl.program_id(1) * tq
    rows = NSA_GROUP * tq
    qpos = qs + lax.broadcasted_iota(jnp.int32, (1, rows), 1) % tq
    ncmp = kcmp_ref.shape[0]
    gates_t = jax.nn.sigmoid(gt_ref[...])
    head_out = []
    for kvh in range(NSA_KV_HEADS):
        q4t = jnp.concatenate(
            [qt_ref[(NSA_GROUP * kvh + g) * LANES:(NSA_GROUP * kvh + g + 1) * LANES, :]
             for g in range(NSA_GROUP)], axis=1)
        s = _dot(kcmp_ref[...], q4t)
        wlen = NSA_WINDOW + tq
        ws = pl.multiple_of(jnp.maximum(qs - NSA_WINDOW, 0), tq)
        sw = _dot(kwin_ref[pl.ds(ws, wlen), :], q4t)
        cmp_end = lax.broadcasted_iota(jnp.int32, (ncmp, 1), 0) * CMP_STRIDE + (CMP_LEN - 1)
        e, inv = _softmax_terms_t(s, cmp_end <= qpos)
        p = e * inv
        o_cmp_t = _dot(vcmp_ref[...], p.astype(BF16))
        psum = p[:, 0:tq] + p[:, tq:2 * tq] + p[:, 2 * tq:3 * tq] + p[:, 3 * tq:4 * tq]
        hi, mid, lo = _split3(psum)
        selt = selt_ref[...]
        imp_t = _dot(selt, hi) + _dot(selt, mid) + _dot(selt, lo)
        j_io = lax.broadcasted_iota(jnp.int32, imp_t.shape, 0)
        cur = (qs + lax.broadcasted_iota(jnp.int32, imp_t.shape, 1)) // SLC_BLOCK
        forced = jnp.logical_or(j_io == 0, j_io == cur)
        score_t = jnp.where(j_io <= cur, jnp.where(forced, FORCED_BLOCK_SCORE, imp_t), -jnp.inf)
        bias_t = jnp.where(_topk_mask_t(score_t, N_SEL), 0.0, NEG_BIG).astype(BF16)
        qext_t = jnp.concatenate([q4t, jnp.concatenate([bias_t] * NSA_GROUP, axis=1)], axis=0)

        def tile(t):
            return pl.ds(t * NSA_TK if isinstance(t, int) else pl.multiple_of(t * NSA_TK, NSA_TK), NSA_TK)

        cc = rows // SWEEP_CHUNKS

        def sel_scores(t, c, qext_t=qext_t):
            return _dot(jnp.concatenate([kslc_ref[tile(t), :], eall_ref[tile(t), :]], axis=1),
                        qext_t[:, c * cc:(c + 1) * cc])

        def sel_values(t, p):
            return _dot(vslc_ref[:, tile(t)], p)

        def sel_mask(t, c):
            kpos = t * NSA_TK + lax.broadcasted_iota(jnp.int32, (NSA_TK, 1), 0)
            return kpos <= qs + (c * cc + lax.broadcasted_iota(jnp.int32, (1, cc), 1)) % tq

        l_sel, acc_sel = _causal_sweep_t(sel_scores, sel_values, sel_mask, qs // NSA_TK, rows)
        o_sel_t = acc_sel / l_sel
        dist = qpos - (ws + lax.broadcasted_iota(jnp.int32, (wlen, 1), 0))
        ew, inv_w = _softmax_terms_t(sw, jnp.logical_and(dist >= 0, dist < NSA_WINDOW))
        o_win_t = _dot(vwin_ref[:, pl.ds(ws, wlen)], ew.astype(BF16)) * inv_w
        for g in range(NSA_GROUP):
            h = NSA_GROUP * kvh + g
            cs = slice(g * tq, (g + 1) * tq)
            mixed = (gates_t[3 * h:3 * h + 1, :] * o_cmp_t[:, cs] + gates_t[3 * h + 1:3 * h + 2, :] * o_sel_t[:, cs]
                     + gates_t[3 * h + 2:3 * h + 3, :] * o_win_t[:, cs])
            head_out.append(mixed[kvh * HEAD_DIM:(kvh + 1) * HEAD_DIM, :])
    for pair in range(NSA_HEADS // 2):
        slab_t = jnp.concatenate([head_out[2 * pair], head_out[2 * pair + 1]], axis=0)
        o_ref[:, pair * LANES:(pair + 1) * LANES] = slab_t.T.astype(o_ref.dtype)


def nsa_attend_prompt(q_nsa_t, gate_t, kcmp16, vcmp_t, kslc16, vslc_t, kwin16, vwin_t, eall, selt):
    b, _, t = q_nsa_t.shape
    ncmp = kcmp16.shape[1]
    rows = lambda n: pl.BlockSpec((None, n, LANES), lambda i, j: (i, 0, 0))
    cols = lambda n: pl.BlockSpec((None, LANES, n), lambda i, j: (i, 0, 0))
    return pl.pallas_call(
        _nsa_prompt_kernel,
        grid=(b, t // NSA_TQ),
        in_specs=[pl.BlockSpec((None, NSA_HEADS * LANES, NSA_TQ), lambda i, j: (i, 0, j)),
                  pl.BlockSpec((None, LANES, NSA_TQ), lambda i, j: (i, 0, j)),
                  rows(ncmp), cols(ncmp), rows(t), cols(t), rows(t), cols(t),
                  _const_spec(eall.shape), _const_spec(selt.shape)],
        out_specs=pl.BlockSpec((None, NSA_TQ, NSA_HEADS * HEAD_DIM), lambda i, j: (i, j, 0)),
        out_shape=jax.ShapeDtypeStruct((b, t, NSA_HEADS * HEAD_DIM), BF16),
        compiler_params=_cparams("parallel", "parallel"),
        name="nsa_attend_prompt",
    )(q_nsa_t, gate_t, kcmp16, vcmp_t, kslc16, vslc_t, kwin16, vwin_t, eall, selt)


DIFF_TQ = 512
DIFF_TK = 512


def _diff_lambda(lam_ref, lam_init):
    lv = lam_ref[...]
    a = jnp.sum(lv[0:1] * lv[1:2], axis=-1, keepdims=True)
    b = jnp.sum(lv[2:3] * lv[3:4], axis=-1, keepdims=True)
    return jnp.exp(a) - jnp.exp(b) + lam_init


def _diff_prompt_kernel(qt_ref, k_ref, vt_ref, lam_ref, ng_ref, o_ref, *, lam_init):
    tq = DIFF_TQ
    qs = pl.program_id(2) * tq
    q2t = jnp.concatenate([qt_ref[0:LANES, :], qt_ref[LANES:2 * LANES, :]], axis=1)
    cc = 2 * tq // SWEEP_CHUNKS

    def tile(t):
        return pl.ds(t * DIFF_TK if isinstance(t, int) else pl.multiple_of(t * DIFF_TK, DIFF_TK), DIFF_TK)

    def scores(t, c):
        return _dot(k_ref[tile(t), :], q2t[:, c * cc:(c + 1) * cc])

    def values(t, p):
        return _dot(vt_ref[:, tile(t)], p)

    def mask(t, c):
        kpos = t * DIFF_TK + lax.broadcasted_iota(jnp.int32, (DIFF_TK, 1), 0)
        qpos = qs + (c * cc + lax.broadcasted_iota(jnp.int32, (1, cc), 1)) % tq
        return kpos <= qpos

    l, acc = _causal_sweep_t(scores, values, mask, qs // DIFF_TK, 2 * tq)
    ot = acc / l
    lam = _diff_lambda(lam_ref, lam_init)
    ot = ot[:, 0:tq] - lam * ot[:, tq:2 * tq]
    o = ot.T
    o_ref[...] = (_rms(o, ng_ref[...], DIFF_NORM_EPS) * (1.0 - lam_init)).astype(o_ref.dtype)


def diff_attend_prompt(q_diff_t, k_diff16, v_diff_t, lam_vec, norm_g, lam_init):
    b, _, t = q_diff_t.shape
    return pl.pallas_call(
        functools.partial(_diff_prompt_kernel, lam_init=lam_init),
        grid=(b, DIFF_HEADS, t // DIFF_TQ),
        in_specs=[pl.BlockSpec((None, 2 * LANES, DIFF_TQ), lambda i, h, j: (i, h, j)),
                  pl.BlockSpec((None, t, LANES), lambda i, h, j: (i, 0, h)),
                  pl.BlockSpec((None, LANES, t), lambda i, h, j: (i, h, 0)),
                  _const_spec(lam_vec.shape), _const_spec((1, LANES))],
        out_specs=pl.BlockSpec((None, DIFF_TQ, LANES), lambda i, h, j: (i, j, h)),
        out_shape=jax.ShapeDtypeStruct((b, t, DIFF_HEADS * LANES), BF16),
        compiler_params=_cparams("parallel", "parallel", "parallel"),
        name="diff_attend_prompt",
    )(q_diff_t, k_diff16, v_diff_t, lam_vec, norm_g.reshape(1, LANES))


def _dil_prompt_kernel(q_ref, kvp_ref, kvc_ref, o_ref, lse_ref):
    band = DIL_BAND
    hw = DIL_HEADS * HEAD_DIM
    first = pl.program_id(2) == 0
    qi = lax.broadcasted_iota(jnp.int32, (band, 2 * band), 0) + band
    kj = lax.broadcasted_iota(jnp.int32, (band, 2 * band), 1)
    rel = qi - kj
    ok = jnp.logical_and(rel >= 0, rel <= band)
    ok = jnp.logical_and(ok, jnp.logical_not(jnp.logical_and(first, kj < band)))
    lane = lax.broadcasted_iota(jnp.int32, (band, LANES), 1)
    scores = []
    for h in range(DIL_HEADS):
        cs = slice((h // 2) * LANES, (h // 2 + 1) * LANES)
        k2 = jnp.concatenate([kvp_ref[:, cs], kvc_ref[:, cs]], axis=0)
        scores.append(_dot_nt(q_ref[:, h * LANES:(h + 1) * LANES], k2))
    for pair in range(DIL_HEADS // 2):
        cs = slice(pair * LANES, (pair + 1) * LANES)
        vs = slice(hw + pair * LANES, hw + (pair + 1) * LANES)
        v2 = jnp.concatenate([kvp_ref[:, vs], kvc_ref[:, vs]], axis=0)
        outs, lses = [], []
        for hh in range(2):
            s = jnp.where(ok, scores[2 * pair + hh], -jnp.inf)
            m = jnp.max(s, axis=-1, keepdims=True)
            e = jnp.exp2(s - m)
            den = jnp.sum(e, axis=-1, keepdims=True)
            outs.append(_dot(e.astype(BF16), v2) * (1.0 / den))
            lses.append(jnp.log(den) + m * LN2)
        o_ref[:, cs] = jnp.where(lane < HEAD_DIM, outs[0], outs[1])
        lse_ref[:, cs] = jnp.where(lane < HEAD_DIM, lses[0], lses[1])


def dil_attend_prompt(q_r, kv_r):
    b, dil, n, _ = q_r.shape
    hw = DIL_HEADS * HEAD_DIM
    blk = lambda w, prev: pl.BlockSpec(
        (None, None, DIL_BAND, w), (lambda i, r, u: (i, r, jnp.maximum(u - 1, 0), 0)) if prev
        else (lambda i, r, u: (i, r, u, 0)))
    return pl.pallas_call(
        _dil_prompt_kernel,
        grid=(b, dil, n // DIL_BAND),
        in_specs=[blk(DIL_HEADS * LANES, False), blk(2 * hw, True), blk(2 * hw, False)],
        out_specs=[blk(hw, False), blk(hw, False)],
        out_shape=[jax.ShapeDtypeStruct((b, dil, n, hw), F32)] * 2,
        compiler_params=_cparams("parallel", "parallel", "parallel"),
        name="dil_attend_prompt",
    )(q_r, kv_r, kv_r)


TRANSPOSE_UNROLL = 8


def _page_copies(cache_ref, pt_ref, buf_ref, sem_ref, bi, slot, n_pages):
    return [pltpu.make_async_copy(cache_ref.at[pt_ref[bi, j], pl.ds(0, 2 * LANES), :],
                                  buf_ref.at[slot, j], sem_ref.at[slot]) for j in range(n_pages)]


def _compress_sample_kernel(pt_ref, cache_ref, pe_ref, wa_ref, wb_ref, o_ref, page_ref, kbuf_ref, vbuf_ref,
                            sem_ref, *, n_pages):
    i = pl.program_id(0)
    slot = i % 2
    nsub = n_pages * PAGE_SIZE // CMP_STRIDE
    copies = functools.partial(_page_copies, cache_ref, pt_ref, page_ref, sem_ref, n_pages=n_pages)

    @pl.when(i == 0)
    def _():
        for cp in copies(0, 0):
            cp.start()

    @pl.when(i + 1 < pl.num_programs(0))
    def _():
        for cp in copies(i + 1, 1 - slot):
            cp.start()

    for cp in copies(i, slot):
        cp.wait()

    def to_rows(jj, carry):
        for u in range(TRANSPOSE_UNROLL):
            j = jj * TRANSPOSE_UNROLL + u
            r0 = pl.multiple_of(j * PAGE_SIZE, PAGE_SIZE)
            kbuf_ref[pl.ds(r0, PAGE_SIZE), :] = page_ref[slot, j, 0:LANES, :].T
            vbuf_ref[pl.ds(r0, PAGE_SIZE), :] = page_ref[slot, j, LANES:2 * LANES, :].T
        return carry

    lax.fori_loop(0, n_pages // TRANSPOSE_UNROLL, to_rows, 0)
    o_ref[...] = _compress_rows(kbuf_ref, vbuf_ref, pe_ref, wa_ref, wb_ref, nsub).astype(o_ref.dtype)


def nsa_compress_sample(cache_t, page_table, pe4, wa, wb):
    bs, n_pages = page_table.shape
    past = n_pages * PAGE_SIZE
    nsub = past // CMP_STRIDE
    grid_spec = pltpu.PrefetchScalarGridSpec(
        num_scalar_prefetch=1,
        grid=(bs,),
        in_specs=[pl.BlockSpec(memory_space=pl.ANY),
                  pl.BlockSpec(pe4.shape, lambda i, pt: (0, 0)),
                  pl.BlockSpec(wa.shape, lambda i, pt: (0, 0, 0)),
                  pl.BlockSpec(wb.shape, lambda i, pt: (0, 0, 0))],
        out_specs=pl.BlockSpec((None, nsub, 2 * LANES), lambda i, pt: (i, 0, 0)),
        scratch_shapes=[pltpu.VMEM((2, n_pages, 2 * LANES, PAGE_SIZE), F32),
                        pltpu.VMEM((past, LANES), F32), pltpu.VMEM((past, LANES), F32),
                        pltpu.SemaphoreType.DMA((2,))],
    )
    return pl.pallas_call(
        functools.partial(_compress_sample_kernel, n_pages=n_pages),
        grid_spec=grid_spec,
        out_shape=jax.ShapeDtypeStruct((bs, nsub, 2 * LANES), BF16),
        compiler_params=_cparams("arbitrary"),
        name="nsa_compress_sample",
    )(page_table, cache_t, pe4, wa, wb)


def _group_sum_rows(x):
    parts = [jnp.sum(x[NSA_GROUP * k:NSA_GROUP * (k + 1)], axis=0, keepdims=True) for k in range(NSA_KV_HEADS)]
    return _pad_rows(jnp.concatenate(parts, axis=0), x.shape[0])


def _nsa_sample_cmp_kernel(q_ref, cmp_ref, selt_ref, ocmp_ref, imp_ref, *, qpos):
    q8 = q_ref[...]
    ncmp = cmp_ref.shape[0]
    s = _dot_nt(q8, cmp_ref[:, 0:LANES])
    cmp_end = lax.broadcasted_iota(jnp.int32, (1, ncmp), 1) * CMP_STRIDE + (CMP_LEN - 1)
    p = _softmax_rows(s, cmp_end <= qpos)
    ocmp_ref[...] = _dot(p.astype(BF16), cmp_ref[:, LANES:2 * LANES])
    hi, mid, lo = _split3(_group_sum_rows(p))
    selt = selt_ref[...]
    imp_ref[...] = _dot_nt(hi, selt) + _dot_nt(mid, selt) + _dot_nt(lo, selt)


def nsa_sample_cmp(q8, cmp_kv, selt, qpos):
    bs = q8.shape[0]
    ncmp = cmp_kv.shape[1]
    blk = pl.BlockSpec((None, NSA_HEADS, LANES), lambda i: (i, 0, 0))
    return pl.pallas_call(
        functools.partial(_nsa_sample_cmp_kernel, qpos=qpos),
        grid=(bs,),
        in_specs=[blk, pl.BlockSpec((None, ncmp, 2 * LANES), lambda i: (i, 0, 0)), _const_spec(selt.shape)],
        out_specs=[blk, blk],
        out_shape=[jax.ShapeDtypeStruct((bs, NSA_HEADS, LANES), F32)] * 2,
        compiler_params=_cparams("parallel"),
        name="nsa_sample_cmp",
    )(q8, cmp_kv, selt)


def _topk_lanes_kernel(imp_ref, idx_ref, *, k):
    score = imp_ref[...]
    lane = lax.broadcasted_iota(jnp.int32, score.shape, 1)
    lane_f = lane.astype(F32)
    score = jnp.where(lane == 0, FORCED_BLOCK_SCORE, score)
    idx = jnp.zeros(score.shape, F32)
    for r in range(k):
        m = jnp.max(score, axis=-1, keepdims=True)
        jmin = jnp.min(jnp.where(score == m, lane_f, float(LANES)), axis=-1, keepdims=True)
        idx = jnp.where(lane == r, jmin, idx)
        score = jnp.where(lane_f == jmin, -jnp.inf, score)
    idx_ref[...] = idx.astype(jnp.int32)


def topk_lanes(imp, k):
    bs = imp.shape[0]
    x = imp.reshape(bs * NSA_HEADS, LANES)
    out = pl.pallas_call(
        functools.partial(_topk_lanes_kernel, k=k),
        out_shape=jax.ShapeDtypeStruct(x.shape, jnp.int32),
        name="topk_lanes",
    )(x)
    return out.reshape(bs, NSA_HEADS, LANES)


N_SEL_CACHE = N_SEL - 1


def _sel_copies(cache_ref, pt_ref, sel_ref, buf_ref, sem_ref, bi, slot):
    cps = []
    for kvh in range(NSA_KV_HEADS):
        for r in range(N_SEL_CACHE):
            j = sel_ref[bi, kvh * N_SEL_CACHE + r]
            cps.append(pltpu.make_async_copy(
                cache_ref.at[pt_ref[bi, j // 2], pl.ds(2 * LANES, 2 * LANES), :],
                buf_ref.at[slot, kvh, r], sem_ref.at[slot]))
    return cps


def _pick_gate(gates8, branch):
    row = lax.broadcasted_iota(jnp.int32, gates8.shape, 0)
    lane = lax.broadcasted_iota(jnp.int32, gates8.shape, 1)
    return jnp.sum(jnp.where(lane == 3 * row + branch, gates8, 0.0), axis=-1, keepdims=True)


def _nsa_sample_attend_kernel(pt_ref, sel_ref, q_ref, gate_ref, ocmp_ref, new_ref, win_ref, cache_ref,
                              o_ref, buf_ref, sem_ref):
    i = pl.program_id(0)
    slot = i % 2
    copies = functools.partial(_sel_copies, cache_ref, pt_ref, sel_ref, buf_ref, sem_ref)

    @pl.when(i == 0)
    def _():
        for cp in copies(0, 0):
            cp.start()

    @pl.when(i + 1 < pl.num_programs(0))
    def _():
        for cp in copies(i + 1, 1 - slot):
            cp.start()

    q8 = q_ref[...]
    q8f = q8.astype(F32)
    row = lax.broadcasted_iota(jnp.int32, (NSA_HEADS, 1), 0)
    new = new_ref[...]
    rnd = lambda x: x.astype(BF16).astype(F32)

    def probs_with_new_key(s, k_new, mask):
        s_new = jnp.sum(q8f * rnd(k_new), axis=-1, keepdims=True)
        s = jnp.where(mask, s, -jnp.inf)
        m = jnp.maximum(jnp.max(s, axis=-1, keepdims=True), s_new)
        e = jnp.exp(s - m)
        e_new = jnp.exp(s_new - m)
        den = jnp.sum(e, axis=-1, keepdims=True) + e_new
        return (e / den).astype(BF16), rnd(e_new / den)

    wb = win_ref.shape[1]
    widx = lax.broadcasted_iota(jnp.int32, (1, wb), 1)
    pw, pw_new = probs_with_new_key(_dot(q8, win_ref[0:LANES, :].astype(BF16)), new[:, 4 * LANES:5 * LANES],
                                    widx > wb - NSA_WINDOW)
    o_win = _dot_nt(pw, win_ref[LANES:2 * LANES, :].astype(BF16)) + pw_new * rnd(new[:, 5 * LANES:6 * LANES])
    for cp in copies(i, slot):
        cp.wait()
    lane = lax.broadcasted_iota(jnp.int32, (1, PAGE_SIZE), 1)
    o_sel = []
    for kvh in range(NSA_KV_HEADS):
        ss, masks = [], []
        for r in range(N_SEL_CACHE):
            ss.append(_dot(q8, buf_ref[slot, kvh, r, 0:LANES, :].astype(BF16)))
            masks.append(lane // SLC_BLOCK == sel_ref[i, kvh * N_SEL_CACHE + r] % 2)
        p, p_new = probs_with_new_key(jnp.concatenate(ss, axis=1), new[:, 2 * LANES:3 * LANES],
                                      jnp.concatenate(masks, axis=1))
        o = p_new * rnd(new[:, 3 * LANES:4 * LANES])
        for r in range(N_SEL_CACHE):
            o = o + _dot_nt(p[:, r * PAGE_SIZE:(r + 1) * PAGE_SIZE],
                            buf_ref[slot, kvh, r, LANES:2 * LANES, :].astype(BF16))
        o_sel.append(o)
    o_sel = jnp.where(row < NSA_GROUP, o_sel[0], o_sel[1])
    gates8 = jnp.broadcast_to(jax.nn.sigmoid(gate_ref[...]), (NSA_HEADS, LANES))
    o_ref[...] = (_pick_gate(gates8, 0) * ocmp_ref[...] + _pick_gate(gates8, 1) * o_sel
                  + _pick_gate(gates8, 2) * o_win)


def nsa_sample_attend(page_table, sel_idx, q8, gate, o_cmp, new_rows, win_t, cache_t):
    bs = q8.shape[0]
    wb = win_t.shape[2]
    blk = lambda w: pl.BlockSpec((None, NSA_HEADS, w), lambda i, pt, sel: (i, 0, 0))
    one = lambda w: pl.BlockSpec((None, 1, w), lambda i, pt, sel: (i, 0, 0))
    grid_spec = pltpu.PrefetchScalarGridSpec(
        num_scalar_prefetch=2,
        grid=(bs,),
        in_specs=[blk(LANES), one(LANES), blk(LANES), one(new_rows.shape[-1]),
                  pl.BlockSpec((None, 2 * LANES, wb), lambda i, pt, sel: (i, 0, 0)),
                  pl.BlockSpec(memory_space=pl.ANY)],
        out_specs=blk(LANES),
        scratch_shapes=[pltpu.VMEM((2, NSA_KV_HEADS, N_SEL_CACHE, 2 * LANES, PAGE_SIZE), F32),
                        pltpu.SemaphoreType.DMA((2,))],
    )
    return pl.pallas_call(
        _nsa_sample_attend_kernel,
        grid_spec=grid_spec,
        out_shape=jax.ShapeDtypeStruct((bs, NSA_HEADS, LANES), F32),
        compiler_params=_cparams("arbitrary"),
        name="nsa_sample_attend",
    )(page_table, sel_idx, q8, gate, o_cmp, new_rows, win_t, cache_t)


DIFF_PAGES_PER_STEP = 8
DIFF_ROW_STRIDE = 2 * DIFF_HEADS


def _diff_sample_kernel(pt_ref, qt_ref, new_ref, lam_ref, ng_ref, *rest, lam_init):
    pages = rest[:DIFF_PAGES_PER_STEP]
    o_ref, m_ref, l_ref, acc_ref = rest[DIFF_PAGES_PER_STEP:]
    c = pl.program_id(1)
    hw = DIFF_HEADS * LANES
    qt = qt_ref[...]
    rnd = lambda x: x.astype(BF16).astype(F32)

    def heads(page, which):
        return jnp.concatenate(
            [page[pl.ds(which * DIFF_HEADS + h, PAGE_SIZE, stride=DIFF_ROW_STRIDE), :] for h in range(DIFF_HEADS)],
            axis=1).astype(BF16)

    @pl.when(c == 0)
    def _():
        s_new = jnp.sum(qt.astype(F32) * rnd(new_ref[:, 0:hw]), axis=-1, keepdims=True)
        m_ref[...] = jnp.broadcast_to(s_new, m_ref.shape)
        l_ref[...] = jnp.ones(l_ref.shape, F32)
        acc_ref[...] = jnp.broadcast_to(rnd(new_ref[:, hw:2 * hw]), acc_ref.shape)

    s = jnp.concatenate([_dot_nt(qt, heads(page, 0)) for page in pages], axis=1)
    m_old = m_ref[:, 0:1]
    m_new = jnp.maximum(m_old, jnp.max(s, axis=-1, keepdims=True))
    alpha = jnp.exp(m_old - m_new)
    e = jnp.exp(s - m_new)
    l = l_ref[:, 0:1] * alpha + jnp.sum(e, axis=-1, keepdims=True)
    acc = acc_ref[...] * alpha
    for i, page in enumerate(pages):
        acc = acc + _dot(e[:, i * PAGE_SIZE:(i + 1) * PAGE_SIZE].astype(BF16), heads(page, 1))
    m_ref[...] = jnp.broadcast_to(m_new, m_ref.shape)
    l_ref[...] = jnp.broadcast_to(l, l_ref.shape)
    acc_ref[...] = acc

    @pl.when(c == pl.num_programs(1) - 1)
    def _():
        o = acc / l
        lam = _diff_lambda(lam_ref, lam_init)
        for h in range(DIFF_HEADS):
            cs = slice(h * LANES, (h + 1) * LANES)
            oh = o[h:h + 1, cs] - lam * o[DIFF_HEADS + h:DIFF_HEADS + h + 1, cs]
            o_ref[:, cs] = _rms(oh, ng_ref[...], DIFF_NORM_EPS) * (1.0 - lam_init)


def diff_attend_sample(page_table, qt, new_rows, lam_vec, norm_g, cache_v, lam_init):
    bs, n_pages = page_table.shape
    p = DIFF_PAGES_PER_STEP
    hw = DIFF_HEADS * LANES
    page_specs = [pl.BlockSpec((None, PAGE_SIZE * DIFF_ROW_STRIDE, LANES),
                               functools.partial(lambda i, c, pt, k: (pt[i, c * p + k], 0, 0), k=k))
                  for k in range(p)]
    grid_spec = pltpu.PrefetchScalarGridSpec(
        num_scalar_prefetch=1,
        grid=(bs, n_pages // p),
        in_specs=[pl.BlockSpec((None, 2 * DIFF_HEADS, hw), lambda i, c, pt: (i, 0, 0)),
                  pl.BlockSpec((None, 1, 2 * hw), lambda i, c, pt: (i, 0, 0)),
                  pl.BlockSpec(lam_vec.shape, lambda i, c, pt: (0, 0)),
                  pl.BlockSpec((1, LANES), lambda i, c, pt: (0, 0))] + page_specs,
        out_specs=pl.BlockSpec((None, 1, hw), lambda i, c, pt: (i, 0, 0)),
        scratch_shapes=[pltpu.VMEM((2 * DIFF_HEADS, LANES), F32), pltpu.VMEM((2 * DIFF_HEADS, LANES), F32),
                        pltpu.VMEM((2 * DIFF_HEADS, hw), F32)],
    )
    return pl.pallas_call(
        functools.partial(_diff_sample_kernel, lam_init=lam_init),
        grid_spec=grid_spec,
        out_shape=jax.ShapeDtypeStruct((bs, 1, hw), F32),
        compiler_params=_cparams("parallel", "arbitrary"),
        name="diff_attend_sample",
    )(page_table, qt, new_rows, lam_vec, norm_g.reshape(1, LANES), *([cache_v] * p))


def _col_rep(row):
    x = jnp.broadcast_to(row, (LANES, row.shape[1]))
    return jnp.concatenate([x[:, c * LANES:(c + 1) * LANES].T for c in range(row.shape[1] // LANES)], axis=0)


def _head_sum(x):
    return jnp.sum(x.reshape(DIL_HEADS, HEAD_DIM, x.shape[1]), axis=1)


def _head_expand(x):
    return jnp.broadcast_to(x[:, None, :], (DIL_HEADS, HEAD_DIM, x.shape[1])).reshape(
        DIL_HEADS * HEAD_DIM, x.shape[1])


STATE_SHIFT_ROWS = 256


def _shift_in(st_ref, new_col, out_ref):
    n_rows, width = st_ref.shape
    lane = lax.broadcasted_iota(jnp.int32, (STATE_SHIFT_ROWS, LANES), 1)
    for r0 in range(0, n_rows, STATE_SHIFT_ROWS):
        rs = slice(r0, r0 + STATE_SHIFT_ROWS)
        rolled = pltpu.roll(st_ref[rs, :], width - 1, 1)
        if width > LANES:
            out_ref[rs, 0:width - LANES] = rolled[:, 0:width - LANES]
        out_ref[rs, width - LANES:width] = jnp.where(lane == LANES - 1, new_col[rs, :], rolled[:, width - LANES:width])


def _dil_sample_kernel(q_ref, new0_ref, new1_ref, new2_ref, st0_ref, st1_ref, st2_ref, o_ref,
                       nst0_ref, nst1_ref, nst2_ref):
    hw = DIL_HEADS * HEAD_DIM
    outs, lses = [], []
    for g, (new_ref, st_ref, nst_ref) in enumerate(((new0_ref, st0_ref, nst0_ref), (new1_ref, st1_ref, nst1_ref),
                                                    (new2_ref, st2_ref, nst2_ref))):
        win, dil = DIL_GROUPS[g]
        qc = _col_rep(q_ref[:, g * hw:(g + 1) * hw])
        kn = _col_rep(new_ref[:, 0:hw])
        vn = _col_rep(new_ref[:, hw:2 * hw])
        _shift_in(st_ref, jnp.concatenate([kn, vn], axis=0), nst_ref)
        s_new = _head_sum(qc * kn)[:, 0:1]
        n_chunks = win // LANES
        s = jnp.concatenate([_head_sum(st_ref[0:hw, c * LANES:(c + 1) * LANES] * qc) for c in range(n_chunks)],
                            axis=1)
        lane = lax.broadcasted_iota(jnp.int32, s.shape, 1)
        s = jnp.where(lane % dil == 0, s, -jnp.inf)
        m = jnp.maximum(jnp.max(s, axis=-1, keepdims=True), s_new)
        e = jnp.exp(s - m)
        e_new = jnp.exp(s_new - m)
        den = jnp.sum(e, axis=-1, keepdims=True) + e_new
        p = e / den
        acc = _head_expand(jnp.broadcast_to(e_new / den, (DIL_HEADS, LANES))) * vn * (1.0 / LANES)
        for c in range(n_chunks):
            cs = slice(c * LANES, (c + 1) * LANES)
            acc = acc + st_ref[hw:2 * hw, cs] * _head_expand(p[:, cs])
        outs.append(jnp.sum(acc, axis=-1, keepdims=True))
        lses.append(jnp.log(den) + m)
    mx = jnp.maximum(jnp.maximum(lses[0], lses[1]), lses[2])
    es = [jnp.exp(l - mx) for l in lses]
    tot = es[0] + es[1] + es[2]
    mix = jnp.zeros((hw, LANES), F32)
    for g in range(len(DIL_GROUPS)):
        alpha = _head_expand(jnp.broadcast_to(es[g] / tot, (DIL_HEADS, LANES)))
        mix = mix + alpha * jnp.broadcast_to(outs[g], (hw, LANES))
    rows = jnp.concatenate([mix[c * LANES:(c + 1) * LANES, :].T for c in range(hw // LANES)], axis=1)
    o_ref[...] = rows[0:1]


def dil_attend_sample(q, news, states_t):
    bs = q.shape[0]
    hw = DIL_HEADS * HEAD_DIM
    st_specs = [pl.BlockSpec((None, 2 * hw, st.shape[2]), lambda i: (i, 0, 0)) for st in states_t]
    outs = pl.pallas_call(
        _dil_sample_kernel,
        grid=(bs,),
        in_specs=[pl.BlockSpec((None, 1, q.shape[2]), lambda i: (i, 0, 0))]
                 + [pl.BlockSpec((None, 1, 2 * hw), lambda i: (i, 0, 0))] * 3 + st_specs,
        out_specs=[pl.BlockSpec((None, 1, hw), lambda i: (i, 0, 0))] + st_specs,
        out_shape=[jax.ShapeDtypeStruct((bs, 1, hw), F32)]
                  + [jax.ShapeDtypeStruct(st.shape, F32) for st in states_t],
        compiler_params=_cparams("parallel"),
        name="dil_attend_sample",
    )(q, *news, *states_t)
    return outs[0], outs[1:]


AB_SIZES = (NSA_HEADS * HEAD_DIM, 6 * NSA_KV_HEADS * HEAD_DIM, 3 * NSA_HEADS,
            DIFF_HEADS * 2 * HEAD_DIM, DIFF_HEADS * 2 * HEAD_DIM, DIFF_HEADS * 2 * HEAD_DIM)
_QSCALE = HEAD_DIM ** -0.5
_QSCALE2 = _QSCALE * LOG2E


def _rope_tables(pos):
    half = HEAD_DIM // 2
    inv = ROPE_THETA ** (-jnp.arange(half, dtype=F32) / half)
    ang = pos.astype(F32)[:, None] * inv[None, :]
    c, s = jnp.cos(ang), jnp.sin(ang)
    return jnp.tile(c, (1, 4)), jnp.tile(jnp.concatenate([-s, s], axis=1), (1, 2))


def _prep_w_ab(w):
    d = w.shape[0]
    qa, kvb, gl, qd, kd, vd = jnp.split(w, np.cumsum(AB_SIZES)[:-1].tolist(), axis=1)
    kvb = kvb.reshape(d, 6, LANES)
    k3 = kvb[:, 0::2].reshape(d, 3 * LANES)
    v3 = kvb[:, 1::2].reshape(d, 3 * LANES)
    gl = jnp.pad(gl, ((0, 0), (0, LANES - gl.shape[1])))
    return jnp.concatenate([qa, k3, v3, qd, kd, vd, gl], axis=1).astype(BF16)


def _ab_plan(dest, qscale):
    return (
        (0, 4, True, qscale, "half", tuple(dest("qa", j) for j in range(8))),
        (512, 3, True, 1.0, None, (dest("k_cmp", 0), dest("k_slc", 0), dest("k_win", 0))),
        (896, 3, False, 1.0, None, (dest("v_cmp", 0), dest("v_slc", 0), dest("v_win", 0))),
        (1280, 4, True, qscale, "pair", tuple(dest("qd", j) for j in range(8))),
        (1792, 4, True, 1.0, None, tuple(dest("kd", j) for j in range(4))),
        (2304, 4, False, 1.0, None, tuple(dest("vd", j) for j in range(4))),
        (2816, 1, False, 1.0, None, (dest("gate", 0),)),
    )


def _ab_prompt_defs(t):
    defs = (("T", 1024, BF16, t), ("T", 512, F32, t), ("N", 128, BF16), ("N", 256, F32),
            ("T", 256, F32, min(NSA_WINDOW, t)), ("N", 128, BF16), ("T", 1024, BF16, t), ("I", 8, F32),
            ("N", 512, BF16), ("T", 128, F32, t), ("T", 128, BF16, t), ("T", 128, BF16, t), ("T", 512, BF16, t))
    table = {
        "qa": lambda j: ((0, j),), "qd": lambda j: ((6, j),), "gate": lambda j: ((9, 0),),
        "k_cmp": lambda j: ((1, 0), (3, 0)), "v_cmp": lambda j: ((1, 1), (3, 1)),
        "k_slc": lambda j: ((1, 2), (2, 0)), "v_slc": lambda j: ((1, 3), (11, 0)),
        "k_win": lambda j: ((4, 0), (5, 0)), "v_win": lambda j: ((4, 1), (10, 0)),
        "kd": lambda j: ((7, j), (8, j)), "vd": lambda j: ((7, 4 + j), (12, j)),
    }
    return defs, _ab_plan(lambda name, j: table[name](j), _QSCALE2)


def _ab_sample_defs():
    defs = (("N", 1024, BF16), ("N", 512, F32), ("N", 256, F32), ("N", 1024, BF16), ("N", 1024, F32),
            ("N", 128, F32))
    table = {
        "qa": lambda j: ((0, j),), "qd": lambda j: ((3, j),), "gate": lambda j: ((5, 0),),
        "k_cmp": lambda j: ((1, 0),), "v_cmp": lambda j: ((1, 1),),
        "k_slc": lambda j: ((1, 2),), "v_slc": lambda j: ((1, 3),),
        "k_win": lambda j: ((2, 0),), "v_win": lambda j: ((2, 1),),
        "kd": lambda j: ((4, j),), "vd": lambda j: ((4, 4 + j),),
    }
    return defs, _ab_plan(lambda name, j: table[name](j), _QSCALE)


def _c_prompt_defs(t):
    defs, plan = [], []
    for g, (win, dil) in enumerate(DIL_GROUPS):
        defs += [("R", 1024, BF16, dil), ("R", 1024, BF16, dil), ("T", 1024, F32, min(win, t))]
        plan += [
            (g * 1536, 4, True, _QSCALE2, "pair", tuple(((3 * g, j),) for j in range(8))),
            (g * 1536 + 512, 4, True, 1.0, None, tuple(((3 * g + 1, j), (3 * g + 2, j)) for j in range(4))),
            (g * 1536 + 1024, 4, False, 1.0, None,
             tuple(((3 * g + 1, 4 + j), (3 * g + 2, 4 + j)) for j in range(4))),
        ]
    return tuple(defs), tuple(plan)


def _c_sample_defs():
    defs = (("N", 1536, F32), ("N", 1024, F32), ("N", 1024, F32), ("N", 1024, F32))
    plan = []
    for g in range(len(DIL_GROUPS)):
        plan += [
            (g * 1536, 4, True, _QSCALE, None, tuple(((0, 4 * g + j),) for j in range(4))),
            (g * 1536 + 512, 4, True, 1.0, None, tuple(((1 + g, j),) for j in range(4))),
            (g * 1536 + 1024, 4, False, 1.0, None, tuple(((1 + g, 4 + j),) for j in range(4))),
        ]
    return defs, tuple(plan)


def _prep_cmp(w_cmp, pe_cmp):
    wk, wv = w_cmp[0], w_cmp[1]
    z = jnp.zeros_like(wk)
    w4 = jnp.concatenate([jnp.concatenate([wk, z, z, z], axis=-1), jnp.concatenate([z, wk, z, z], axis=-1),
                          jnp.concatenate([z, z, wv, z], axis=-1), jnp.concatenate([z, z, z, wv], axis=-1)],
                         axis=1).astype(BF16)
    pe4 = jnp.concatenate([pe_cmp[0], pe_cmp[0], pe_cmp[1], pe_cmp[1]], axis=-1)
    half = CMP_LEN // 2
    return pe4, w4[:half], w4[half:]


def _block_indicator(n_keys):
    return (jnp.arange(n_keys)[:, None] // SLC_BLOCK == jnp.arange(LANES)[None, :]).astype(BF16)


def _cmp_to_block(n_cmp):
    r = SLC_BLOCK // CMP_STRIDE
    return (jnp.arange(n_cmp)[None, :] // r == jnp.arange(LANES)[:, None]).astype(BF16)


def _rows_from_t(x_t, lead):
    b, _, r = x_t.shape
    nd = len(lead)
    return x_t.reshape((b,) + tuple(lead) + (HEAD_DIM, r)).transpose((0, nd + 2) + tuple(range(1, nd + 2)))


def _rows_to_t(x):
    b, r = x.shape[:2]
    nd = x.ndim
    return x.transpose((0,) + tuple(range(2, nd)) + (1,)).reshape(b, -1, r)


def ab_mix_prompt(h, b, t, g_in, w_ab, cmp_prep, lam_vec, dn_g, lam_init, cos, sin):
    defs, plan = _ab_prompt_defs(t)
    (q_nsa_t, rows_nsa_t, kslc16, cmp_rows, rows_win_t, kwin16, q_diff_t, rows_diff, k_diff16, gate_t, vwin_t,
     vslc_t, v_diff_t) = project(h, b, g_in, cos, sin, w_ab, plan, defs)
    r3 = lambda x: x.reshape(b, t, x.shape[-1])
    pe4, wa, wb = cmp_prep
    kcmp16, vcmp_t = nsa_compress_prompt(r3(cmp_rows), pe4, wa, wb)
    o_nsa = nsa_attend_prompt(q_nsa_t, gate_t, kcmp16, vcmp_t, r3(kslc16), vslc_t, r3(kwin16), vwin_t,
                              _block_indicator(t), _cmp_to_block(t // CMP_STRIDE))
    o_diff = diff_attend_prompt(q_diff_t, r3(k_diff16), v_diff_t, lam_vec, dn_g, lam_init)
    mixed = [o_nsa.reshape(b * t, -1), o_diff.reshape(b * t, -1)]
    return mixed, rows_nsa_t, rows_win_t, rows_diff.reshape(b, t, 2, DIFF_HEADS, 2 * HEAD_DIM)


def dil_mix_prompt(h, b, t, g_in, w_c, cos, sin):
    defs, plan = _c_prompt_defs(t)
    outs = project(h, b, g_in, cos, sin, w_c, plan, defs)
    os_, lses, rows_t = [], [], []
    for gi in range(len(DIL_GROUPS)):
        o, lse = dil_attend_prompt(outs[3 * gi], outs[3 * gi + 1])
        os_.append(o)
        lses.append(lse)
        rows_t.append(outs[3 * gi + 2])
    return os_, lses, rows_t


def _diff_qt(q_diff):
    bs = q_diff.shape[0]
    qd = q_diff.reshape(bs, DIFF_HEADS, 2, LANES).transpose(0, 2, 1, 3)
    eye = jnp.eye(DIFF_HEADS, dtype=q_diff.dtype)
    return (qd[:, :, :, None, :] * eye[None, None, :, :, None]).reshape(bs, 2 * DIFF_HEADS, DIFF_HEADS * LANES)


def ab_mix_sample(hs, g_in, w_ab, cmp_prep, lam_vec, dn_g, lam_init, cos, sin,
                  cache_nsa, cache_diff, win_state, page_table):
    bs = hs.shape[0]
    defs, plan = _ab_sample_defs()
    q_nsa, rows_nsa, rows_win, q_diff, rows_diff, gate = project(hs, 1, g_in, cos, sin, w_ab, plan, defs)
    n_pages = page_table.shape[1]
    past = n_pages * PAGE_SIZE
    assert past // SLC_BLOCK == LANES, "selection-block axis is laid out on the 128 lanes"
    n_pool = cache_nsa.shape[0]
    cache_t = _rows_to_t(cache_nsa)
    cache_v = cache_diff.reshape(n_pool, PAGE_SIZE * DIFF_ROW_STRIDE, LANES)
    pe4, wa, wb = cmp_prep
    cmp_kv = nsa_compress_sample(cache_t, page_table, pe4, wa, wb)
    q8 = q_nsa.reshape(bs, NSA_HEADS, LANES)
    o_cmp, imp = nsa_sample_cmp(q8, cmp_kv, _cmp_to_block(past // CMP_STRIDE), past)
    sel_idx = topk_lanes(imp, N_SEL_CACHE)[:, :NSA_KV_HEADS, :N_SEL_CACHE].reshape(bs, -1)
    new_rows = jnp.concatenate([rows_nsa, rows_win], axis=-1).reshape(bs, 1, -1)
    o8 = nsa_sample_attend(page_table, sel_idx, q8, gate.reshape(bs, 1, LANES), o_cmp, new_rows,
                           _rows_to_t(win_state), cache_t)
    o8 = o8.reshape(bs, NSA_HEADS, 2, HEAD_DIM)
    o_nsa = jnp.concatenate([o8[:, :NSA_GROUP, 0], o8[:, NSA_GROUP:, 1]], axis=1).reshape(bs, -1)
    o_diff = diff_attend_sample(page_table, _diff_qt(q_diff), rows_diff.reshape(bs, 1, -1), lam_vec, dn_g,
                                cache_v, lam_init).reshape(bs, -1)
    mixed = jnp.concatenate([o_nsa, o_diff], axis=-1).astype(BF16)
    return mixed, rows_nsa, rows_win, rows_diff


def dil_mix_sample(hs, g_in, w_c_rows, cos, sin, states):
    bs = hs.shape[0]
    defs, plan = _c_sample_defs()
    outs = project(hs, 1, g_in, cos, sin, w_c_rows, plan, defs)
    news = [x.reshape(bs, 1, -1) for x in outs[1:]]
    sts = []
    for (win, dil), st in zip(DIL_GROUPS, states):
        assert st.shape[1] == win and win == DIL_BAND * dil, "state buffer must hold the full dilated window"
        sts.append(_rows_to_t(st))
    o, new_sts = dil_attend_sample(outs[0].reshape(bs, 1, -1), news, sts)
    return o.reshape(bs, -1).astype(BF16), [_rows_from_t(x, (2, DIL_HEADS)) for x in new_sts]


def kernel(x_prompt, x_sample, cache_nsa, cache_diff, state_nsa_win, state_dil_0, state_dil_1, state_dil_2,
           page_table, norm_g, ffn_w_in, ffn_w_out, w_in_ab, w_out_ab, nsa_w_cmp, nsa_pe_cmp, diff_lambda,
           diff_norm_g, w_in_c, w_out_c):
    b, t, d = x_prompt.shape
    bs, ns, _ = x_sample.shape
    assert ns == 1, "sample group is one new token per sequence"
    depth = norm_g.shape[0]
    past = page_table.shape[1] * PAGE_SIZE
    hp = x_prompt.reshape(b * t, d)
    hs = x_sample.reshape(bs, d)
    cos_p, sin_p = _rope_tables(jnp.tile(jnp.arange(t, dtype=jnp.int32), b))
    cos_s, sin_s = _rope_tables(jnp.full((bs,), past, jnp.int32))
    w_ffn_in = ffn_w_in.astype(BF16)
    w_ffn_out = ffn_w_out.astype(BF16)
    state_dil = (state_dil_0, state_dil_1, state_dil_2)
    nsa_p, nsa_s, win_p, win_s, diff_p, diff_s = [], [], [], [], [], []
    dil_p = [[] for _ in DIL_GROUPS]
    dil_s = [[] for _ in DIL_GROUPS]
    for layer in range(depth):
        g = norm_g[layer]
        hp = ffn_half(hp, g[0], g[1], w_ffn_in[layer, 0], w_ffn_out[layer, 0])
        hs = ffn_half(hs, g[0], g[1], w_ffn_in[layer, 0], w_ffn_out[layer, 0])
        if layer % 2 == 0:
            e = layer // 2
            lam_init = 0.8 - 0.6 * math.exp(-0.3 * layer)
            w_ab = _prep_w_ab(w_in_ab[e])
            cmp_prep = _prep_cmp(nsa_w_cmp[e], nsa_pe_cmp[e])
            mp, rn_t, rw_t, rd = ab_mix_prompt(hp, b, t, g[2], w_ab, cmp_prep, diff_lambda[e], diff_norm_g[e],
                                               lam_init, cos_p, sin_p)
            nsa_p.append(_rows_from_t(rn_t, (4, NSA_KV_HEADS)))
            win_p.append(_rows_from_t(rw_t[:, :, -min(NSA_WINDOW, t):], (2, NSA_KV_HEADS)))
            diff_p.append(rd.reshape(b, t, 2, DIFF_HEADS, 2 * HEAD_DIM))
            ms, rn, rw, rd = ab_mix_sample(hs, g[2], w_ab, cmp_prep, diff_lambda[e], diff_norm_g[e], lam_init,
                                           cos_s, sin_s, cache_nsa[e], cache_diff[e], state_nsa_win[e], page_table)
            nsa_s.append(rn.reshape(bs, 1, 4, NSA_KV_HEADS, HEAD_DIM))
            win_full = jnp.concatenate([state_nsa_win[e], rw.reshape(bs, 1, 2, NSA_KV_HEADS, HEAD_DIM)], axis=1)
            win_s.append(win_full[:, -min(NSA_WINDOW, win_full.shape[1]):])
            diff_s.append(rd.reshape(bs, 1, 2, DIFF_HEADS, 2 * HEAD_DIM))
            w_o = w_out_ab[e].astype(BF16)
            hp = outproj(hp, mp, w_o, g[3])
            hs = outproj(hs, [ms], w_o, g[3])
        else:
            o = layer // 2
            w_o = w_out_c[o].astype(BF16)
            w_c = w_in_c[o].astype(BF16)
            os_, lses, rows_t = dil_mix_prompt(hp, b, t, g[2], w_c, cos_p, sin_p)
            hp = outproj_dil(hp, b, os_, lses, w_o, g[3])
            ms, new_states = dil_mix_sample(hs, g[2], w_c, cos_s, sin_s, [st[o] for st in state_dil])
            hs = outproj(hs, [ms], w_o, g[3])
            for gi, (win, dil) in enumerate(DIL_GROUPS):
                dil_p[gi].append(_rows_from_t(rows_t[gi][:, :, -min(win, t):], (2, DIL_HEADS)))
                dil_s[gi].append(new_states[gi])
        hp = ffn_half(hp, g[4], g[5], w_ffn_in[layer, 1], w_ffn_out[layer, 1])
        hs = ffn_half(hs, g[4], g[5], w_ffn_in[layer, 1], w_ffn_out[layer, 1])
    return (hp.reshape(b, t, d), hs.reshape(bs, 1, d), jnp.stack(nsa_p), jnp.stack(nsa_s), jnp.stack(win_p),
            jnp.stack(win_s), jnp.stack(diff_p), jnp.stack(diff_s), jnp.stack(dil_p[0]), jnp.stack(dil_s[0]),
            jnp.stack(dil_p[1]), jnp.stack(dil_s[1]), jnp.stack(dil_p[2]), jnp.stack(dil_s[2]))
```

```python
import functools
import math

import jax
import jax.numpy as jnp
import numpy as np
from jax import lax
from jax.experimental import pallas as pl
from jax.experimental.pallas import tpu as pltpu

F32 = jnp.float32
BF16 = jnp.bfloat16

LANES = 128
SUBLANES = 8
HEAD_DIM = 64
ROPE_THETA = 10000.0
NORM_EPS = 1e-6
PAGE_SIZE = 128
NSA_HEADS = 8
NSA_KV_HEADS = 2
NSA_GROUP = NSA_HEADS // NSA_KV_HEADS
CMP_LEN = 32
CMP_STRIDE = 16
SLC_BLOCK = 64
N_SEL = 16
NSA_WINDOW = 512
FORCED_BLOCK_SCORE = 1.0e4
DIFF_HEADS = 4
DIFF_NORM_EPS = 1e-5
DIL_GROUPS = ((128, 1), (512, 4), (2048, 16))
DIL_HEADS = 8
DIL_BAND = 128
NEG_BIG = -1.0e30
LN2 = math.log(2.0)
LOG2E = 1.0 / LN2
VMEM_LIMIT_BYTES = 56 * 1024 * 1024
TOKEN_TILE = 512


def _cparams(*sem):
    return pltpu.CompilerParams(dimension_semantics=sem, vmem_limit_bytes=VMEM_LIMIT_BYTES)


def _const_spec(shape):
    nd = len(shape)
    return pl.BlockSpec(shape, lambda *_: (0,) * nd, pipeline_mode=pl.Buffered(1))


def _rms(x, g, eps):
    return x * lax.rsqrt(jnp.mean(x * x, axis=-1, keepdims=True) + eps) * g


def _dot(a, b):
    return jnp.dot(a, b, preferred_element_type=F32)


def _dot_nt(a, b):
    return lax.dot_general(a, b, (((1,), (1,)), ((), ())), preferred_element_type=F32)


def _split3(x):
    hi = x.astype(BF16)
    r1 = x - hi.astype(F32)
    mid = r1.astype(BF16)
    lo = (r1 - mid.astype(F32)).astype(BF16)
    return hi, mid, lo


def _pad_rows(x, rows):
    return jnp.concatenate([x, jnp.zeros((rows - x.shape[0], x.shape[1]), x.dtype)], axis=0)


def _token_tile(n):
    return TOKEN_TILE if n % TOKEN_TILE == 0 else n


FFN_CHUNK = 256


def _ffn_kernel(x_ref, gpre_ref, gpost_ref, win_ref, wout_ref, o_ref, *, d_ff):
    x = x_ref[...]
    xn = _rms(x, gpre_ref[...], NORM_EPS).astype(BF16)
    acc = jnp.zeros(x.shape, F32)
    for c in range(d_ff // FFN_CHUNK):
        lo = c * FFN_CHUNK
        gate = _dot(xn, win_ref[:, lo:lo + FFN_CHUNK].astype(BF16))
        up = _dot(xn, win_ref[:, d_ff + lo:d_ff + lo + FFN_CHUNK].astype(BF16))
        act = (gate * jax.nn.sigmoid(gate) * up).astype(BF16)
        acc = acc + _dot(act, wout_ref[lo:lo + FFN_CHUNK, :].astype(BF16))
    o_ref[...] = x + 0.5 * _rms(acc, gpost_ref[...], NORM_EPS)


def ffn_half(h, g_pre, g_post, w_in, w_out):
    n, d = h.shape
    d_ff = w_out.shape[0]
    tm = _token_tile(n)
    return pl.pallas_call(
        functools.partial(_ffn_kernel, d_ff=d_ff),
        grid=(n // tm,),
        in_specs=[pl.BlockSpec((tm, d), lambda i: (i, 0)),
                  _const_spec((1, d)), _const_spec((1, d)),
                  _const_spec(w_in.shape), _const_spec(w_out.shape)],
        out_specs=pl.BlockSpec((tm, d), lambda i: (i, 0)),
        out_shape=jax.ShapeDtypeStruct((n, d), F32),
        compiler_params=_cparams("parallel"),
        name="ffn_half",
    )(h, g_pre.reshape(1, d), g_post.reshape(1, d), w_in, w_out)


def _rope_slab(y, cos, sin):
    lane = lax.broadcasted_iota(jnp.int32, y.shape, 1)
    swapped = jnp.where(lane % HEAD_DIM < HEAD_DIM // 2,
                        pltpu.roll(y, LANES - HEAD_DIM // 2, 1),
                        pltpu.roll(y, HEAD_DIM // 2, 1))
    return y * cos + swapped * sin


def _proj_kernel(x_ref, g_ref, cos_ref, sin_ref, w_ref, *refs, plan, out_defs, first_tiles, tiles_per_b):
    out_refs = refs[:len(out_defs)]
    scr_ref = refs[len(out_defs)]
    tm = x_ref.shape[0]
    tile_in_b = pl.program_id(0) % tiles_per_b
    xn = _rms(x_ref[...], g_ref[...], NORM_EPS).astype(BF16)
    cos = cos_ref[...]
    sin = sin_ref[...]
    lane = lax.broadcasted_iota(jnp.int32, (tm, LANES), 1)

    def emit(val, val_t, out_idx, slab):
        ref = out_refs[out_idx]
        kind = out_defs[out_idx][0]
        cs = slice(slab * LANES, (slab + 1) * LANES)
        if kind == "N":
            ref[:, cs] = val().astype(ref.dtype)
        elif kind == "I":
            ref[pl.ds(slab, tm, stride=out_defs[out_idx][1]), :] = val().astype(ref.dtype)
        elif kind == "T":
            def write_t():
                ref[cs, :] = val_t().astype(ref.dtype)
            if first_tiles[out_idx] == 0:
                write_t()
            else:
                pl.when(tile_in_b >= first_tiles[out_idx])(write_t)
        else:
            dil = out_defs[out_idx][3]
            if dil == 1:
                ref[0, :, cs] = val().astype(ref.dtype)
            else:
                scr_ref[...] = val()
                for r in range(dil):
                    ref[r, :, cs] = scr_ref[pl.ds(r, tm // dil, stride=dil), :].astype(ref.dtype)

    for col0, nslab, rope, scale, pad, dests in plan:
        y = _dot(xn, w_ref[:, col0:col0 + nslab * LANES])
        for j in range(nslab):
            ys = y[:, j * LANES:(j + 1) * LANES]
            if rope:
                ys = _rope_slab(ys, cos, sin)
            if scale != 1.0:
                ys = ys * scale
            if pad is None:
                for out_idx, slab in dests[j]:
                    emit(lambda ys=ys: ys, lambda ys=ys: ys.T, out_idx, slab)
                continue
            transposed = []

            def ys_t(ys=ys, transposed=transposed):
                if not transposed:
                    transposed.append(ys.T)
                return transposed[0]

            for hh in range(2):
                head = 2 * j + hh
                at_hi = hh == 1 if pad == "pair" else head >= nslab

                def val(ys=ys, hh=hh, at_hi=at_hi):
                    v = ys if at_hi == (hh == 1) else pltpu.roll(ys, HEAD_DIM, 1)
                    return jnp.where(lane >= HEAD_DIM if at_hi else lane < HEAD_DIM, v, 0.0)

                def val_t(ys_t=ys_t, hh=hh, at_hi=at_hi):
                    rows = ys_t()[hh * HEAD_DIM:(hh + 1) * HEAD_DIM, :]
                    zero = jnp.zeros_like(rows)
                    return jnp.concatenate([zero, rows] if at_hi else [rows, zero], axis=0)

                for out_idx, slab in dests[head]:
                    emit(val, val_t, out_idx, slab)


def project(h, b, g, cos, sin, w, plan, out_defs):
    n, d = h.shape
    t = n // b
    tm = _token_tile(t)
    tpb = t // tm
    specs, shapes, first_tiles = [], [], []
    for od in out_defs:
        kind, c, dt = od[:3]
        if kind == "N":
            specs.append(pl.BlockSpec((tm, c), lambda i: (i, 0)))
            shapes.append(jax.ShapeDtypeStruct((n, c), dt))
            first_tiles.append(0)
        elif kind == "I":
            specs.append(pl.BlockSpec((tm * c, LANES), lambda i: (i, 0)))
            shapes.append(jax.ShapeDtypeStruct((n * c, LANES), dt))
            first_tiles.append(0)
        elif kind == "T":
            keep = max(min(od[3], t), tm)
            ft = (t - keep) // tm
            specs.append(pl.BlockSpec((None, c, tm),
                                      functools.partial(lambda i, ft: (i // tpb, 0, jnp.maximum(i % tpb - ft, 0)),
                                                        ft=ft)))
            shapes.append(jax.ShapeDtypeStruct((b, c, keep), dt))
            first_tiles.append(ft)
        else:
            dil = od[3]
            specs.append(pl.BlockSpec((None, dil, tm // dil, c), lambda i: (i // tpb, 0, i % tpb, 0)))
            shapes.append(jax.ShapeDtypeStruct((b, dil, t // dil, c), dt))
            first_tiles.append(0)
    return pl.pallas_call(
        functools.partial(_proj_kernel, plan=plan, out_defs=out_defs, first_tiles=tuple(first_tiles),
                          tiles_per_b=tpb),
        grid=(n // tm,),
        in_specs=[pl.BlockSpec((tm, d), lambda i: (i, 0)), _const_spec((1, d)),
                  pl.BlockSpec((tm, LANES), lambda i: (i, 0)),
                  pl.BlockSpec((tm, LANES), lambda i: (i, 0)),
                  _const_spec(w.shape)],
        out_specs=specs,
        out_shape=shapes,
        scratch_shapes=[pltpu.VMEM((tm, LANES), F32)],
        compiler_params=_cparams("arbitrary"),
        name="project",
    )(h, g.reshape(1, d), cos, sin, w)


def _outproj_kernel(h_ref, *refs):
    w_ref, g_ref, o_ref = refs[-3:]
    y, row0 = None, 0
    for m_ref in refs[:-3]:
        c = m_ref.shape[1]
        part = _dot(m_ref[...], w_ref[row0:row0 + c, :])
        y = part if y is None else y + part
        row0 += c
    o_ref[...] = h_ref[...] + _rms(y, g_ref[...], NORM_EPS)


def outproj(h, ms, w, g):
    n, d = h.shape
    tm = _token_tile(n)
    return pl.pallas_call(
        _outproj_kernel,
        grid=(n // tm,),
        in_specs=[pl.BlockSpec((tm, d), lambda i: (i, 0))]
                 + [pl.BlockSpec((tm, m.shape[1]), lambda i: (i, 0)) for m in ms]
                 + [_const_spec(w.shape), _const_spec((1, d))],
        out_specs=pl.BlockSpec((tm, d), lambda i: (i, 0)),
        out_shape=jax.ShapeDtypeStruct((n, d), F32),
        compiler_params=_cparams("parallel"),
        name="outproj",
    )(h, *ms, w, g.reshape(1, d))


def _outproj_dil_kernel(h_ref, *refs):
    ng = len(DIL_GROUPS)
    w_ref, g_ref, o_ref, scr_ref = refs[2 * ng:]
    tm = h_ref.shape[0]
    nslab = DIL_HEADS * HEAD_DIM // LANES
    vals = []
    k = 0
    for gi, (_, dil) in enumerate(DIL_GROUPS):
        per_g = []
        for ref in (refs[2 * gi], refs[2 * gi + 1]):
            slabs = []
            for s in range(nslab):
                cs = slice(s * LANES, (s + 1) * LANES)
                if dil == 1:
                    slabs.append(ref[0, :, cs])
                else:
                    for r in range(dil):
                        scr_ref[k, pl.ds(r, tm // dil, stride=dil), :] = ref[r, :, cs]
                    slabs.append(scr_ref[k])
                    k += 1
            per_g.append(slabs)
        vals.append(per_g)
    mixed = []
    for s in range(nslab):
        l0, l1, l2 = vals[0][1][s], vals[1][1][s], vals[2][1][s]
        mx = jnp.maximum(jnp.maximum(l0, l1), l2)
        e0, e1, e2 = jnp.exp(l0 - mx), jnp.exp(l1 - mx), jnp.exp(l2 - mx)
        den = e0 + e1 + e2
        mixed.append(((e0 / den) * vals[0][0][s] + (e1 / den) * vals[1][0][s]
                      + (e2 / den) * vals[2][0][s]).astype(BF16))
    y = _dot(jnp.concatenate(mixed, axis=1), w_ref[...])
    o_ref[...] = h_ref[...] + _rms(y, g_ref[...], NORM_EPS)


def outproj_dil(h, b, outs, lses, w, g):
    n, d = h.shape
    t = n // b
    tm = _token_tile(t)
    tpb = t // tm
    c = DIL_HEADS * HEAD_DIM
    specs, args, n_scr = [], [], 0
    for (_, dil), o, l in zip(DIL_GROUPS, outs, lses):
        spec = pl.BlockSpec((None, dil, tm // dil, c), lambda i: (i // tpb, 0, i % tpb, 0))
        specs += [spec, spec]
        args += [o, l]
        if dil > 1:
            n_scr += 2 * (c // LANES)
    return pl.pallas_call(
        _outproj_dil_kernel,
        grid=(n // tm,),
        in_specs=[pl.BlockSpec((tm, d), lambda i: (i, 0))] + specs + [_const_spec(w.shape), _const_spec((1, d))],
        out_specs=pl.BlockSpec((tm, d), lambda i: (i, 0)),
        out_shape=jax.ShapeDtypeStruct((n, d), F32),
        scratch_shapes=[pltpu.VMEM((n_scr, tm, LANES), F32)],
        compiler_params=_cparams("parallel"),
        name="outproj_dil",
    )(h, *args, w, g.reshape(1, d))


def _compress_rows(k_ref, v_ref, pe_ref, wa_ref, wb_ref, nsub):
    half = CMP_LEN // 2
    acc_a = jnp.zeros((nsub, 2 * LANES), F32)
    acc_b = jnp.zeros((nsub, 2 * LANES), F32)
    for l in range(half):
        x = jnp.concatenate([k_ref[pl.ds(l, nsub, stride=CMP_STRIDE), :],
                             v_ref[pl.ds(l, nsub, stride=CMP_STRIDE), :]], axis=1)
        acc_a = acc_a + _dot((x + pe_ref[l:l + 1, :]).astype(BF16), wa_ref[l])
        acc_b = acc_b + _dot((x + pe_ref[half + l:half + l + 1, :]).astype(BF16), wb_ref[l])
    comp = acc_a + pltpu.roll(acc_b, nsub - 1, 0)
    row = lax.broadcasted_iota(jnp.int32, comp.shape, 0)
    return jnp.where(row < nsub - 1, comp, 0.0)


def _compress_kernel(k_ref, v_ref, pe_ref, wa_ref, wb_ref, k_out_ref, vt_out_ref, *, nsub):
    comp = _compress_rows(k_ref, v_ref, pe_ref, wa_ref, wb_ref, nsub)
    k_out_ref[...] = comp[:, 0:LANES].astype(k_out_ref.dtype)
    vt_out_ref[...] = comp[:, LANES:2 * LANES].T.astype(vt_out_ref.dtype)


def nsa_compress_prompt(cmp_rows, pe4, wa, wb):
    b, t, _ = cmp_rows.shape
    nsub = t // CMP_STRIDE
    return pl.pallas_call(
        functools.partial(_compress_kernel, nsub=nsub),
        grid=(b,),
        in_specs=[pl.BlockSpec((None, t, LANES), lambda i: (i, 0, 0)),
                  pl.BlockSpec((None, t, LANES), lambda i: (i, 0, 1)),
                  _const_spec(pe4.shape), _const_spec(wa.shape), _const_spec(wb.shape)],
        out_specs=[pl.BlockSpec((None, nsub, LANES), lambda i: (i, 0, 0)),
                   pl.BlockSpec((None, LANES, nsub), lambda i: (i, 0, 0))],
        out_shape=[jax.ShapeDtypeStruct((b, nsub, LANES), BF16), jax.ShapeDtypeStruct((b, LANES, nsub), BF16)],
        compiler_params=_cparams("parallel"),
        name="nsa_compress_prompt",
    )(cmp_rows, cmp_rows, pe4, wa, wb)


NSA_TQ = 256
NSA_TK = 512


def _softmax_terms(s, mask, exp_fn):
    s = jnp.where(mask, s, -jnp.inf)
    m = jnp.max(s, axis=-1, keepdims=True)
    m = jnp.where(m == -jnp.inf, 0.0, m)
    e = exp_fn(s - m)
    den = jnp.sum(e, axis=-1, keepdims=True)
    return e, 1.0 / jnp.where(den > 0, den, 1.0)


def _softmax_rows(s, mask, exp_fn=jnp.exp):
    e, inv = _softmax_terms(s, mask, exp_fn)
    return e * inv


def _topk_mask_t(score_t, k):
    j_io = lax.broadcasted_iota(jnp.int32, score_t.shape, 0)
    nj = score_t.shape[0]
    work = score_t
    for _ in range(k):
        m = jnp.max(work, axis=0, keepdims=True)
        jmin = jnp.min(jnp.where(work == m, j_io, nj), axis=0, keepdims=True)
        work = jnp.where(j_io == jmin, -jnp.inf, work)
    return jnp.logical_and(work == -jnp.inf, score_t > -jnp.inf)


SWEEP_CHUNKS = 4


def _causal_sweep_t(score_fn, pv_fn, mask_fn, n_full, cols):
    nc = SWEEP_CHUNKS
    cc = cols // nc

    def update(tiles, masked, state):
        scs = [[score_fn(t, c) for t in tiles] for c in range(nc)]
        out = []
        for c, (m, l, acc) in enumerate(state):
            sc = scs[c]
            if masked:
                sc = [jnp.where(mask_fn(t, c), s, NEG_BIG) for t, s in zip(tiles, sc)]
            m_new = m
            for s in sc:
                m_new = jnp.maximum(m_new, jnp.max(s, axis=0, keepdims=True))
            alpha = jnp.exp2(m - m_new)
            l = alpha * l
            acc = alpha * acc
            for t, s in zip(tiles, sc):
                pe = jnp.exp2(s - m_new)
                l = l + jnp.sum(pe, axis=0, keepdims=True)
                acc = acc + pv_fn(t, pe.astype(BF16))
            out.append((m_new, l, acc))
        return out

    init = [(jnp.full((1, cc), NEG_BIG, F32), jnp.zeros((1, cc), F32), jnp.zeros((LANES, cc), F32))
            for _ in range(nc)]
    state = lax.fori_loop(0, n_full // 4, lambda u, st: update([4 * u + k for k in range(4)], False, st), init)
    base2 = (n_full // 4) * 4
    state = lax.fori_loop(0, (n_full % 4) // 2, lambda _, st: update([base2, base2 + 1], False, st), state)
    state = lax.fori_loop(0, n_full % 2, lambda _, st: update([n_full - 1], False, st), state)
    state = update([n_full], True, state)
    return (jnp.concatenate([s[1] for s in state], axis=1), jnp.concatenate([s[2] for s in state], axis=1))


def _softmax_terms_t(s, mask):
    s = jnp.where(mask, s, -jnp.inf)
    m = jnp.max(s, axis=0, keepdims=True)
    m = jnp.where(m == -jnp.inf, 0.0, m)
    e = jnp.exp2(s - m)
    den = jnp.sum(e, axis=0, keepdims=True)
    return e, 1.0 / jnp.where(den > 0, den, 1.0)


def _nsa_prompt_kernel(qt_ref, gt_ref, kcmp_ref, vcmp_ref, kslc_ref, vslc_ref, kwin_ref, vwin_ref, eall_ref,
                       selt_ref, o_ref):
    tq = NSA_TQ
    qs = pl.program_id(1) * tq
    rows = NSA_GROUP * tq
    qpos = qs + lax.broadcasted_iota(jnp.int32, (1, rows), 1) % tq
    ncmp = kcmp_ref.shape[0]
    gates_t = jax.nn.sigmoid(gt_ref[...])
    head_out = []
    for kvh in range(NSA_KV_HEADS):
        q4t = jnp.concatenate(
            [qt_ref[(NSA_GROUP * kvh + g) * LANES:(NSA_GROUP * kvh + g + 1) * LANES, :]
             for g in range(NSA_GROUP)], axis=1)
        s = _dot(kcmp_ref[...], q4t)
        wlen = NSA_WINDOW + tq
        ws = pl.multiple_of(jnp.maximum(qs - NSA_WINDOW, 0), tq)
        sw = _dot(kwin_ref[pl.ds(ws, wlen), :], q4t)
        cmp_end = lax.broadcasted_iota(jnp.int32, (ncmp, 1), 0) * CMP_STRIDE + (CMP_LEN - 1)
        e, inv = _softmax_terms_t(s, cmp_end <= qpos)
        p = e * inv
        o_cmp_t = _dot(vcmp_ref[...], p.astype(BF16))
        psum = p[:, 0:tq] + p[:, tq:2 * tq] + p[:, 2 * tq:3 * tq] + p[:, 3 * tq:4 * tq]
        hi, mid, lo = _split3(psum)
        selt = selt_ref[...]
        imp_t = _dot(selt, hi) + _dot(selt, mid) + _dot(selt, lo)
        j_io = lax.broadcasted_iota(jnp.int32, imp_t.shape, 0)
        cur = (qs + lax.broadcasted_iota(jnp.int32, imp_t.shape, 1)) // SLC_BLOCK
        forced = jnp.logical_or(j_io == 0, j_io == cur)
        score_t = jnp.where(j_io <= cur, jnp.where(forced, FORCED_BLOCK_SCORE, imp_t), -jnp.inf)
        bias_t = jnp.where(_topk_mask_t(score_t, N_SEL), 0.0, NEG_BIG).astype(BF16)
        qext_t = jnp.concatenate([q4t, jnp.concatenate([bias_t] * NSA_GROUP, axis=1)], axis=0)

        def tile(t):
            return pl.ds(t * NSA_TK if isinstance(t, int) else pl.multiple_of(t * NSA_TK, NSA_TK), NSA_TK)

        cc = rows // SWEEP_CHUNKS

        def sel_scores(t, c, qext_t=qext_t):
            return _dot(jnp.concatenate([kslc_ref[tile(t), :], eall_ref[tile(t), :]], axis=1),
                        qext_t[:, c * cc:(c + 1) * cc])

        def sel_values(t, p):
            return _dot(vslc_ref[:, tile(t)], p)

        def sel_mask(t, c):
            kpos = t * NSA_TK + lax.broadcasted_iota(jnp.int32, (NSA_TK, 1), 0)
            return kpos <= qs + (c * cc + lax.broadcasted_iota(jnp.int32, (1, cc), 1)) % tq

        l_sel, acc_sel = _causal_sweep_t(sel_scores, sel_values, sel_mask, qs // NSA_TK, rows)
        o_sel_t = acc_sel / l_sel
        dist = qpos - (ws + lax.broadcasted_iota(jnp.int32, (wlen, 1), 0))
        ew, inv_w = _softmax_terms_t(sw, jnp.logical_and(dist >= 0, dist < NSA_WINDOW))
        o_win_t = _dot(vwin_ref[:, pl.ds(ws, wlen)], ew.astype(BF16)) * inv_w
        for g in range(NSA_GROUP):
            h = NSA_GROUP * kvh + g
            cs = slice(g * tq, (g + 1) * tq)
            mixed = (gates_t[3 * h:3 * h + 1, :] * o_cmp_t[:, cs] + gates_t[3 * h + 1:3 * h + 2, :] * o_sel_t[:, cs]
                     + gates_t[3 * h + 2:3 * h + 3, :] * o_win_t[:, cs])
            head_out.append(mixed[kvh * HEAD_DIM:(kvh + 1) * HEAD_DIM, :])
    for pair in range(NSA_HEADS // 2):
        slab_t = jnp.concatenate([head_out[2 * pair], head_out[2 * pair + 1]], axis=0)
        o_ref[:, pair * LANES:(pair + 1) * LANES] = slab_t.T.astype(o_ref.dtype)


def nsa_attend_prompt(q_nsa_t, gate_t, kcmp16, vcmp_t, kslc16, vslc_t, kwin16, vwin_t, eall, selt):
    b, _, t = q_nsa_t.shape
    ncmp = kcmp16.shape[1]
    rows = lambda n: pl.BlockSpec((None, n, LANES), lambda i, j: (i, 0, 0))
    cols = lambda n: pl.BlockSpec((None, LANES, n), lambda i, j: (i, 0, 0))
    return pl.pallas_call(
        _nsa_prompt_kernel,
        grid=(b, t // NSA_TQ),
        in_specs=[pl.BlockSpec((None, NSA_HEADS * LANES, NSA_TQ), lambda i, j: (i, 0, j)),
                  pl.BlockSpec((None, LANES, NSA_TQ), lambda i, j: (i, 0, j)),
                  rows(ncmp), cols(ncmp), rows(t), cols(t), rows(t), cols(t),
                  _const_spec(eall.shape), _const_spec(selt.shape)],
        out_specs=pl.BlockSpec((None, NSA_TQ, NSA_HEADS * HEAD_DIM), lambda i, j: (i, j, 0)),
        out_shape=jax.ShapeDtypeStruct((b, t, NSA_HEADS * HEAD_DIM), BF16),
        compiler_params=_cparams("parallel", "parallel"),
        name="nsa_attend_prompt",
    )(q_nsa_t, gate_t, kcmp16, vcmp_t, kslc16, vslc_t, kwin16, vwin_t, eall, selt)


DIFF_TQ = 512
DIFF_TK = 512


def _diff_lambda(lam_ref, lam_init):
    lv = lam_ref[...]
    a = jnp.sum(lv[0:1] * lv[1:2], axis=-1, keepdims=True)
    b = jnp.sum(lv[2:3] * lv[3:4], axis=-1, keepdims=True)
    return jnp.exp(a) - jnp.exp(b) + lam_init


def _diff_prompt_kernel(qt_ref, k_ref, vt_ref, lam_ref, ng_ref, o_ref, *, lam_init):
    tq = DIFF_TQ
    qs = pl.program_id(2) * tq
    q2t = jnp.concatenate([qt_ref[0:LANES, :], qt_ref[LANES:2 * LANES, :]], axis=1)
    cc = 2 * tq // SWEEP_CHUNKS

    def tile(t):
        return pl.ds(t * DIFF_TK if isinstance(t, int) else pl.multiple_of(t * DIFF_TK, DIFF_TK), DIFF_TK)

    def scores(t, c):
        return _dot(k_ref[tile(t), :], q2t[:, c * cc:(c + 1) * cc])

    def values(t, p):
        return _dot(vt_ref[:, tile(t)], p)

    def mask(t, c):
        kpos = t * DIFF_TK + lax.broadcasted_iota(jnp.int32, (DIFF_TK, 1), 0)
        qpos = qs + (c * cc + lax.broadcasted_iota(jnp.int32, (1, cc), 1)) % tq
        return kpos <= qpos

    l, acc = _causal_sweep_t(scores, values, mask, qs // DIFF_TK, 2 * tq)
    ot = acc / l
    lam = _diff_lambda(lam_ref, lam_init)
    ot = ot[:, 0:tq] - lam * ot[:, tq:2 * tq]
    o = ot.T
    o_ref[...] = (_rms(o, ng_ref[...], DIFF_NORM_EPS) * (1.0 - lam_init)).astype(o_ref.dtype)


def diff_attend_prompt(q_diff_t, k_diff16, v_diff_t, lam_vec, norm_g, lam_init):
    b, _, t = q_diff_t.shape
    return pl.pallas_call(
        functools.partial(_diff_prompt_kernel, lam_init=lam_init),
        grid=(b, DIFF_HEADS, t // DIFF_TQ),
        in_specs=[pl.BlockSpec((None, 2 * LANES, DIFF_TQ), lambda i, h, j: (i, h, j)),
                  pl.BlockSpec((None, t, LANES), lambda i, h, j: (i, 0, h)),
                  pl.BlockSpec((None, LANES, t), lambda i, h, j: (i, h, 0)),
                  _const_spec(lam_vec.shape), _const_spec((1, LANES))],
        out_specs=pl.BlockSpec((None, DIFF_TQ, LANES), lambda i, h, j: (i, j, h)),
        out_shape=jax.ShapeDtypeStruct((b, t, DIFF_HEADS * LANES), BF16),
        compiler_params=_cparams("parallel", "parallel", "parallel"),
        name="diff_attend_prompt",
    )(q_diff_t, k_diff16, v_diff_t, lam_vec, norm_g.reshape(1, LANES))


def _dil_prompt_kernel(q_ref, kvp_ref, kvc_ref, o_ref, lse_ref):
    band = DIL_BAND
    hw = DIL_HEADS * HEAD_DIM
    first = pl.program_id(2) == 0
    qi = lax.broadcasted_iota(jnp.int32, (band, 2 * band), 0) + band
    kj = lax.broadcasted_iota(jnp.int32, (band, 2 * band), 1)
    rel = qi - kj
    ok = jnp.logical_and(rel >= 0, rel <= band)
    ok = jnp.logical_and(ok, jnp.logical_not(jnp.logical_and(first, kj < band)))
    lane = lax.broadcasted_iota(jnp.int32, (band, LANES), 1)
    scores = []
    for h in range(DIL_HEADS):
        cs = slice((h // 2) * LANES, (h // 2 + 1) * LANES)
        k2 = jnp.concatenate([kvp_ref[:, cs], kvc_ref[:, cs]], axis=0)
        scores.append(_dot_nt(q_ref[:, h * LANES:(h + 1) * LANES], k2))
    for pair in range(DIL_HEADS // 2):
        cs = slice(pair * LANES, (pair + 1) * LANES)
        vs = slice(hw + pair * LANES, hw + (pair + 1) * LANES)
        v2 = jnp.concatenate([kvp_ref[:, vs], kvc_ref[:, vs]], axis=0)
        outs, lses = [], []
        for hh in range(2):
            s = jnp.where(ok, scores[2 * pair + hh], -jnp.inf)
            m = jnp.max(s, axis=-1, keepdims=True)
            e = jnp.exp2(s - m)
            den = jnp.sum(e, axis=-1, keepdims=True)
            outs.append(_dot(e.astype(BF16), v2) * (1.0 / den))
            lses.append(jnp.log(den) + m * LN2)
        o_ref[:, cs] = jnp.where(lane < HEAD_DIM, outs[0], outs[1])
        lse_ref[:, cs] = jnp.where(lane < HEAD_DIM, lses[0], lses[1])


def dil_attend_prompt(q_r, kv_r):
    b, dil, n, _ = q_r.shape
    hw = DIL_HEADS * HEAD_DIM
    blk = lambda w, prev: pl.BlockSpec(
        (None, None, DIL_BAND, w), (lambda i, r, u: (i, r, jnp.maximum(u - 1, 0), 0)) if prev
        else (lambda i, r, u: (i, r, u, 0)))
    return pl.pallas_call(
        _dil_prompt_kernel,
        grid=(b, dil, n // DIL_BAND),
        in_specs=[blk(DIL_HEADS * LANES, False), blk(2 * hw, True), blk(2 * hw, False)],
        out_specs=[blk(hw, False), blk(hw, False)],
        out_shape=[jax.ShapeDtypeStruct((b, dil, n, hw), F32)] * 2,
        compiler_params=_cparams("parallel", "parallel", "parallel"),
        name="dil_attend_prompt",
    )(q_r, kv_r, kv_r)


TRANSPOSE_UNROLL = 8


def _page_copies(cache_ref, pt_ref, buf_ref, sem_ref, bi, slot, n_pages):
    return [pltpu.make_async_copy(cache_ref.at[pt_ref[bi, j], pl.ds(0, 2 * LANES), :],
                                  buf_ref.at[slot, j], sem_ref.at[slot]) for j in range(n_pages)]


def _compress_sample_kernel(pt_ref, cache_ref, pe_ref, wa_ref, wb_ref, o_ref, page_ref, kbuf_ref, vbuf_ref,
                            sem_ref, *, n_pages):
    i = pl.program_id(0)
    slot = i % 2
    nsub = n_pages * PAGE_SIZE // CMP_STRIDE
    copies = functools.partial(_page_copies, cache_ref, pt_ref, page_ref, sem_ref, n_pages=n_pages)

    @pl.when(i == 0)
    def _():
        for cp in copies(0, 0):
            cp.start()

    @pl.when(i + 1 < pl.num_programs(0))
    def _():
        for cp in copies(i + 1, 1 - slot):
            cp.start()

    for cp in copies(i, slot):
        cp.wait()

    def to_rows(jj, carry):
        for u in range(TRANSPOSE_UNROLL):
            j = jj * TRANSPOSE_UNROLL + u
            r0 = pl.multiple_of(j * PAGE_SIZE, PAGE_SIZE)
            kbuf_ref[pl.ds(r0, PAGE_SIZE), :] = page_ref[slot, j, 0:LANES, :].T
            vbuf_ref[pl.ds(r0, PAGE_SIZE), :] = page_ref[slot, j, LANES:2 * LANES, :].T
        return carry

    lax.fori_loop(0, n_pages // TRANSPOSE_UNROLL, to_rows, 0)
    o_ref[...] = _compress_rows(kbuf_ref, vbuf_ref, pe_ref, wa_ref, wb_ref, nsub).astype(o_ref.dtype)


def nsa_compress_sample(cache_t, page_table, pe4, wa, wb):
    bs, n_pages = page_table.shape
    past = n_pages * PAGE_SIZE
    nsub = past // CMP_STRIDE
    grid_spec = pltpu.PrefetchScalarGridSpec(
        num_scalar_prefetch=1,
        grid=(bs,),
        in_specs=[pl.BlockSpec(memory_space=pl.ANY),
                  pl.BlockSpec(pe4.shape, lambda i, pt: (0, 0)),
                  pl.BlockSpec(wa.shape, lambda i, pt: (0, 0, 0)),
                  pl.BlockSpec(wb.shape, lambda i, pt: (0, 0, 0))],
        out_specs=pl.BlockSpec((None, nsub, 2 * LANES), lambda i, pt: (i, 0, 0)),
        scratch_shapes=[pltpu.VMEM((2, n_pages, 2 * LANES, PAGE_SIZE), F32),
                        pltpu.VMEM((past, LANES), F32), pltpu.VMEM((past, LANES), F32),
                        pltpu.SemaphoreType.DMA((2,))],
    )
    return pl.pallas_call(
        functools.partial(_compress_sample_kernel, n_pages=n_pages),
        grid_spec=grid_spec,
        out_shape=jax.ShapeDtypeStruct((bs, nsub, 2 * LANES), BF16),
        compiler_params=_cparams("arbitrary"),
        name="nsa_compress_sample",
    )(page_table, cache_t, pe4, wa, wb)


def _group_sum_rows(x):
    parts = [jnp.sum(x[NSA_GROUP * k:NSA_GROUP * (k + 1)], axis=0, keepdims=True) for k in range(NSA_KV_HEADS)]
    return _pad_rows(jnp.concatenate(parts, axis=0), x.shape[0])


def _nsa_sample_cmp_kernel(q_ref, cmp_ref, selt_ref, ocmp_ref, imp_ref, *, qpos):
    q8 = q_ref[...]
    ncmp = cmp_ref.shape[0]
    s = _dot_nt(q8, cmp_ref[:, 0:LANES])
    cmp_end = lax.broadcasted_iota(jnp.int32, (1, ncmp), 1) * CMP_STRIDE + (CMP_LEN - 1)
    p = _softmax_rows(s, cmp_end <= qpos)
    ocmp_ref[...] = _dot(p.astype(BF16), cmp_ref[:, LANES:2 * LANES])
    hi, mid, lo = _split3(_group_sum_rows(p))
    selt = selt_ref[...]
    imp_ref[...] = _dot_nt(hi, selt) + _dot_nt(mid, selt) + _dot_nt(lo, selt)


def nsa_sample_cmp(q8, cmp_kv, selt, qpos):
    bs = q8.shape[0]
    ncmp = cmp_kv.shape[1]
    blk = pl.BlockSpec((None, NSA_HEADS, LANES), lambda i: (i, 0, 0))
    return pl.pallas_call(
        functools.partial(_nsa_sample_cmp_kernel, qpos=qpos),
        grid=(bs,),
        in_specs=[blk, pl.BlockSpec((None, ncmp, 2 * LANES), lambda i: (i, 0, 0)), _const_spec(selt.shape)],
        out_specs=[blk, blk],
        out_shape=[jax.ShapeDtypeStruct((bs, NSA_HEADS, LANES), F32)] * 2,
        compiler_params=_cparams("parallel"),
        name="nsa_sample_cmp",
    )(q8, cmp_kv, selt)


def _topk_lanes_kernel(imp_ref, idx_ref, *, k):
    score = imp_ref[...]
    lane = lax.broadcasted_iota(jnp.int32, score.shape, 1)
    lane_f = lane.astype(F32)
    score = jnp.where(lane == 0, FORCED_BLOCK_SCORE, score)
    idx = jnp.zeros(score.shape, F32)
    for r in range(k):
        m = jnp.max(score, axis=-1, keepdims=True)
        jmin = jnp.min(jnp.where(score == m, lane_f, float(LANES)), axis=-1, keepdims=True)
        idx = jnp.where(lane == r, jmin, idx)
        score = jnp.where(lane_f == jmin, -jnp.inf, score)
    idx_ref[...] = idx.astype(jnp.int32)


def topk_lanes(imp, k):
    bs = imp.shape[0]
    x = imp.reshape(bs * NSA_HEADS, LANES)
    out = pl.pallas_call(
        functools.partial(_topk_lanes_kernel, k=k),
        out_shape=jax.ShapeDtypeStruct(x.shape, jnp.int32),
        name="topk_lanes",
    )(x)
    return out.reshape(bs, NSA_HEADS, LANES)


N_SEL_CACHE = N_SEL - 1


def _sel_copies(cache_ref, pt_ref, sel_ref, buf_ref, sem_ref, bi, slot):
    cps = []
    for kvh in range(NSA_KV_HEADS):
        for r in range(N_SEL_CACHE):
            j = sel_ref[bi, kvh * N_SEL_CACHE + r]
            cps.append(pltpu.make_async_copy(
                cache_ref.at[pt_ref[bi, j // 2], pl.ds(2 * LANES, 2 * LANES), :],
                buf_ref.at[slot, kvh, r], sem_ref.at[slot]))
    return cps


def _pick_gate(gates8, branch):
    row = lax.broadcasted_iota(jnp.int32, gates8.shape, 0)
    lane = lax.broadcasted_iota(jnp.int32, gates8.shape, 1)
    return jnp.sum(jnp.where(lane == 3 * row + branch, gates8, 0.0), axis=-1, keepdims=True)


def _nsa_sample_attend_kernel(pt_ref, sel_ref, q_ref, gate_ref, ocmp_ref, new_ref, win_ref, cache_ref,
                              o_ref, buf_ref, sem_ref):
    i = pl.program_id(0)
    slot = i % 2
    copies = functools.partial(_sel_copies, cache_ref, pt_ref, sel_ref, buf_ref, sem_ref)

    @pl.when(i == 0)
    def _():
        for cp in copies(0, 0):
            cp.start()

    @pl.when(i + 1 < pl.num_programs(0))
    def _():
        for cp in copies(i + 1, 1 - slot):
            cp.start()

    q8 = q_ref[...]
    q8f = q8.astype(F32)
    row = lax.broadcasted_iota(jnp.int32, (NSA_HEADS, 1), 0)
    new = new_ref[...]
    rnd = lambda x: x.astype(BF16).astype(F32)

    def probs_with_new_key(s, k_new, mask):
        s_new = jnp.sum(q8f * rnd(k_new), axis=-1, keepdims=True)
        s = jnp.where(mask, s, -jnp.inf)
        m = jnp.maximum(jnp.max(s, axis=-1, keepdims=True), s_new)
        e = jnp.exp(s - m)
        e_new = jnp.exp(s_new - m)
        den = jnp.sum(e, axis=-1, keepdims=True) + e_new
        return (e / den).astype(BF16), rnd(e_new / den)

    wb = win_ref.shape[1]
    widx = lax.broadcasted_iota(jnp.int32, (1, wb), 1)
    pw, pw_new = probs_with_new_key(_dot(q8, win_ref[0:LANES, :].astype(BF16)), new[:, 4 * LANES:5 * LANES],
                                    widx > wb - NSA_WINDOW)
    o_win = _dot_nt(pw, win_ref[LANES:2 * LANES, :].astype(BF16)) + pw_new * rnd(new[:, 5 * LANES:6 * LANES])
    for cp in copies(i, slot):
        cp.wait()
    lane = lax.broadcasted_iota(jnp.int32, (1, PAGE_SIZE), 1)
    o_sel = []
    for kvh in range(NSA_KV_HEADS):
        ss, masks = [], []
        for r in range(N_SEL_CACHE):
            ss.append(_dot(q8, buf_ref[slot, kvh, r, 0:LANES, :].astype(BF16)))
            masks.append(lane // SLC_BLOCK == sel_ref[i, kvh * N_SEL_CACHE + r] % 2)
        p, p_new = probs_with_new_key(jnp.concatenate(ss, axis=1), new[:, 2 * LANES:3 * LANES],
                                      jnp.concatenate(masks, axis=1))
        o = p_new * rnd(new[:, 3 * LANES:4 * LANES])
        for r in range(N_SEL_CACHE):
            o = o + _dot_nt(p[:, r * PAGE_SIZE:(r + 1) * PAGE_SIZE],
                            buf_ref[slot, kvh, r, LANES:2 * LANES, :].astype(BF16))
        o_sel.append(o)
    o_sel = jnp.where(row < NSA_GROUP, o_sel[0], o_sel[1])
    gates8 = jnp.broadcast_to(jax.nn.sigmoid(gate_ref[...]), (NSA_HEADS, LANES))
    o_ref[...] = (_pick_gate(gates8, 0) * ocmp_ref[...] + _pick_gate(gates8, 1) * o_sel
                  + _pick_gate(gates8, 2) * o_win)


def nsa_sample_attend(page_table, sel_idx, q8, gate, o_cmp, new_rows, win_t, cache_t):
    bs = q8.shape[0]
    wb = win_t.shape[2]
    blk = lambda w: pl.BlockSpec((None, NSA_HEADS, w), lambda i, pt, sel: (i, 0, 0))
    one = lambda w: pl.BlockSpec((None, 1, w), lambda i, pt, sel: (i, 0, 0))
    grid_spec = pltpu.PrefetchScalarGridSpec(
        num_scalar_prefetch=2,
        grid=(bs,),
        in_specs=[blk(LANES), one(LANES), blk(LANES), one(new_rows.shape[-1]),
                  pl.BlockSpec((None, 2 * LANES, wb), lambda i, pt, sel: (i, 0, 0)),
                  pl.BlockSpec(memory_space=pl.ANY)],
        out_specs=blk(LANES),
        scratch_shapes=[pltpu.VMEM((2, NSA_KV_HEADS, N_SEL_CACHE, 2 * LANES, PAGE_SIZE), F32),
                        pltpu.SemaphoreType.DMA((2,))],
    )
    return pl.pallas_call(
        _nsa_sample_attend_kernel,
        grid_spec=grid_spec,
        out_shape=jax.ShapeDtypeStruct((bs, NSA_HEADS, LANES), F32),
        compiler_params=_cparams("arbitrary"),
        name="nsa_sample_attend",
    )(page_table, sel_idx, q8, gate, o_cmp, new_rows, win_t, cache_t)


DIFF_PAGES_PER_STEP = 16
DIFF_ROW_STRIDE = 2 * DIFF_HEADS


def _diff_sample_kernel(pt_ref, qt_ref, new_ref, lam_ref, ng_ref, *rest, lam_init):
    pages = rest[:DIFF_PAGES_PER_STEP]
    o_ref, m_ref, l_ref, acc_ref = rest[DIFF_PAGES_PER_STEP:]
    c = pl.program_id(1)
    hw = DIFF_HEADS * LANES
    qt = qt_ref[...]
    rnd = lambda x: x.astype(BF16).astype(F32)

    def heads(page, which):
        return jnp.concatenate(
            [page[pl.ds(which * DIFF_HEADS + h, PAGE_SIZE, stride=DIFF_ROW_STRIDE), :] for h in range(DIFF_HEADS)],
            axis=1).astype(BF16)

    @pl.when(c == 0)
    def _():
        s_new = jnp.sum(qt.astype(F32) * rnd(new_ref[:, 0:hw]), axis=-1, keepdims=True)
        m_ref[...] = jnp.broadcast_to(s_new, m_ref.shape)
        l_ref[...] = jnp.ones(l_ref.shape, F32)
        acc_ref[...] = jnp.broadcast_to(rnd(new_ref[:, hw:2 * hw]), acc_ref.shape)

    s = jnp.concatenate([_dot_nt(qt, heads(page, 0)) for page in pages], axis=1)
    m_old = m_ref[:, 0:1]
    m_new = jnp.maximum(m_old, jnp.max(s, axis=-1, keepdims=True))
    alpha = jnp.exp(m_old - m_new)
    e = jnp.exp(s - m_new)
    l = l_ref[:, 0:1] * alpha + jnp.sum(e, axis=-1, keepdims=True)
    acc = acc_ref[...] * alpha
    for i, page in enumerate(pages):
        acc = acc + _dot(e[:, i * PAGE_SIZE:(i + 1) * PAGE_SIZE].astype(BF16), heads(page, 1))
    m_ref[...] = jnp.broadcast_to(m_new, m_ref.shape)
    l_ref[...] = jnp.broadcast_to(l, l_ref.shape)
    acc_ref[...] = acc

    @pl.when(c == pl.num_programs(1) - 1)
    def _():
        o = acc / l
        lam = _diff_lambda(lam_ref, lam_init)
        for h in range(DIFF_HEADS):
            cs = slice(h * LANES, (h + 1) * LANES)
            oh = o[h:h + 1, cs] - lam * o[DIFF_HEADS + h:DIFF_HEADS + h + 1, cs]
            o_ref[:, cs] = _rms(oh, ng_ref[...], DIFF_NORM_EPS) * (1.0 - lam_init)


def diff_attend_sample(page_table, qt, new_rows, lam_vec, norm_g, cache_v, lam_init):
    bs, n_pages = page_table.shape
    p = DIFF_PAGES_PER_STEP
    hw = DIFF_HEADS * LANES
    page_specs = [pl.BlockSpec((None, PAGE_SIZE * DIFF_ROW_STRIDE, LANES),
                               functools.partial(lambda i, c, pt, k: (pt[i, c * p + k], 0, 0), k=k))
                  for k in range(p)]
    grid_spec = pltpu.PrefetchScalarGridSpec(
        num_scalar_prefetch=1,
        grid=(bs, n_pages // p),
        in_specs=[pl.BlockSpec((None, 2 * DIFF_HEADS, hw), lambda i, c, pt: (i, 0, 0)),
                  pl.BlockSpec((None, 1, 2 * hw), lambda i, c, pt: (i, 0, 0)),
                  pl.BlockSpec(lam_vec.shape, lambda i, c, pt: (0, 0)),
                  pl.BlockSpec((1, LANES), lambda i, c, pt: (0, 0))] + page_specs,
        out_specs=pl.BlockSpec((None, 1, hw), lambda i, c, pt: (i, 0, 0)),
        scratch_shapes=[pltpu.VMEM((2 * DIFF_HEADS, LANES), F32), pltpu.VMEM((2 * DIFF_HEADS, LANES), F32),
                        pltpu.VMEM((2 * DIFF_HEADS, hw), F32)],
    )
    return pl.pallas_call(
        functools.partial(_diff_sample_kernel, lam_init=lam_init),
        grid_spec=grid_spec,
        out_shape=jax.ShapeDtypeStruct((bs, 1, hw), F32),
        compiler_params=_cparams("parallel", "arbitrary"),
        name="diff_attend_sample",
    )(page_table, qt, new_rows, lam_vec, norm_g.reshape(1, LANES), *([cache_v] * p))


def _col_rep(row):
    x = jnp.broadcast_to(row, (LANES, row.shape[1]))
    return jnp.concatenate([x[:, c * LANES:(c + 1) * LANES].T for c in range(row.shape[1] // LANES)], axis=0)


def _head_sum(x):
    return jnp.sum(x.reshape(DIL_HEADS, HEAD_DIM, x.shape[1]), axis=1)


def _head_expand(x):
    return jnp.broadcast_to(x[:, None, :], (DIL_HEADS, HEAD_DIM, x.shape[1])).reshape(
        DIL_HEADS * HEAD_DIM, x.shape[1])


STATE_SHIFT_ROWS = 256


def _shift_in(st_ref, new_col, out_ref):
    n_rows, width = st_ref.shape
    lane = lax.broadcasted_iota(jnp.int32, (STATE_SHIFT_ROWS, LANES), 1)
    for r0 in range(0, n_rows, STATE_SHIFT_ROWS):
        rs = slice(r0, r0 + STATE_SHIFT_ROWS)
        rolled = pltpu.roll(st_ref[rs, :], width - 1, 1)
        if width > LANES:
            out_ref[rs, 0:width - LANES] = rolled[:, 0:width - LANES]
        out_ref[rs, width - LANES:width] = jnp.where(lane == LANES - 1, new_col[rs, :], rolled[:, width - LANES:width])


def _dil_sample_kernel(q_ref, new0_ref, new1_ref, new2_ref, st0_ref, st1_ref, st2_ref, o_ref,
                       nst0_ref, nst1_ref, nst2_ref):
    hw = DIL_HEADS * HEAD_DIM
    outs, lses = [], []
    for g, (new_ref, st_ref, nst_ref) in enumerate(((new0_ref, st0_ref, nst0_ref), (new1_ref, st1_ref, nst1_ref),
                                                    (new2_ref, st2_ref, nst2_ref))):
        win, dil = DIL_GROUPS[g]
        qc = _col_rep(q_ref[:, g * hw:(g + 1) * hw])
        kn = _col_rep(new_ref[:, 0:hw])
        vn = _col_rep(new_ref[:, hw:2 * hw])
        _shift_in(st_ref, jnp.concatenate([kn, vn], axis=0), nst_ref)
        s_new = _head_sum(qc * kn)[:, 0:1]
        n_chunks = win // LANES
        s = jnp.concatenate([_head_sum(st_ref[0:hw, c * LANES:(c + 1) * LANES] * qc) for c in range(n_chunks)],
                            axis=1)
        lane = lax.broadcasted_iota(jnp.int32, s.shape, 1)
        s = jnp.where(lane % dil == 0, s, -jnp.inf)
        m = jnp.maximum(jnp.max(s, axis=-1, keepdims=True), s_new)
        e = jnp.exp(s - m)
        e_new = jnp.exp(s_new - m)
        den = jnp.sum(e, axis=-1, keepdims=True) + e_new
        p = e / den
        acc = _head_expand(jnp.broadcast_to(e_new / den, (DIL_HEADS, LANES))) * vn * (1.0 / LANES)
        for c in range(n_chunks):
            cs = slice(c * LANES, (c + 1) * LANES)
            acc = acc + st_ref[hw:2 * hw, cs] * _head_expand(p[:, cs])
        outs.append(jnp.sum(acc, axis=-1, keepdims=True))
        lses.append(jnp.log(den) + m)
    mx = jnp.maximum(jnp.maximum(lses[0], lses[1]), lses[2])
    es = [jnp.exp(l - mx) for l in lses]
    tot = es[0] + es[1] + es[2]
    mix = jnp.zeros((hw, LANES), F32)
    for g in range(len(DIL_GROUPS)):
        alpha = _head_expand(jnp.broadcast_to(es[g] / tot, (DIL_HEADS, LANES)))
        mix = mix + alpha * jnp.broadcast_to(outs[g], (hw, LANES))
    rows = jnp.concatenate([mix[c * LANES:(c + 1) * LANES, :].T for c in range(hw // LANES)], axis=1)
    o_ref[...] = rows[0:1]


def dil_attend_sample(q, news, states_t):
    bs = q.shape[0]
    hw = DIL_HEADS * HEAD_DIM
    st_specs = [pl.BlockSpec((None, 2 * hw, st.shape[2]), lambda i: (i, 0, 0)) for st in states_t]
    outs = pl.pallas_call(
        _dil_sample_kernel,
        grid=(bs,),
        in_specs=[pl.BlockSpec((None, 1, q.shape[2]), lambda i: (i, 0, 0))]
                 + [pl.BlockSpec((None, 1, 2 * hw), lambda i: (i, 0, 0))] * 3 + st_specs,
        out_specs=[pl.BlockSpec((None, 1, hw), lambda i: (i, 0, 0))] + st_specs,
        out_shape=[jax.ShapeDtypeStruct((bs, 1, hw), F32)]
                  + [jax.ShapeDtypeStruct(st.shape, F32) for st in states_t],
        compiler_params=_cparams("parallel"),
        name="dil_attend_sample",
    )(q, *news, *states_t)
    return outs[0], outs[1:]


AB_SIZES = (NSA_HEADS * HEAD_DIM, 6 * NSA_KV_HEADS * HEAD_DIM, 3 * NSA_HEADS,
            DIFF_HEADS * 2 * HEAD_DIM, DIFF_HEADS * 2 * HEAD_DIM, DIFF_HEADS * 2 * HEAD_DIM)
_QSCALE = HEAD_DIM ** -0.5
_QSCALE2 = _QSCALE * LOG2E


def _rope_tables(pos):
    half = HEAD_DIM // 2
    inv = ROPE_THETA ** (-jnp.arange(half, dtype=F32) / half)
    ang = pos.astype(F32)[:, None] * inv[None, :]
    c, s = jnp.cos(ang), jnp.sin(ang)
    return jnp.tile(c, (1, 4)), jnp.tile(jnp.concatenate([-s, s], axis=1), (1, 2))


def _prep_w_ab(w):
    d = w.shape[0]
    qa, kvb, gl, qd, kd, vd = jnp.split(w, np.cumsum(AB_SIZES)[:-1].tolist(), axis=1)
    kvb = kvb.reshape(d, 6, LANES)
    k3 = kvb[:, 0::2].reshape(d, 3 * LANES)
    v3 = kvb[:, 1::2].reshape(d, 3 * LANES)
    gl = jnp.pad(gl, ((0, 0), (0, LANES - gl.shape[1])))
    return jnp.concatenate([qa, k3, v3, qd, kd, vd, gl], axis=1).astype(BF16)


def _ab_plan(dest, qscale):
    return (
        (0, 4, True, qscale, "half", tuple(dest("qa", j) for j in range(8))),
        (512, 3, True, 1.0, None, (dest("k_cmp", 0), dest("k_slc", 0), dest("k_win", 0))),
        (896, 3, False, 1.0, None, (dest("v_cmp", 0), dest("v_slc", 0), dest("v_win", 0))),
        (1280, 4, True, qscale, "pair", tuple(dest("qd", j) for j in range(8))),
        (1792, 4, True, 1.0, None, tuple(dest("kd", j) for j in range(4))),
        (2304, 4, False, 1.0, None, tuple(dest("vd", j) for j in range(4))),
        (2816, 1, False, 1.0, None, (dest("gate", 0),)),
    )


def _ab_prompt_defs(t):
    defs = (("T", 1024, BF16, t), ("T", 512, F32, t), ("N", 128, BF16), ("N", 256, F32),
            ("T", 256, F32, min(NSA_WINDOW, t)), ("N", 128, BF16), ("T", 1024, BF16, t), ("I", 8, F32),
            ("N", 512, BF16), ("T", 128, F32, t), ("T", 128, BF16, t), ("T", 128, BF16, t), ("T", 512, BF16, t))
    table = {
        "qa": lambda j: ((0, j),), "qd": lambda j: ((6, j),), "gate": lambda j: ((9, 0),),
        "k_cmp": lambda j: ((1, 0), (3, 0)), "v_cmp": lambda j: ((1, 1), (3, 1)),
        "k_slc": lambda j: ((1, 2), (2, 0)), "v_slc": lambda j: ((1, 3), (11, 0)),
        "k_win": lambda j: ((4, 0), (5, 0)), "v_win": lambda j: ((4, 1), (10, 0)),
        "kd": lambda j: ((7, j), (8, j)), "vd": lambda j: ((7, 4 + j), (12, j)),
    }
    return defs, _ab_plan(lambda name, j: table[name](j), _QSCALE2)


def _ab_sample_defs():
    defs = (("N", 1024, BF16), ("N", 512, F32), ("N", 256, F32), ("N", 1024, BF16), ("N", 1024, F32),
            ("N", 128, F32))
    table = {
        "qa": lambda j: ((0, j),), "qd": lambda j: ((3, j),), "gate": lambda j: ((5, 0),),
        "k_cmp": lambda j: ((1, 0),), "v_cmp": lambda j: ((1, 1),),
        "k_slc": lambda j: ((1, 2),), "v_slc": lambda j: ((1, 3),),
        "k_win": lambda j: ((2, 0),), "v_win": lambda j: ((2, 1),),
        "kd": lambda j: ((4, j),), "vd": lambda j: ((4, 4 + j),),
    }
    return defs, _ab_plan(lambda name, j: table[name](j), _QSCALE)


def _c_prompt_defs(t):
    defs, plan = [], []
    for g, (win, dil) in enumerate(DIL_GROUPS):
        defs += [("R", 1024, BF16, dil), ("R", 1024, BF16, dil), ("T", 1024, F32, min(win, t))]
        plan += [
            (g * 1536, 4, True, _QSCALE2, "pair", tuple(((3 * g, j),) for j in range(8))),
            (g * 1536 + 512, 4, True, 1.0, None, tuple(((3 * g + 1, j), (3 * g + 2, j)) for j in range(4))),
            (g * 1536 + 1024, 4, False, 1.0, None,
             tuple(((3 * g + 1, 4 + j), (3 * g + 2, 4 + j)) for j in range(4))),
        ]
    return tuple(defs), tuple(plan)


def _c_sample_defs():
    defs = (("N", 1536, F32), ("N", 1024, F32), ("N", 1024, F32), ("N", 1024, F32))
    plan = []
    for g in range(len(DIL_GROUPS)):
        plan += [
            (g * 1536, 4, True, _QSCALE, None, tuple(((0, 4 * g + j),) for j in range(4))),
            (g * 1536 + 512, 4, True, 1.0, None, tuple(((1 + g, j),) for j in range(4))),
            (g * 1536 + 1024, 4, False, 1.0, None, tuple(((1 + g, 4 + j),) for j in range(4))),
        ]
    return defs, tuple(plan)


def _prep_cmp(w_cmp, pe_cmp):
    wk, wv = w_cmp[0], w_cmp[1]
    z = jnp.zeros_like(wk)
    w4 = jnp.concatenate([jnp.concatenate([wk, z, z, z], axis=-1), jnp.concatenate([z, wk, z, z], axis=-1),
                          jnp.concatenate([z, z, wv, z], axis=-1), jnp.concatenate([z, z, z, wv], axis=-1)],
                         axis=1).astype(BF16)
    pe4 = jnp.concatenate([pe_cmp[0], pe_cmp[0], pe_cmp[1], pe_cmp[1]], axis=-1)
    half = CMP_LEN // 2
    return pe4, w4[:half], w4[half:]


def _block_indicator(n_keys):
    return (jnp.arange(n_keys)[:, None] // SLC_BLOCK == jnp.arange(LANES)[None, :]).astype(BF16)


def _cmp_to_block(n_cmp):
    r = SLC_BLOCK // CMP_STRIDE
    return (jnp.arange(n_cmp)[None, :] // r == jnp.arange(LANES)[:, None]).astype(BF16)


def _rows_from_t(x_t, lead):
    b, _, r = x_t.shape
    nd = len(lead)
    return x_t.reshape((b,) + tuple(lead) + (HEAD_DIM, r)).transpose((0, nd + 2) + tuple(range(1, nd + 2)))


def _rows_to_t(x):
    b, r = x.shape[:2]
    nd = x.ndim
    return x.transpose((0,) + tuple(range(2, nd)) + (1,)).reshape(b, -1, r)


def ab_mix_prompt(h, b, t, g_in, w_ab, cmp_prep, lam_vec, dn_g, lam_init, cos, sin):
    defs, plan = _ab_prompt_defs(t)
    (q_nsa_t, rows_nsa_t, kslc16, cmp_rows, rows_win_t, kwin16, q_diff_t, rows_diff, k_diff16, gate_t, vwin_t,
     vslc_t, v_diff_t) = project(h, b, g_in, cos, sin, w_ab, plan, defs)
    r3 = lambda x: x.reshape(b, t, x.shape[-1])
    pe4, wa, wb = cmp_prep
    kcmp16, vcmp_t = nsa_compress_prompt(r3(cmp_rows), pe4, wa, wb)
    o_nsa = nsa_attend_prompt(q_nsa_t, gate_t, kcmp16, vcmp_t, r3(kslc16), vslc_t, r3(kwin16), vwin_t,
                              _block_indicator(t), _cmp_to_block(t // CMP_STRIDE))
    o_diff = diff_attend_prompt(q_diff_t, r3(k_diff16), v_diff_t, lam_vec, dn_g, lam_init)
    mixed = [o_nsa.reshape(b * t, -1), o_diff.reshape(b * t, -1)]
    return mixed, rows_nsa_t, rows_win_t, rows_diff.reshape(b, t, 2, DIFF_HEADS, 2 * HEAD_DIM)


def dil_mix_prompt(h, b, t, g_in, w_c, cos, sin):
    defs, plan = _c_prompt_defs(t)
    outs = project(h, b, g_in, cos, sin, w_c, plan, defs)
    os_, lses, rows_t = [], [], []
    for gi in range(len(DIL_GROUPS)):
        o, lse = dil_attend_prompt(outs[3 * gi], outs[3 * gi + 1])
        os_.append(o)
        lses.append(lse)
        rows_t.append(outs[3 * gi + 2])
    return os_, lses, rows_t


def _diff_qt(q_diff):
    bs = q_diff.shape[0]
    qd = q_diff.reshape(bs, DIFF_HEADS, 2, LANES).transpose(0, 2, 1, 3)
    eye = jnp.eye(DIFF_HEADS, dtype=q_diff.dtype)
    return (qd[:, :, :, None, :] * eye[None, None, :, :, None]).reshape(bs, 2 * DIFF_HEADS, DIFF_HEADS * LANES)


def ab_mix_sample(hs, g_in, w_ab, cmp_prep, lam_vec, dn_g, lam_init, cos, sin,
                  cache_nsa, cache_diff, win_state, page_table):
    bs = hs.shape[0]
    defs, plan = _ab_sample_defs()
    q_nsa, rows_nsa, rows_win, q_diff, rows_diff, gate = project(hs, 1, g_in, cos, sin, w_ab, plan, defs)
    n_pages = page_table.shape[1]
    past = n_pages * PAGE_SIZE
    assert past // SLC_BLOCK == LANES, "selection-block axis is laid out on the 128 lanes"
    n_pool = cache_nsa.shape[0]
    cache_t = _rows_to_t(cache_nsa)
    cache_v = cache_diff.reshape(n_pool, PAGE_SIZE * DIFF_ROW_STRIDE, LANES)
    pe4, wa, wb = cmp_prep
    cmp_kv = nsa_compress_sample(cache_t, page_table, pe4, wa, wb)
    q8 = q_nsa.reshape(bs, NSA_HEADS, LANES)
    o_cmp, imp = nsa_sample_cmp(q8, cmp_kv, _cmp_to_block(past // CMP_STRIDE), past)
    sel_idx = topk_lanes(imp, N_SEL_CACHE)[:, :NSA_KV_HEADS, :N_SEL_CACHE].reshape(bs, -1)
    new_rows = jnp.concatenate([rows_nsa, rows_win], axis=-1).reshape(bs, 1, -1)
    o8 = nsa_sample_attend(page_table, sel_idx, q8, gate.reshape(bs, 1, LANES), o_cmp, new_rows,
                           _rows_to_t(win_state), cache_t)
    o8 = o8.reshape(bs, NSA_HEADS, 2, HEAD_DIM)
    o_nsa = jnp.concatenate([o8[:, :NSA_GROUP, 0], o8[:, NSA_GROUP:, 1]], axis=1).reshape(bs, -1)
    o_diff = diff_attend_sample(page_table, _diff_qt(q_diff), rows_diff.reshape(bs, 1, -1), lam_vec, dn_g,
                                cache_v, lam_init).reshape(bs, -1)
    mixed = jnp.concatenate([o_nsa, o_diff], axis=-1).astype(BF16)
    return mixed, rows_nsa, rows_win, rows_diff


def dil_mix_sample(hs, g_in, w_c_rows, cos, sin, states):
    bs = hs.shape[0]
    defs, plan = _c_sample_defs()
    outs = project(hs, 1, g_in, cos, sin, w_c_rows, plan, defs)
    news = [x.reshape(bs, 1, -1) for x in outs[1:]]
    sts = []
    for (win, dil), st in zip(DIL_GROUPS, states):
        assert st.shape[1] == win and win == DIL_BAND * dil, "state buffer must hold the full dilated window"
        sts.append(_rows_to_t(st))
    o, new_sts = dil_attend_sample(outs[0].reshape(bs, 1, -1), news, sts)
    return o.reshape(bs, -1).astype(BF16), [_rows_from_t(x, (2, DIL_HEADS)) for x in new_sts]


def kernel(x_prompt, x_sample, cache_nsa, cache_diff, state_nsa_win, state_dil_0, state_dil_1, state_dil_2,
           page_table, norm_g, ffn_w_in, ffn_w_out, w_in_ab, w_out_ab, nsa_w_cmp, nsa_pe_cmp, diff_lambda,
           diff_norm_g, w_in_c, w_out_c):
    b, t, d = x_prompt.shape
    bs, ns, _ = x_sample.shape
    assert ns == 1, "sample group is one new token per sequence"
    depth = norm_g.shape[0]
    past = page_table.shape[1] * PAGE_SIZE
    hp = x_prompt.reshape(b * t, d)
    hs = x_sample.reshape(bs, d)
    cos_p, sin_p = _rope_tables(jnp.tile(jnp.arange(t, dtype=jnp.int32), b))
    cos_s, sin_s = _rope_tables(jnp.full((bs,), past, jnp.int32))
    w_ffn_in = ffn_w_in
    w_ffn_out = ffn_w_out
    state_dil = (state_dil_0, state_dil_1, state_dil_2)
    nsa_p, nsa_s, win_p, win_s, diff_p, diff_s = [], [], [], [], [], []
    dil_p = [[] for _ in DIL_GROUPS]
    dil_s = [[] for _ in DIL_GROUPS]
    for layer in range(depth):
        g = norm_g[layer]
        hp = ffn_half(hp, g[0], g[1], w_ffn_in[layer, 0], w_ffn_out[layer, 0])
        hs = ffn_half(hs, g[0], g[1], w_ffn_in[layer, 0], w_ffn_out[layer, 0])
        if layer % 2 == 0:
            e = layer // 2
            lam_init = 0.8 - 0.6 * math.exp(-0.3 * layer)
            w_ab = _prep_w_ab(w_in_ab[e])
            cmp_prep = _prep_cmp(nsa_w_cmp[e], nsa_pe_cmp[e])
            mp, rn_t, rw_t, rd = ab_mix_prompt(hp, b, t, g[2], w_ab, cmp_prep, diff_lambda[e], diff_norm_g[e],
                                               lam_init, cos_p, sin_p)
            nsa_p.append(_rows_from_t(rn_t, (4, NSA_KV_HEADS)))
            win_p.append(_rows_from_t(rw_t[:, :, -min(NSA_WINDOW, t):], (2, NSA_KV_HEADS)))
            diff_p.append(rd.reshape(b, t, 2, DIFF_HEADS, 2 * HEAD_DIM))
            ms, rn, rw, rd = ab_mix_sample(hs, g[2], w_ab, cmp_prep, diff_lambda[e], diff_norm_g[e], lam_init,
                                           cos_s, sin_s, cache_nsa[e], cache_diff[e], state_nsa_win[e], page_table)
            nsa_s.append(rn.reshape(bs, 1, 4, NSA_KV_HEADS, HEAD_DIM))
            win_full = jnp.concatenate([state_nsa_win[e], rw.reshape(bs, 1, 2, NSA_KV_HEADS, HEAD_DIM)], axis=1)
            win_s.append(win_full[:, -min(NSA_WINDOW, win_full.shape[1]):])
            diff_s.append(rd.reshape(bs, 1, 2, DIFF_HEADS, 2 * HEAD_DIM))
            w_o = w_out_ab[e].astype(BF16)
            hp = outproj(hp, mp, w_o, g[3])
            hs = outproj(hs, [ms], w_o, g[3])
        else:
            o = layer // 2
            w_o = w_out_c[o].astype(BF16)
            w_c = w_in_c[o].astype(BF16)
            os_, lses, rows_t = dil_mix_prompt(hp, b, t, g[2], w_c, cos_p, sin_p)
            hp = outproj_dil(hp, b, os_, lses, w_o, g[3])
            ms, new_states = dil_mix_sample(hs, g[2], w_c, cos_s, sin_s, [st[o] for st in state_dil])
            hs = outproj(hs, [ms], w_o, g[3])
            for gi, (win, dil) in enumerate(DIL_GROUPS):
                dil_p[gi].append(_rows_from_t(rows_t[gi][:, :, -min(win, t):], (2, DIL_HEADS)))
                dil_s[gi].append(new_states[gi])
        hp = ffn_half(hp, g[4], g[5], w_ffn_in[layer, 1], w_ffn_out[layer, 1])
        hs = ffn_half(hs, g[4], g[5], w_ffn_in[layer, 1], w_ffn_out[layer, 1])
    return (hp.reshape(b, t, d), hs.reshape(bs, 1, d), jnp.stack(nsa_p), jnp.stack(nsa_s), jnp.stack(win_p),
            jnp.stack(win_s), jnp.stack(diff_p), jnp.stack(diff_s), jnp.stack(dil_p[0]), jnp.stack(dil_s[0]),
            jnp.stack(dil_p[1]), jnp.stack(dil_s[1]), jnp.stack(dil_p[2]), jnp.stack(dil_s[2]))
```

```python
import functools
import math

import jax
import jax.numpy as jnp
import numpy as np
from jax import lax
from jax.experimental import pallas as pl
from jax.experimental.pallas import tpu as pltpu

F32 = jnp.float32
BF16 = jnp.bfloat16

LANES = 128
SUBLANES = 8
HEAD_DIM = 64
ROPE_THETA = 10000.0
NORM_EPS = 1e-6
PAGE_SIZE = 128
NSA_HEADS = 8
NSA_KV_HEADS = 2
NSA_GROUP = NSA_HEADS // NSA_KV_HEADS
CMP_LEN = 32
CMP_STRIDE = 16
SLC_BLOCK = 64
N_SEL = 16
NSA_WINDOW = 512
FORCED_BLOCK_SCORE = 1.0e4
DIFF_HEADS = 4
DIFF_NORM_EPS = 1e-5
DIL_GROUPS = ((128, 1), (512, 4), (2048, 16))
DIL_HEADS = 8
DIL_BAND = 128
NEG_BIG = -1.0e30
LN2 = math.log(2.0)
LOG2E = 1.0 / LN2
VMEM_LIMIT_BYTES = 56 * 1024 * 1024
TOKEN_TILE = 512


def _cparams(*sem):
    return pltpu.CompilerParams(dimension_semantics=sem, vmem_limit_bytes=VMEM_LIMIT_BYTES)


def _const_spec(shape):
    nd = len(shape)
    return pl.BlockSpec(shape, lambda *_: (0,) * nd, pipeline_mode=pl.Buffered(1))


def _rms(x, g, eps):
    return x * lax.rsqrt(jnp.mean(x * x, axis=-1, keepdims=True) + eps) * g


def _dot(a, b):
    return jnp.dot(a, b, preferred_element_type=F32)


def _dot_nt(a, b):
    return lax.dot_general(a, b, (((1,), (1,)), ((), ())), preferred_element_type=F32)


def _split3(x):
    hi = x.astype(BF16)
    r1 = x - hi.astype(F32)
    mid = r1.astype(BF16)
    lo = (r1 - mid.astype(F32)).astype(BF16)
    return hi, mid, lo


def _pad_rows(x, rows):
    return jnp.concatenate([x, jnp.zeros((rows - x.shape[0], x.shape[1]), x.dtype)], axis=0)


def _token_tile(n):
    return TOKEN_TILE if n % TOKEN_TILE == 0 else n


FFN_CHUNK = 256


def _ffn_kernel(x_ref, gpre_ref, gpost_ref, win_ref, wout_ref, o_ref, *, d_ff):
    x = x_ref[...]
    xn = _rms(x, gpre_ref[...], NORM_EPS).astype(BF16)
    acc = jnp.zeros(x.shape, F32)
    for c in range(d_ff // FFN_CHUNK):
        lo = c * FFN_CHUNK
        gate = _dot(xn, win_ref[:, lo:lo + FFN_CHUNK].astype(BF16))
        up = _dot(xn, win_ref[:, d_ff + lo:d_ff + lo + FFN_CHUNK].astype(BF16))
        act = (gate * jax.nn.sigmoid(gate) * up).astype(BF16)
        acc = acc + _dot(act, wout_ref[lo:lo + FFN_CHUNK, :].astype(BF16))
    o_ref[...] = x + 0.5 * _rms(acc, gpost_ref[...], NORM_EPS)


def ffn_half(h, g_pre, g_post, w_in_all, w_out_all, layer, which):
    n, d = h.shape
    d_ff = w_out_all.shape[2]
    tm = _token_tile(n)
    pick = lambda shape: pl.BlockSpec((None, None) + tuple(shape[2:]), lambda i: (layer, which, 0, 0),
                                      pipeline_mode=pl.Buffered(1))
    return pl.pallas_call(
        functools.partial(_ffn_kernel, d_ff=d_ff),
        grid=(n // tm,),
        in_specs=[pl.BlockSpec((tm, d), lambda i: (i, 0)),
                  _const_spec((1, d)), _const_spec((1, d)),
                  pick(w_in_all.shape), pick(w_out_all.shape)],
        out_specs=pl.BlockSpec((tm, d), lambda i: (i, 0)),
        out_shape=jax.ShapeDtypeStruct((n, d), F32),
        compiler_params=_cparams("parallel"),
        name="ffn_half",
    )(h, g_pre.reshape(1, d), g_post.reshape(1, d), w_in_all, w_out_all)


def _rope_slab(y, cos, sin):
    lane = lax.broadcasted_iota(jnp.int32, y.shape, 1)
    swapped = jnp.where(lane % HEAD_DIM < HEAD_DIM // 2,
                        pltpu.roll(y, LANES - HEAD_DIM // 2, 1),
                        pltpu.roll(y, HEAD_DIM // 2, 1))
    return y * cos + swapped * sin


def _proj_kernel(x_ref, g_ref, cos_ref, sin_ref, w_ref, *refs, plan, out_defs, first_tiles, tiles_per_b):
    out_refs = refs[:len(out_defs)]
    scr_ref = refs[len(out_defs)]
    tm = x_ref.shape[0]
    tile_in_b = pl.program_id(0) % tiles_per_b
    xn = _rms(x_ref[...], g_ref[...], NORM_EPS).astype(BF16)
    cos = cos_ref[...]
    sin = sin_ref[...]
    lane = lax.broadcasted_iota(jnp.int32, (tm, LANES), 1)

    def emit(val, val_t, out_idx, slab):
        ref = out_refs[out_idx]
        kind = out_defs[out_idx][0]
        cs = slice(slab * LANES, (slab + 1) * LANES)
        if kind == "N":
            ref[:, cs] = val().astype(ref.dtype)
        elif kind == "I":
            ref[pl.ds(slab, tm, stride=out_defs[out_idx][1]), :] = val().astype(ref.dtype)
        elif kind == "T":
            def write_t():
                ref[cs, :] = val_t().astype(ref.dtype)
            if first_tiles[out_idx] == 0:
                write_t()
            else:
                pl.when(tile_in_b >= first_tiles[out_idx])(write_t)
        else:
            dil = out_defs[out_idx][3]
            if dil == 1:
                ref[0, :, cs] = val().astype(ref.dtype)
            else:
                scr_ref[...] = val()
                for r in range(dil):
                    ref[r, :, cs] = scr_ref[pl.ds(r, tm // dil, stride=dil), :].astype(ref.dtype)

    for col0, nslab, rope, scale, pad, dests in plan:
        y = _dot(xn, w_ref[:, col0:col0 + nslab * LANES])
        for j in range(nslab):
            ys = y[:, j * LANES:(j + 1) * LANES]
            if rope:
                ys = _rope_slab(ys, cos, sin)
            if scale != 1.0:
                ys = ys * scale
            if pad is None:
                for out_idx, slab in dests[j]:
                    emit(lambda ys=ys: ys, lambda ys=ys: ys.T, out_idx, slab)
                continue
            transposed = []

            def ys_t(ys=ys, transposed=transposed):
                if not transposed:
                    transposed.append(ys.T)
                return transposed[0]

            for hh in range(2):
                head = 2 * j + hh
                at_hi = hh == 1 if pad == "pair" else head >= nslab

                def val(ys=ys, hh=hh, at_hi=at_hi):
                    v = ys if at_hi == (hh == 1) else pltpu.roll(ys, HEAD_DIM, 1)
                    return jnp.where(lane >= HEAD_DIM if at_hi else lane < HEAD_DIM, v, 0.0)

                def val_t(ys_t=ys_t, hh=hh, at_hi=at_hi):
                    rows = ys_t()[hh * HEAD_DIM:(hh + 1) * HEAD_DIM, :]
                    zero = jnp.zeros_like(rows)
                    return jnp.concatenate([zero, rows] if at_hi else [rows, zero], axis=0)

                for out_idx, slab in dests[head]:
                    emit(val, val_t, out_idx, slab)


def project(h, b, g, cos, sin, w, plan, out_defs):
    n, d = h.shape
    t = n // b
    tm = _token_tile(t)
    tpb = t // tm
    specs, shapes, first_tiles = [], [], []
    for od in out_defs:
        kind, c, dt = od[:3]
        if kind == "N":
            specs.append(pl.BlockSpec((tm, c), lambda i: (i, 0)))
            shapes.append(jax.ShapeDtypeStruct((n, c), dt))
            first_tiles.append(0)
        elif kind == "I":
            specs.append(pl.BlockSpec((tm * c, LANES), lambda i: (i, 0)))
            shapes.append(jax.ShapeDtypeStruct((n * c, LANES), dt))
            first_tiles.append(0)
        elif kind == "T":
            keep = max(min(od[3], t), tm)
            ft = (t - keep) // tm
            specs.append(pl.BlockSpec((None, c, tm),
                                      functools.partial(lambda i, ft: (i // tpb, 0, jnp.maximum(i % tpb - ft, 0)),
                                                        ft=ft)))
            shapes.append(jax.ShapeDtypeStruct((b, c, keep), dt))
            first_tiles.append(ft)
        else:
            dil = od[3]
            specs.append(pl.BlockSpec((None, dil, tm // dil, c), lambda i: (i // tpb, 0, i % tpb, 0)))
            shapes.append(jax.ShapeDtypeStruct((b, dil, t // dil, c), dt))
            first_tiles.append(0)
    return pl.pallas_call(
        functools.partial(_proj_kernel, plan=plan, out_defs=out_defs, first_tiles=tuple(first_tiles),
                          tiles_per_b=tpb),
        grid=(n // tm,),
        in_specs=[pl.BlockSpec((tm, d), lambda i: (i, 0)), _const_spec((1, d)),
                  pl.BlockSpec((tm, LANES), lambda i: (i, 0)),
                  pl.BlockSpec((tm, LANES), lambda i: (i, 0)),
                  _const_spec(w.shape)],
        out_specs=specs,
        out_shape=shapes,
        scratch_shapes=[pltpu.VMEM((tm, LANES), F32)],
        compiler_params=_cparams("arbitrary"),
        name="project",
    )(h, g.reshape(1, d), cos, sin, w)


def _outproj_kernel(h_ref, *refs):
    w_ref, g_ref, o_ref = refs[-3:]
    y, row0 = None, 0
    for m_ref in refs[:-3]:
        c = m_ref.shape[1]
        part = _dot(m_ref[...], w_ref[row0:row0 + c, :])
        y = part if y is None else y + part
        row0 += c
    o_ref[...] = h_ref[...] + _rms(y, g_ref[...], NORM_EPS)


def outproj(h, ms, w, g):
    n, d = h.shape
    tm = _token_tile(n)
    return pl.pallas_call(
        _outproj_kernel,
        grid=(n // tm,),
        in_specs=[pl.BlockSpec((tm, d), lambda i: (i, 0))]
                 + [pl.BlockSpec((tm, m.shape[1]), lambda i: (i, 0)) for m in ms]
                 + [_const_spec(w.shape), _const_spec((1, d))],
        out_specs=pl.BlockSpec((tm, d), lambda i: (i, 0)),
        out_shape=jax.ShapeDtypeStruct((n, d), F32),
        compiler_params=_cparams("parallel"),
        name="outproj",
    )(h, *ms, w, g.reshape(1, d))


def _outproj_dil_kernel(h_ref, *refs):
    ng = len(DIL_GROUPS)
    w_ref, g_ref, o_ref, scr_ref = refs[2 * ng:]
    tm = h_ref.shape[0]
    nslab = DIL_HEADS * HEAD_DIM // LANES
    vals = []
    k = 0
    for gi, (_, dil) in enumerate(DIL_GROUPS):
        per_g = []
        for ref in (refs[2 * gi], refs[2 * gi + 1]):
            slabs = []
            for s in range(nslab):
                cs = slice(s * LANES, (s + 1) * LANES)
                if dil == 1:
                    slabs.append(ref[0, :, cs])
                else:
                    for r in range(dil):
                        scr_ref[k, pl.ds(r, tm // dil, stride=dil), :] = ref[r, :, cs]
                    slabs.append(scr_ref[k])
                    k += 1
            per_g.append(slabs)
        vals.append(per_g)
    mixed = []
    for s in range(nslab):
        l0, l1, l2 = vals[0][1][s], vals[1][1][s], vals[2][1][s]
        mx = jnp.maximum(jnp.maximum(l0, l1), l2)
        e0, e1, e2 = jnp.exp(l0 - mx), jnp.exp(l1 - mx), jnp.exp(l2 - mx)
        den = e0 + e1 + e2
        mixed.append(((e0 / den) * vals[0][0][s] + (e1 / den) * vals[1][0][s]
                      + (e2 / den) * vals[2][0][s]).astype(BF16))
    y = _dot(jnp.concatenate(mixed, axis=1), w_ref[...])
    o_ref[...] = h_ref[...] + _rms(y, g_ref[...], NORM_EPS)


def outproj_dil(h, b, outs, lses, w, g):
    n, d = h.shape
    t = n // b
    tm = _token_tile(t)
    tpb = t // tm
    c = DIL_HEADS * HEAD_DIM
    specs, args, n_scr = [], [], 0
    for (_, dil), o, l in zip(DIL_GROUPS, outs, lses):
        spec = pl.BlockSpec((None, dil, tm // dil, c), lambda i: (i // tpb, 0, i % tpb, 0))
        specs += [spec, spec]
        args += [o, l]
        if dil > 1:
            n_scr += 2 * (c // LANES)
    return pl.pallas_call(
        _outproj_dil_kernel,
        grid=(n // tm,),
        in_specs=[pl.BlockSpec((tm, d), lambda i: (i, 0))] + specs + [_const_spec(w.shape), _const_spec((1, d))],
        out_specs=pl.BlockSpec((tm, d), lambda i: (i, 0)),
        out_shape=jax.ShapeDtypeStruct((n, d), F32),
        scratch_shapes=[pltpu.VMEM((n_scr, tm, LANES), F32)],
        compiler_params=_cparams("parallel"),
        name="outproj_dil",
    )(h, *args, w, g.reshape(1, d))


def _compress_rows(k_ref, v_ref, pe_ref, wa_ref, wb_ref, nsub):
    half = CMP_LEN // 2
    acc_a = jnp.zeros((nsub, 2 * LANES), F32)
    acc_b = jnp.zeros((nsub, 2 * LANES), F32)
    for l in range(half):
        x = jnp.concatenate([k_ref[pl.ds(l, nsub, stride=CMP_STRIDE), :],
                             v_ref[pl.ds(l, nsub, stride=CMP_STRIDE), :]], axis=1)
        acc_a = acc_a + _dot((x + pe_ref[l:l + 1, :]).astype(BF16), wa_ref[l])
        acc_b = acc_b + _dot((x + pe_ref[half + l:half + l + 1, :]).astype(BF16), wb_ref[l])
    comp = acc_a + pltpu.roll(acc_b, nsub - 1, 0)
    row = lax.broadcasted_iota(jnp.int32, comp.shape, 0)
    return jnp.where(row < nsub - 1, comp, 0.0)


def _compress_kernel(k_ref, v_ref, pe_ref, wa_ref, wb_ref, k_out_ref, vt_out_ref, *, nsub):
    comp = _compress_rows(k_ref, v_ref, pe_ref, wa_ref, wb_ref, nsub)
    k_out_ref[...] = comp[:, 0:LANES].astype(k_out_ref.dtype)
    vt_out_ref[...] = comp[:, LANES:2 * LANES].T.astype(vt_out_ref.dtype)


def nsa_compress_prompt(cmp_rows, pe4, wa, wb):
    b, t, _ = cmp_rows.shape
    nsub = t // CMP_STRIDE
    return pl.pallas_call(
        functools.partial(_compress_kernel, nsub=nsub),
        grid=(b,),
        in_specs=[pl.BlockSpec((None, t, LANES), lambda i: (i, 0, 0)),
                  pl.BlockSpec((None, t, LANES), lambda i: (i, 0, 1)),
                  _const_spec(pe4.shape), _const_spec(wa.shape), _const_spec(wb.shape)],
        out_specs=[pl.BlockSpec((None, nsub, LANES), lambda i: (i, 0, 0)),
                   pl.BlockSpec((None, LANES, nsub), lambda i: (i, 0, 0))],
        out_shape=[jax.ShapeDtypeStruct((b, nsub, LANES), BF16), jax.ShapeDtypeStruct((b, LANES, nsub), BF16)],
        compiler_params=_cparams("parallel"),
        name="nsa_compress_prompt",
    )(cmp_rows, cmp_rows, pe4, wa, wb)


NSA_TQ = 256
NSA_TK = 512


def _softmax_terms(s, mask, exp_fn):
    s = jnp.where(mask, s, -jnp.inf)
    m = jnp.max(s, axis=-1, keepdims=True)
    m = jnp.where(m == -jnp.inf, 0.0, m)
    e = exp_fn(s - m)
    den = jnp.sum(e, axis=-1, keepdims=True)
    return e, 1.0 / jnp.where(den > 0, den, 1.0)


def _softmax_rows(s, mask, exp_fn=jnp.exp):
    e, inv = _softmax_terms(s, mask, exp_fn)
    return e * inv


def _topk_mask_t(score_t, k):
    j_io = lax.broadcasted_iota(jnp.int32, score_t.shape, 0)
    nj = score_t.shape[0]
    work = score_t
    for _ in range(k):
        m = jnp.max(work, axis=0, keepdims=True)
        jmin = jnp.min(jnp.where(work == m, j_io, nj), axis=0, keepdims=True)
        work = jnp.where(j_io == jmin, -jnp.inf, work)
    return jnp.logical_and(work == -jnp.inf, score_t > -jnp.inf)


SWEEP_CHUNKS = 4


def _causal_sweep_t(score_fn, pv_fn, mask_fn, n_full, cols):
    nc = SWEEP_CHUNKS
    cc = cols // nc

    def update(tiles, masked, state):
        scs = [[score_fn(t, c) for t in tiles] for c in range(nc)]
        out = []
        for c, (m, l, acc) in enumerate(state):
            sc = scs[c]
            if masked:
                sc = [jnp.where(mask_fn(t, c), s, NEG_BIG) for t, s in zip(tiles, sc)]
            m_new = m
            for s in sc:
                m_new = jnp.maximum(m_new, jnp.max(s, axis=0, keepdims=True))
            alpha = jnp.exp2(m - m_new)
            l = alpha * l
            acc = alpha * acc
            for t, s in zip(tiles, sc):
                pe = jnp.exp2(s - m_new)
                l = l + jnp.sum(pe, axis=0, keepdims=True)
                acc = acc + pv_fn(t, pe.astype(BF16))
            out.append((m_new, l, acc))
        return out

    init = [(jnp.full((1, cc), NEG_BIG, F32), jnp.zeros((1, cc), F32), jnp.zeros((LANES, cc), F32))
            for _ in range(nc)]
    state = lax.fori_loop(0, n_full // 4, lambda u, st: update([4 * u + k for k in range(4)], False, st), init)
    base2 = (n_full // 4) * 4
    state = lax.fori_loop(0, (n_full % 4) // 2, lambda _, st: update([base2, base2 + 1], False, st), state)
    state = lax.fori_loop(0, n_full % 2, lambda _, st: update([n_full - 1], False, st), state)
    state = update([n_full], True, state)
    return (jnp.concatenate([s[1] for s in state], axis=1), jnp.concatenate([s[2] for s in state], axis=1))


def _softmax_terms_t(s, mask):
    s = jnp.where(mask, s, -jnp.inf)
    m = jnp.max(s, axis=0, keepdims=True)
    m = jnp.where(m == -jnp.inf, 0.0, m)
    e = jnp.exp2(s - m)
    den = jnp.sum(e, axis=0, keepdims=True)
    return e, 1.0 / jnp.where(den > 0, den, 1.0)


def _nsa_prompt_kernel(qt_ref, gt_ref, kcmp_ref, vcmp_ref, kslc_ref, vslc_ref, kwin_ref, vwin_ref, eall_ref,
                       selt_ref, o_ref):
    tq = NSA_TQ
    qs = pl.program_id(1) * tq
    rows = NSA_GROUP * tq
    qpos = qs + lax.broadcasted_iota(jnp.int32, (1, rows), 1) % tq
    ncmp = kcmp_ref.shape[0]
    gates_t = jax.nn.sigmoid(gt_ref[...])
    head_out = []
    for kvh in range(NSA_KV_HEADS):
        q4t = jnp.concatenate(
            [qt_ref[(NSA_GROUP * kvh + g) * LANES:(NSA_GROUP * kvh + g + 1) * LANES, :]
             for g in range(NSA_GROUP)], axis=1)
        s = _dot(kcmp_ref[...], q4t)
        wlen = NSA_WINDOW + tq
        ws = pl.multiple_of(jnp.maximum(qs - NSA_WINDOW, 0), tq)
        sw = _dot(kwin_ref[pl.ds(ws, wlen), :], q4t)
        cmp_end = lax.broadcasted_iota(jnp.int32, (ncmp, 1), 0) * CMP_STRIDE + (CMP_LEN - 1)
        e, inv = _softmax_terms_t(s, cmp_end <= qpos)
        p = e * inv
        o_cmp_t = _dot(vcmp_ref[...], p.astype(BF16))
        psum = p[:, 0:tq] + p[:, tq:2 * tq] + p[:, 2 * tq:3 * tq] + p[:, 3 * tq:4 * tq]
        hi, mid, lo = _split3(psum)
        selt = selt_ref[...]
        imp_t = _dot(selt, hi) + _dot(selt, mid) + _dot(selt, lo)
        j_io = lax.broadcasted_iota(jnp.int32, imp_t.shape, 0)
        cur = (qs + lax.broadcasted_iota(jnp.int32, imp_t.shape, 1)) // SLC_BLOCK
        forced = jnp.logical_or(j_io == 0, j_io == cur)
        score_t = jnp.where(j_io <= cur, jnp.where(forced, FORCED_BLOCK_SCORE, imp_t), -jnp.inf)
        bias_t = jnp.where(_topk_mask_t(score_t, N_SEL), 0.0, NEG_BIG).astype(BF16)
        qext_t = jnp.concatenate([q4t, jnp.concatenate([bias_t] * NSA_GROUP, axis=1)], axis=0)

        def tile(t):
            return pl.ds(t * NSA_TK if isinstance(t, int) else pl.multiple_of(t * NSA_TK, NSA_TK), NSA_TK)

        cc = rows // SWEEP_CHUNKS

        def sel_scores(t, c, qext_t=qext_t):
            return _dot(jnp.concatenate([kslc_ref[tile(t), :], eall_ref[tile(t), :]], axis=1),
                        qext_t[:, c * cc:(c + 1) * cc])

        def sel_values(t, p):
            return _dot(vslc_ref[:, tile(t)], p)

        def sel_mask(t, c):
            kpos = t * NSA_TK + lax.broadcasted_iota(jnp.int32, (NSA_TK, 1), 0)
            return kpos <= qs + (c * cc + lax.broadcasted_iota(jnp.int32, (1, cc), 1)) % tq

        l_sel, acc_sel = _causal_sweep_t(sel_scores, sel_values, sel_mask, qs // NSA_TK, rows)
        o_sel_t = acc_sel / l_sel
        dist = qpos - (ws + lax.broadcasted_iota(jnp.int32, (wlen, 1), 0))
        ew, inv_w = _softmax_terms_t(sw, jnp.logical_and(dist >= 0, dist < NSA_WINDOW))
        o_win_t = _dot(vwin_ref[:, pl.ds(ws, wlen)], ew.astype(BF16)) * inv_w
        for g in range(NSA_GROUP):
            h = NSA_GROUP * kvh + g
            cs = slice(g * tq, (g + 1) * tq)
            mixed = (gates_t[3 * h:3 * h + 1, :] * o_cmp_t[:, cs] + gates_t[3 * h + 1:3 * h + 2, :] * o_sel_t[:, cs]
                     + gates_t[3 * h + 2:3 * h + 3, :] * o_win_t[:, cs])
            head_out.append(mixed[kvh * HEAD_DIM:(kvh + 1) * HEAD_DIM, :])
    for pair in range(NSA_HEADS // 2):
        slab_t = jnp.concatenate([head_out[2 * pair], head_out[2 * pair + 1]], axis=0)
        o_ref[:, pair * LANES:(pair + 1) * LANES] = slab_t.T.astype(o_ref.dtype)


def nsa_attend_prompt(q_nsa_t, gate_t, kcmp16, vcmp_t, kslc16, vslc_t, kwin16, vwin_t, eall, selt):
    b, _, t = q_nsa_t.shape
    ncmp = kcmp16.shape[1]
    rows = lambda n: pl.BlockSpec((None, n, LANES), lambda i, j: (i, 0, 0))
    cols = lambda n: pl.BlockSpec((None, LANES, n), lambda i, j: (i, 0, 0))
    return pl.pallas_call(
        _nsa_prompt_kernel,
        grid=(b, t // NSA_TQ),
        in_specs=[pl.BlockSpec((None, NSA_HEADS * LANES, NSA_TQ), lambda i, j: (i, 0, j)),
                  pl.BlockSpec((None, LANES, NSA_TQ), lambda i, j: (i, 0, j)),
                  rows(ncmp), cols(ncmp), rows(t), cols(t), rows(t), cols(t),
                  _const_spec(eall.shape), _const_spec(selt.shape)],
        out_specs=pl.BlockSpec((None, NSA_TQ, NSA_HEADS * HEAD_DIM), lambda i, j: (i, j, 0)),
        out_shape=jax.ShapeDtypeStruct((b, t, NSA_HEADS * HEAD_DIM), BF16),
        compiler_params=_cparams("parallel", "parallel"),
        name="nsa_attend_prompt",
    )(q_nsa_t, gate_t, kcmp16, vcmp_t, kslc16, vslc_t, kwin16, vwin_t, eall, selt)


DIFF_TQ = 512
DIFF_TK = 512


def _diff_lambda(lam_ref, lam_init):
    lv = lam_ref[...]
    a = jnp.sum(lv[0:1] * lv[1:2], axis=-1, keepdims=True)
    b = jnp.sum(lv[2:3] * lv[3:4], axis=-1, keepdims=True)
    return jnp.exp(a) - jnp.exp(b) + lam_init


def _diff_prompt_kernel(qt_ref, k_ref, vt_ref, lam_ref, ng_ref, o_ref, *, lam_init):
    tq = DIFF_TQ
    qs = pl.program_id(2) * tq
    q2t = jnp.concatenate([qt_ref[0:LANES, :], qt_ref[LANES:2 * LANES, :]], axis=1)
    cc = 2 * tq // SWEEP_CHUNKS

    def tile(t):
        return pl.ds(t * DIFF_TK if isinstance(t, int) else pl.multiple_of(t * DIFF_TK, DIFF_TK), DIFF_TK)

    def scores(t, c):
        return _dot(k_ref[tile(t), :], q2t[:, c * cc:(c + 1) * cc])

    def values(t, p):
        return _dot(vt_ref[:, tile(t)], p)

    def mask(t, c):
        kpos = t * DIFF_TK + lax.broadcasted_iota(jnp.int32, (DIFF_TK, 1), 0)
        qpos = qs + (c * cc + lax.broadcasted_iota(jnp.int32, (1, cc), 1)) % tq
        return kpos <= qpos

    l, acc = _causal_sweep_t(scores, values, mask, qs // DIFF_TK, 2 * tq)
    ot = acc / l
    lam = _diff_lambda(lam_ref, lam_init)
    ot = ot[:, 0:tq] - lam * ot[:, tq:2 * tq]
    o = ot.T
    o_ref[...] = (_rms(o, ng_ref[...], DIFF_NORM_EPS) * (1.0 - lam_init)).astype(o_ref.dtype)


def diff_attend_prompt(q_diff_t, k_diff16, v_diff_t, lam_vec, norm_g, lam_init):
    b, _, t = q_diff_t.shape
    return pl.pallas_call(
        functools.partial(_diff_prompt_kernel, lam_init=lam_init),
        grid=(b, DIFF_HEADS, t // DIFF_TQ),
        in_specs=[pl.BlockSpec((None, 2 * LANES, DIFF_TQ), lambda i, h, j: (i, h, j)),
                  pl.BlockSpec((None, t, LANES), lambda i, h, j: (i, 0, h)),
                  pl.BlockSpec((None, LANES, t), lambda i, h, j: (i, h, 0)),
                  _const_spec(lam_vec.shape), _const_spec((1, LANES))],
        out_specs=pl.BlockSpec((None, DIFF_TQ, LANES), lambda i, h, j: (i, j, h)),
        out_shape=jax.ShapeDtypeStruct((b, t, DIFF_HEADS * LANES), BF16),
        compiler_params=_cparams("parallel", "parallel", "parallel"),
        name="diff_attend_prompt",
    )(q_diff_t, k_diff16, v_diff_t, lam_vec, norm_g.reshape(1, LANES))


def _dil_prompt_kernel(q_ref, kvp_ref, kvc_ref, o_ref, lse_ref):
    band = DIL_BAND
    hw = DIL_HEADS * HEAD_DIM
    first = pl.program_id(2) == 0
    qi = lax.broadcasted_iota(jnp.int32, (band, 2 * band), 0) + band
    kj = lax.broadcasted_iota(jnp.int32, (band, 2 * band), 1)
    rel = qi - kj
    ok = jnp.logical_and(rel >= 0, rel <= band)
    ok = jnp.logical_and(ok, jnp.logical_not(jnp.logical_and(first, kj < band)))
    lane = lax.broadcasted_iota(jnp.int32, (band, LANES), 1)
    scores = []
    for h in range(DIL_HEADS):
        cs = slice((h // 2) * LANES, (h // 2 + 1) * LANES)
        k2 = jnp.concatenate([kvp_ref[:, cs], kvc_ref[:, cs]], axis=0)
        scores.append(_dot_nt(q_ref[:, h * LANES:(h + 1) * LANES], k2))
    for pair in range(DIL_HEADS // 2):
        cs = slice(pair * LANES, (pair + 1) * LANES)
        vs = slice(hw + pair * LANES, hw + (pair + 1) * LANES)
        v2 = jnp.concatenate([kvp_ref[:, vs], kvc_ref[:, vs]], axis=0)
        outs, lses = [], []
        for hh in range(2):
            s = jnp.where(ok, scores[2 * pair + hh], -jnp.inf)
            m = jnp.max(s, axis=-1, keepdims=True)
            e = jnp.exp2(s - m)
            den = jnp.sum(e, axis=-1, keepdims=True)
            outs.append(_dot(e.astype(BF16), v2) * (1.0 / den))
            lses.append(jnp.log(den) + m * LN2)
        o_ref[:, cs] = jnp.where(lane < HEAD_DIM, outs[0], outs[1])
        lse_ref[:, cs] = jnp.where(lane < HEAD_DIM, lses[0], lses[1])


def dil_attend_prompt(q_r, kv_r):
    b, dil, n, _ = q_r.shape
    hw = DIL_HEADS * HEAD_DIM
    blk = lambda w, prev: pl.BlockSpec(
        (None, None, DIL_BAND, w), (lambda i, r, u: (i, r, jnp.maximum(u - 1, 0), 0)) if prev
        else (lambda i, r, u: (i, r, u, 0)))
    return pl.pallas_call(
        _dil_prompt_kernel,
        grid=(b, dil, n // DIL_BAND),
        in_specs=[blk(DIL_HEADS * LANES, False), blk(2 * hw, True), blk(2 * hw, False)],
        out_specs=[blk(hw, False), blk(hw, False)],
        out_shape=[jax.ShapeDtypeStruct((b, dil, n, hw), F32)] * 2,
        compiler_params=_cparams("parallel", "parallel", "parallel"),
        name="dil_attend_prompt",
    )(q_r, kv_r, kv_r)


TRANSPOSE_UNROLL = 8


def _page_copies(cache_ref, pt_ref, buf_ref, sem_ref, bi, slot, n_pages):
    return [pltpu.make_async_copy(cache_ref.at[pt_ref[bi, j], pl.ds(0, 2 * LANES), :],
                                  buf_ref.at[slot, j], sem_ref.at[slot]) for j in range(n_pages)]


def _compress_sample_kernel(pt_ref, cache_ref, pe_ref, wa_ref, wb_ref, o_ref, page_ref, kbuf_ref, vbuf_ref,
                            sem_ref, *, n_pages):
    i = pl.program_id(0)
    slot = i % 2
    nsub = n_pages * PAGE_SIZE // CMP_STRIDE
    copies = functools.partial(_page_copies, cache_ref, pt_ref, page_ref, sem_ref, n_pages=n_pages)

    @pl.when(i == 0)
    def _():
        for cp in copies(0, 0):
            cp.start()

    @pl.when(i + 1 < pl.num_programs(0))
    def _():
        for cp in copies(i + 1, 1 - slot):
            cp.start()

    for cp in copies(i, slot):
        cp.wait()

    def to_rows(jj, carry):
        for u in range(TRANSPOSE_UNROLL):
            j = jj * TRANSPOSE_UNROLL + u
            r0 = pl.multiple_of(j * PAGE_SIZE, PAGE_SIZE)
            kbuf_ref[pl.ds(r0, PAGE_SIZE), :] = page_ref[slot, j, 0:LANES, :].T
            vbuf_ref[pl.ds(r0, PAGE_SIZE), :] = page_ref[slot, j, LANES:2 * LANES, :].T
        return carry

    lax.fori_loop(0, n_pages // TRANSPOSE_UNROLL, to_rows, 0)
    o_ref[...] = _compress_rows(kbuf_ref, vbuf_ref, pe_ref, wa_ref, wb_ref, nsub).astype(o_ref.dtype)


def nsa_compress_sample(cache_t, page_table, pe4, wa, wb):
    bs, n_pages = page_table.shape
    past = n_pages * PAGE_SIZE
    nsub = past // CMP_STRIDE
    grid_spec = pltpu.PrefetchScalarGridSpec(
        num_scalar_prefetch=1,
        grid=(bs,),
        in_specs=[pl.BlockSpec(memory_space=pl.ANY),
                  pl.BlockSpec(pe4.shape, lambda i, pt: (0, 0)),
                  pl.BlockSpec(wa.shape, lambda i, pt: (0, 0, 0)),
                  pl.BlockSpec(wb.shape, lambda i, pt: (0, 0, 0))],
        out_specs=pl.BlockSpec((None, nsub, 2 * LANES), lambda i, pt: (i, 0, 0)),
        scratch_shapes=[pltpu.VMEM((2, n_pages, 2 * LANES, PAGE_SIZE), F32),
                        pltpu.VMEM((past, LANES), F32), pltpu.VMEM((past, LANES), F32),
                        pltpu.SemaphoreType.DMA((2,))],
    )
    return pl.pallas_call(
        functools.partial(_compress_sample_kernel, n_pages=n_pages),
        grid_spec=grid_spec,
        out_shape=jax.ShapeDtypeStruct((bs, nsub, 2 * LANES), BF16),
        compiler_params=_cparams("arbitrary"),
        name="nsa_compress_sample",
    )(page_table, cache_t, pe4, wa, wb)


def _group_sum_rows(x):
    parts = [jnp.sum(x[NSA_GROUP * k:NSA_GROUP * (k + 1)], axis=0, keepdims=True) for k in range(NSA_KV_HEADS)]
    return _pad_rows(jnp.concatenate(parts, axis=0), x.shape[0])


def _nsa_sample_cmp_kernel(q_ref, cmp_ref, selt_ref, ocmp_ref, imp_ref, *, qpos):
    q8 = q_ref[...]
    ncmp = cmp_ref.shape[0]
    s = _dot_nt(q8, cmp_ref[:, 0:LANES])
    cmp_end = lax.broadcasted_iota(jnp.int32, (1, ncmp), 1) * CMP_STRIDE + (CMP_LEN - 1)
    p = _softmax_rows(s, cmp_end <= qpos)
    ocmp_ref[...] = _dot(p.astype(BF16), cmp_ref[:, LANES:2 * LANES])
    hi, mid, lo = _split3(_group_sum_rows(p))
    selt = selt_ref[...]
    imp_ref[...] = _dot_nt(hi, selt) + _dot_nt(mid, selt) + _dot_nt(lo, selt)


def nsa_sample_cmp(q8, cmp_kv, selt, qpos):
    bs = q8.shape[0]
    ncmp = cmp_kv.shape[1]
    blk = pl.BlockSpec((None, NSA_HEADS, LANES), lambda i: (i, 0, 0))
    return pl.pallas_call(
        functools.partial(_nsa_sample_cmp_kernel, qpos=qpos),
        grid=(bs,),
        in_specs=[blk, pl.BlockSpec((None, ncmp, 2 * LANES), lambda i: (i, 0, 0)), _const_spec(selt.shape)],
        out_specs=[blk, blk],
        out_shape=[jax.ShapeDtypeStruct((bs, NSA_HEADS, LANES), F32)] * 2,
        compiler_params=_cparams("parallel"),
        name="nsa_sample_cmp",
    )(q8, cmp_kv, selt)


def _topk_lanes_kernel(imp_ref, idx_ref, *, k):
    score = imp_ref[...]
    lane = lax.broadcasted_iota(jnp.int32, score.shape, 1)
    lane_f = lane.astype(F32)
    score = jnp.where(lane == 0, FORCED_BLOCK_SCORE, score)
    idx = jnp.zeros(score.shape, F32)
    for r in range(k):
        m = jnp.max(score, axis=-1, keepdims=True)
        jmin = jnp.min(jnp.where(score == m, lane_f, float(LANES)), axis=-1, keepdims=True)
        idx = jnp.where(lane == r, jmin, idx)
        score = jnp.where(lane_f == jmin, -jnp.inf, score)
    idx_ref[...] = idx.astype(jnp.int32)


def topk_lanes(imp, k):
    bs = imp.shape[0]
    x = imp.reshape(bs * NSA_HEADS, LANES)
    out = pl.pallas_call(
        functools.partial(_topk_lanes_kernel, k=k),
        out_shape=jax.ShapeDtypeStruct(x.shape, jnp.int32),
        name="topk_lanes",
    )(x)
    return out.reshape(bs, NSA_HEADS, LANES)


N_SEL_CACHE = N_SEL - 1


def _sel_copies(cache_ref, pt_ref, sel_ref, buf_ref, sem_ref, bi, slot):
    cps = []
    for kvh in range(NSA_KV_HEADS):
        for r in range(N_SEL_CACHE):
            j = sel_ref[bi, kvh * N_SEL_CACHE + r]
            cps.append(pltpu.make_async_copy(
                cache_ref.at[pt_ref[bi, j // 2], pl.ds(2 * LANES, 2 * LANES), :],
                buf_ref.at[slot, kvh, r], sem_ref.at[slot]))
    return cps


def _pick_gate(gates8, branch):
    row = lax.broadcasted_iota(jnp.int32, gates8.shape, 0)
    lane = lax.broadcasted_iota(jnp.int32, gates8.shape, 1)
    return jnp.sum(jnp.where(lane == 3 * row + branch, gates8, 0.0), axis=-1, keepdims=True)


def _nsa_sample_attend_kernel(pt_ref, sel_ref, q_ref, gate_ref, ocmp_ref, new_ref, win_ref, cache_ref,
                              o_ref, buf_ref, sem_ref):
    i = pl.program_id(0)
    slot = i % 2
    copies = functools.partial(_sel_copies, cache_ref, pt_ref, sel_ref, buf_ref, sem_ref)

    @pl.when(i == 0)
    def _():
        for cp in copies(0, 0):
            cp.start()

    @pl.when(i + 1 < pl.num_programs(0))
    def _():
        for cp in copies(i + 1, 1 - slot):
            cp.start()

    q8 = q_ref[...]
    q8f = q8.astype(F32)
    row = lax.broadcasted_iota(jnp.int32, (NSA_HEADS, 1), 0)
    new = new_ref[...]
    rnd = lambda x: x.astype(BF16).astype(F32)

    def probs_with_new_key(s, k_new, mask):
        s_new = jnp.sum(q8f * rnd(k_new), axis=-1, keepdims=True)
        s = jnp.where(mask, s, -jnp.inf)
        m = jnp.maximum(jnp.max(s, axis=-1, keepdims=True), s_new)
        e = jnp.exp(s - m)
        e_new = jnp.exp(s_new - m)
        den = jnp.sum(e, axis=-1, keepdims=True) + e_new
        return (e / den).astype(BF16), rnd(e_new / den)

    wb = win_ref.shape[1]
    widx = lax.broadcasted_iota(jnp.int32, (1, wb), 1)
    pw, pw_new = probs_with_new_key(_dot(q8, win_ref[0:LANES, :].astype(BF16)), new[:, 4 * LANES:5 * LANES],
                                    widx > wb - NSA_WINDOW)
    o_win = _dot_nt(pw, win_ref[LANES:2 * LANES, :].astype(BF16)) + pw_new * rnd(new[:, 5 * LANES:6 * LANES])
    for cp in copies(i, slot):
        cp.wait()
    lane = lax.broadcasted_iota(jnp.int32, (1, PAGE_SIZE), 1)
    o_sel = []
    for kvh in range(NSA_KV_HEADS):
        ss, masks = [], []
        for r in range(N_SEL_CACHE):
            ss.append(_dot(q8, buf_ref[slot, kvh, r, 0:LANES, :].astype(BF16)))
            masks.append(lane // SLC_BLOCK == sel_ref[i, kvh * N_SEL_CACHE + r] % 2)
        p, p_new = probs_with_new_key(jnp.concatenate(ss, axis=1), new[:, 2 * LANES:3 * LANES],
                                      jnp.concatenate(masks, axis=1))
        o = p_new * rnd(new[:, 3 * LANES:4 * LANES])
        for r in range(N_SEL_CACHE):
            o = o + _dot_nt(p[:, r * PAGE_SIZE:(r + 1) * PAGE_SIZE],
                            buf_ref[slot, kvh, r, LANES:2 * LANES, :].astype(BF16))
        o_sel.append(o)
    o_sel = jnp.where(row < NSA_GROUP, o_sel[0], o_sel[1])
    gates8 = jnp.broadcast_to(jax.nn.sigmoid(gate_ref[...]), (NSA_HEADS, LANES))
    o_ref[...] = (_pick_gate(gates8, 0) * ocmp_ref[...] + _pick_gate(gates8, 1) * o_sel
                  + _pick_gate(gates8, 2) * o_win)


def nsa_sample_attend(page_table, sel_idx, q8, gate, o_cmp, new_rows, win_t, cache_t):
    bs = q8.shape[0]
    wb = win_t.shape[2]
    blk = lambda w: pl.BlockSpec((None, NSA_HEADS, w), lambda i, pt, sel: (i, 0, 0))
    one = lambda w: pl.BlockSpec((None, 1, w), lambda i, pt, sel: (i, 0, 0))
    grid_spec = pltpu.PrefetchScalarGridSpec(
        num_scalar_prefetch=2,
        grid=(bs,),
        in_specs=[blk(LANES), one(LANES), blk(LANES), one(new_rows.shape[-1]),
                  pl.BlockSpec((None, 2 * LANES, wb), lambda i, pt, sel: (i, 0, 0)),
                  pl.BlockSpec(memory_space=pl.ANY)],
        out_specs=blk(LANES),
        scratch_shapes=[pltpu.VMEM((2, NSA_KV_HEADS, N_SEL_CACHE, 2 * LANES, PAGE_SIZE), F32),
                        pltpu.SemaphoreType.DMA((2,))],
    )
    return pl.pallas_call(
        _nsa_sample_attend_kernel,
        grid_spec=grid_spec,
        out_shape=jax.ShapeDtypeStruct((bs, NSA_HEADS, LANES), F32),
        compiler_params=_cparams("arbitrary"),
        name="nsa_sample_attend",
    )(page_table, sel_idx, q8, gate, o_cmp, new_rows, win_t, cache_t)


DIFF_PAGES_PER_STEP = 16
DIFF_ROW_STRIDE = 2 * DIFF_HEADS


def _diff_sample_kernel(pt_ref, qt_ref, new_ref, lam_ref, ng_ref, *rest, lam_init):
    pages = rest[:DIFF_PAGES_PER_STEP]
    o_ref, m_ref, l_ref, acc_ref = rest[DIFF_PAGES_PER_STEP:]
    c = pl.program_id(1)
    hw = DIFF_HEADS * LANES
    qt = qt_ref[...]
    rnd = lambda x: x.astype(BF16).astype(F32)

    def heads(page, which):
        return jnp.concatenate(
            [page[pl.ds(which * DIFF_HEADS + h, PAGE_SIZE, stride=DIFF_ROW_STRIDE), :] for h in range(DIFF_HEADS)],
            axis=1).astype(BF16)

    @pl.when(c == 0)
    def _():
        s_new = jnp.sum(qt.astype(F32) * rnd(new_ref[:, 0:hw]), axis=-1, keepdims=True)
        m_ref[...] = jnp.broadcast_to(s_new, m_ref.shape)
        l_ref[...] = jnp.ones(l_ref.shape, F32)
        acc_ref[...] = jnp.broadcast_to(rnd(new_ref[:, hw:2 * hw]), acc_ref.shape)

    s = jnp.concatenate([_dot_nt(qt, heads(page, 0)) for page in pages], axis=1)
    m_old = m_ref[:, 0:1]
    m_new = jnp.maximum(m_old, jnp.max(s, axis=-1, keepdims=True))
    alpha = jnp.exp(m_old - m_new)
    e = jnp.exp(s - m_new)
    l = l_ref[:, 0:1] * alpha + jnp.sum(e, axis=-1, keepdims=True)
    acc = acc_ref[...] * alpha
    for i, page in enumerate(pages):
        acc = acc + _dot(e[:, i * PAGE_SIZE:(i + 1) * PAGE_SIZE].astype(BF16), heads(page, 1))
    m_ref[...] = jnp.broadcast_to(m_new, m_ref.shape)
    l_ref[...] = jnp.broadcast_to(l, l_ref.shape)
    acc_ref[...] = acc

    @pl.when(c == pl.num_programs(1) - 1)
    def _():
        o = acc / l
        lam = _diff_lambda(lam_ref, lam_init)
        for h in range(DIFF_HEADS):
            cs = slice(h * LANES, (h + 1) * LANES)
            oh = o[h:h + 1, cs] - lam * o[DIFF_HEADS + h:DIFF_HEADS + h + 1, cs]
            o_ref[:, cs] = _rms(oh, ng_ref[...], DIFF_NORM_EPS) * (1.0 - lam_init)


def diff_attend_sample(page_table, qt, new_rows, lam_vec, norm_g, cache_v, lam_init):
    bs, n_pages = page_table.shape
    p = DIFF_PAGES_PER_STEP
    hw = DIFF_HEADS * LANES
    page_specs = [pl.BlockSpec((None, PAGE_SIZE * DIFF_ROW_STRIDE, LANES),
                               functools.partial(lambda i, c, pt, k: (pt[i, c * p + k], 0, 0), k=k))
                  for k in range(p)]
    grid_spec = pltpu.PrefetchScalarGridSpec(
        num_scalar_prefetch=1,
        grid=(bs, n_pages // p),
        in_specs=[pl.BlockSpec((None, 2 * DIFF_HEADS, hw), lambda i, c, pt: (i, 0, 0)),
                  pl.BlockSpec((None, 1, 2 * hw), lambda i, c, pt: (i, 0, 0)),
                  pl.BlockSpec(lam_vec.shape, lambda i, c, pt: (0, 0)),
                  pl.BlockSpec((1, LANES), lambda i, c, pt: (0, 0))] + page_specs,
        out_specs=pl.BlockSpec((None, 1, hw), lambda i, c, pt: (i, 0, 0)),
        scratch_shapes=[pltpu.VMEM((2 * DIFF_HEADS, LANES), F32), pltpu.VMEM((2 * DIFF_HEADS, LANES), F32),
                        pltpu.VMEM((2 * DIFF_HEADS, hw), F32)],
    )
    return pl.pallas_call(
        functools.partial(_diff_sample_kernel, lam_init=lam_init),
        grid_spec=grid_spec,
        out_shape=jax.ShapeDtypeStruct((bs, 1, hw), F32),
        compiler_params=_cparams("parallel", "arbitrary"),
        name="diff_attend_sample",
    )(page_table, qt, new_rows, lam_vec, norm_g.reshape(1, LANES), *([cache_v] * p))


def _col_rep(row):
    x = jnp.broadcast_to(row, (LANES, row.shape[1]))
    return jnp.concatenate([x[:, c * LANES:(c + 1) * LANES].T for c in range(row.shape[1] // LANES)], axis=0)


def _head_sum(x):
    return jnp.sum(x.reshape(DIL_HEADS, HEAD_DIM, x.shape[1]), axis=1)


def _head_expand(x):
    return jnp.broadcast_to(x[:, None, :], (DIL_HEADS, HEAD_DIM, x.shape[1])).reshape(
        DIL_HEADS * HEAD_DIM, x.shape[1])


STATE_SHIFT_ROWS = 256


def _shift_in(st_ref, new_col, out_ref):
    n_rows, width = st_ref.shape
    lane = lax.broadcasted_iota(jnp.int32, (STATE_SHIFT_ROWS, LANES), 1)
    for r0 in range(0, n_rows, STATE_SHIFT_ROWS):
        rs = slice(r0, r0 + STATE_SHIFT_ROWS)
        rolled = pltpu.roll(st_ref[rs, :], width - 1, 1)
        if width > LANES:
            out_ref[rs, 0:width - LANES] = rolled[:, 0:width - LANES]
        out_ref[rs, width - LANES:width] = jnp.where(lane == LANES - 1, new_col[rs, :], rolled[:, width - LANES:width])


def _dil_sample_kernel(q_ref, new0_ref, new1_ref, new2_ref, st0_ref, st1_ref, st2_ref, o_ref,
                       nst0_ref, nst1_ref, nst2_ref):
    hw = DIL_HEADS * HEAD_DIM
    outs, lses = [], []
    for g, (new_ref, st_ref, nst_ref) in enumerate(((new0_ref, st0_ref, nst0_ref), (new1_ref, st1_ref, nst1_ref),
                                                    (new2_ref, st2_ref, nst2_ref))):
        win, dil = DIL_GROUPS[g]
        qc = _col_rep(q_ref[:, g * hw:(g + 1) * hw])
        kn = _col_rep(new_ref[:, 0:hw])
        vn = _col_rep(new_ref[:, hw:2 * hw])
        _shift_in(st_ref, jnp.concatenate([kn, vn], axis=0), nst_ref)
        s_new = _head_sum(qc * kn)[:, 0:1]
        n_chunks = win // LANES
        s = jnp.concatenate([_head_sum(st_ref[0:hw, c * LANES:(c + 1) * LANES] * qc) for c in range(n_chunks)],
                            axis=1)
        lane = lax.broadcasted_iota(jnp.int32, s.shape, 1)
        s = jnp.where(lane % dil == 0, s, -jnp.inf)
        m = jnp.maximum(jnp.max(s, axis=-1, keepdims=True), s_new)
        e = jnp.exp(s - m)
        e_new = jnp.exp(s_new - m)
        den = jnp.sum(e, axis=-1, keepdims=True) + e_new
        p = e / den
        acc = _head_expand(jnp.broadcast_to(e_new / den, (DIL_HEADS, LANES))) * vn * (1.0 / LANES)
        for c in range(n_chunks):
            cs = slice(c * LANES, (c + 1) * LANES)
            acc = acc + st_ref[hw:2 * hw, cs] * _head_expand(p[:, cs])
        outs.append(jnp.sum(acc, axis=-1, keepdims=True))
        lses.append(jnp.log(den) + m)
    mx = jnp.maximum(jnp.maximum(lses[0], lses[1]), lses[2])
    es = [jnp.exp(l - mx) for l in lses]
    tot = es[0] + es[1] + es[2]
    mix = jnp.zeros((hw, LANES), F32)
    for g in range(len(DIL_GROUPS)):
        alpha = _head_expand(jnp.broadcast_to(es[g] / tot, (DIL_HEADS, LANES)))
        mix = mix + alpha * jnp.broadcast_to(outs[g], (hw, LANES))
    rows = jnp.concatenate([mix[c * LANES:(c + 1) * LANES, :].T for c in range(hw // LANES)], axis=1)
    o_ref[...] = rows[0:1]


def dil_attend_sample(q, news, states_t):
    bs = q.shape[0]
    hw = DIL_HEADS * HEAD_DIM
    st_specs = [pl.BlockSpec((None, 2 * hw, st.shape[2]), lambda i: (i, 0, 0)) for st in states_t]
    outs = pl.pallas_call(
        _dil_sample_kernel,
        grid=(bs,),
        in_specs=[pl.BlockSpec((None, 1, q.shape[2]), lambda i: (i, 0, 0))]
                 + [pl.BlockSpec((None, 1, 2 * hw), lambda i: (i, 0, 0))] * 3 + st_specs,
        out_specs=[pl.BlockSpec((None, 1, hw), lambda i: (i, 0, 0))] + st_specs,
        out_shape=[jax.ShapeDtypeStruct((bs, 1, hw), F32)]
                  + [jax.ShapeDtypeStruct(st.shape, F32) for st in states_t],
        compiler_params=_cparams("parallel"),
        name="dil_attend_sample",
    )(q, *news, *states_t)
    return outs[0], outs[1:]


AB_SIZES = (NSA_HEADS * HEAD_DIM, 6 * NSA_KV_HEADS * HEAD_DIM, 3 * NSA_HEADS,
            DIFF_HEADS * 2 * HEAD_DIM, DIFF_HEADS * 2 * HEAD_DIM, DIFF_HEADS * 2 * HEAD_DIM)
_QSCALE = HEAD_DIM ** -0.5
_QSCALE2 = _QSCALE * LOG2E


def _rope_tables(pos):
    half = HEAD_DIM // 2
    inv = ROPE_THETA ** (-jnp.arange(half, dtype=F32) / half)
    ang = pos.astype(F32)[:, None] * inv[None, :]
    c, s = jnp.cos(ang), jnp.sin(ang)
    return jnp.tile(c, (1, 4)), jnp.tile(jnp.concatenate([-s, s], axis=1), (1, 2))


def _prep_w_ab(w):
    d = w.shape[0]
    qa, kvb, gl, qd, kd, vd = jnp.split(w, np.cumsum(AB_SIZES)[:-1].tolist(), axis=1)
    kvb = kvb.reshape(d, 6, LANES)
    k3 = kvb[:, 0::2].reshape(d, 3 * LANES)
    v3 = kvb[:, 1::2].reshape(d, 3 * LANES)
    gl = jnp.pad(gl, ((0, 0), (0, LANES - gl.shape[1])))
    return jnp.concatenate([qa, k3, v3, qd, kd, vd, gl], axis=1).astype(BF16)


def _ab_plan(dest, qscale):
    return (
        (0, 4, True, qscale, "half", tuple(dest("qa", j) for j in range(8))),
        (512, 3, True, 1.0, None, (dest("k_cmp", 0), dest("k_slc", 0), dest("k_win", 0))),
        (896, 3, False, 1.0, None, (dest("v_cmp", 0), dest("v_slc", 0), dest("v_win", 0))),
        (1280, 4, True, qscale, "pair", tuple(dest("qd", j) for j in range(8))),
        (1792, 4, True, 1.0, None, tuple(dest("kd", j) for j in range(4))),
        (2304, 4, False, 1.0, None, tuple(dest("vd", j) for j in range(4))),
        (2816, 1, False, 1.0, None, (dest("gate", 0),)),
    )


def _ab_prompt_defs(t):
    defs = (("T", 1024, BF16, t), ("T", 512, F32, t), ("N", 128, BF16), ("N", 256, F32),
            ("T", 256, F32, min(NSA_WINDOW, t)), ("N", 128, BF16), ("T", 1024, BF16, t), ("I", 8, F32),
            ("N", 512, BF16), ("T", 128, F32, t), ("T", 128, BF16, t), ("T", 128, BF16, t), ("T", 512, BF16, t))
    table = {
        "qa": lambda j: ((0, j),), "qd": lambda j: ((6, j),), "gate": lambda j: ((9, 0),),
        "k_cmp": lambda j: ((1, 0), (3, 0)), "v_cmp": lambda j: ((1, 1), (3, 1)),
        "k_slc": lambda j: ((1, 2), (2, 0)), "v_slc": lambda j: ((1, 3), (11, 0)),
        "k_win": lambda j: ((4, 0), (5, 0)), "v_win": lambda j: ((4, 1), (10, 0)),
        "kd": lambda j: ((7, j), (8, j)), "vd": lambda j: ((7, 4 + j), (12, j)),
    }
    return defs, _ab_plan(lambda name, j: table[name](j), _QSCALE2)


def _ab_sample_defs():
    defs = (("N", 1024, BF16), ("N", 512, F32), ("N", 256, F32), ("N", 1024, BF16), ("N", 1024, F32),
            ("N", 128, F32))
    table = {
        "qa": lambda j: ((0, j),), "qd": lambda j: ((3, j),), "gate": lambda j: ((5, 0),),
        "k_cmp": lambda j: ((1, 0),), "v_cmp": lambda j: ((1, 1),),
        "k_slc": lambda j: ((1, 2),), "v_slc": lambda j: ((1, 3),),
        "k_win": lambda j: ((2, 0),), "v_win": lambda j: ((2, 1),),
        "kd": lambda j: ((4, j),), "vd": lambda j: ((4, 4 + j),),
    }
    return defs, _ab_plan(lambda name, j: table[name](j), _QSCALE)


def _c_prompt_defs(t):
    defs, plan = [], []
    for g, (win, dil) in enumerate(DIL_GROUPS):
        defs += [("R", 1024, BF16, dil), ("R", 1024, BF16, dil), ("T", 1024, F32, min(win, t))]
        plan += [
            (g * 1536, 4, True, _QSCALE2, "pair", tuple(((3 * g, j),) for j in range(8))),
            (g * 1536 + 512, 4, True, 1.0, None, tuple(((3 * g + 1, j), (3 * g + 2, j)) for j in range(4))),
            (g * 1536 + 1024, 4, False, 1.0, None,
             tuple(((3 * g + 1, 4 + j), (3 * g + 2, 4 + j)) for j in range(4))),
        ]
    return tuple(defs), tuple(plan)


def _c_sample_defs():
    defs = (("N", 1536, F32), ("N", 1024, F32), ("N", 1024, F32), ("N", 1024, F32))
    plan = []
    for g in range(len(DIL_GROUPS)):
        plan += [
            (g * 1536, 4, True, _QSCALE, None, tuple(((0, 4 * g + j),) for j in range(4))),
            (g * 1536 + 512, 4, True, 1.0, None, tuple(((1 + g, j),) for j in range(4))),
            (g * 1536 + 1024, 4, False, 1.0, None, tuple(((1 + g, 4 + j),) for j in range(4))),
        ]
    return defs, tuple(plan)


def _prep_cmp(w_cmp, pe_cmp):
    wk, wv = w_cmp[0], w_cmp[1]
    z = jnp.zeros_like(wk)
    w4 = jnp.concatenate([jnp.concatenate([wk, z, z, z], axis=-1), jnp.concatenate([z, wk, z, z], axis=-1),
                          jnp.concatenate([z, z, wv, z], axis=-1), jnp.concatenate([z, z, z, wv], axis=-1)],
                         axis=1).astype(BF16)
    pe4 = jnp.concatenate([pe_cmp[0], pe_cmp[0], pe_cmp[1], pe_cmp[1]], axis=-1)
    half = CMP_LEN // 2
    return pe4, w4[:half], w4[half:]


def _block_indicator(n_keys):
    return (jnp.arange(n_keys)[:, None] // SLC_BLOCK == jnp.arange(LANES)[None, :]).astype(BF16)


def _cmp_to_block(n_cmp):
    r = SLC_BLOCK // CMP_STRIDE
    return (jnp.arange(n_cmp)[None, :] // r == jnp.arange(LANES)[:, None]).astype(BF16)


def _rows_from_t(x_t, lead):
    b, _, r = x_t.shape
    nd = len(lead)
    return x_t.reshape((b,) + tuple(lead) + (HEAD_DIM, r)).transpose((0, nd + 2) + tuple(range(1, nd + 2)))


def _rows_to_t(x):
    b, r = x.shape[:2]
    nd = x.ndim
    return x.transpose((0,) + tuple(range(2, nd)) + (1,)).reshape(b, -1, r)


def ab_mix_prompt(h, b, t, g_in, w_ab, cmp_prep, lam_vec, dn_g, lam_init, cos, sin):
    defs, plan = _ab_prompt_defs(t)
    (q_nsa_t, rows_nsa_t, kslc16, cmp_rows, rows_win_t, kwin16, q_diff_t, rows_diff, k_diff16, gate_t, vwin_t,
     vslc_t, v_diff_t) = project(h, b, g_in, cos, sin, w_ab, plan, defs)
    r3 = lambda x: x.reshape(b, t, x.shape[-1])
    pe4, wa, wb = cmp_prep
    kcmp16, vcmp_t = nsa_compress_prompt(r3(cmp_rows), pe4, wa, wb)
    o_nsa = nsa_attend_prompt(q_nsa_t, gate_t, kcmp16, vcmp_t, r3(kslc16), vslc_t, r3(kwin16), vwin_t,
                              _block_indicator(t), _cmp_to_block(t // CMP_STRIDE))
    o_diff = diff_attend_prompt(q_diff_t, r3(k_diff16), v_diff_t, lam_vec, dn_g, lam_init)
    mixed = [o_nsa.reshape(b * t, -1), o_diff.reshape(b * t, -1)]
    return mixed, rows_nsa_t, rows_win_t, rows_diff.reshape(b, t, 2, DIFF_HEADS, 2 * HEAD_DIM)


def dil_mix_prompt(h, b, t, g_in, w_c, cos, sin):
    defs, plan = _c_prompt_defs(t)
    outs = project(h, b, g_in, cos, sin, w_c, plan, defs)
    os_, lses, rows_t = [], [], []
    for gi in range(len(DIL_GROUPS)):
        o, lse = dil_attend_prompt(outs[3 * gi], outs[3 * gi + 1])
        os_.append(o)
        lses.append(lse)
        rows_t.append(outs[3 * gi + 2])
    return os_, lses, rows_t


def _diff_qt(q_diff):
    bs = q_diff.shape[0]
    qd = q_diff.reshape(bs, DIFF_HEADS, 2, LANES).transpose(0, 2, 1, 3)
    eye = jnp.eye(DIFF_HEADS, dtype=q_diff.dtype)
    return (qd[:, :, :, None, :] * eye[None, None, :, :, None]).reshape(bs, 2 * DIFF_HEADS, DIFF_HEADS * LANES)


def ab_mix_sample(hs, g_in, w_ab, cmp_prep, lam_vec, dn_g, lam_init, cos, sin,
                  cache_nsa, cache_diff, win_state, page_table):
    bs = hs.shape[0]
    defs, plan = _ab_sample_defs()
    q_nsa, rows_nsa, rows_win, q_diff, rows_diff, gate = project(hs, 1, g_in, cos, sin, w_ab, plan, defs)
    n_pages = page_table.shape[1]
    past = n_pages * PAGE_SIZE
    assert past // SLC_BLOCK == LANES, "selection-block axis is laid out on the 128 lanes"
    n_pool = cache_nsa.shape[0]
    cache_t = _rows_to_t(cache_nsa)
    cache_v = cache_diff.reshape(n_pool, PAGE_SIZE * DIFF_ROW_STRIDE, LANES)
    pe4, wa, wb = cmp_prep
    cmp_kv = nsa_compress_sample(cache_t, page_table, pe4, wa, wb)
    q8 = q_nsa.reshape(bs, NSA_HEADS, LANES)
    o_cmp, imp = nsa_sample_cmp(q8, cmp_kv, _cmp_to_block(past // CMP_STRIDE), past)
    sel_idx = topk_lanes(imp, N_SEL_CACHE)[:, :NSA_KV_HEADS, :N_SEL_CACHE].reshape(bs, -1)
    new_rows = jnp.concatenate([rows_nsa, rows_win], axis=-1).reshape(bs, 1, -1)
    o8 = nsa_sample_attend(page_table, sel_idx, q8, gate.reshape(bs, 1, LANES), o_cmp, new_rows,
                           _rows_to_t(win_state), cache_t)
    o8 = o8.reshape(bs, NSA_HEADS, 2, HEAD_DIM)
    o_nsa = jnp.concatenate([o8[:, :NSA_GROUP, 0], o8[:, NSA_GROUP:, 1]], axis=1).reshape(bs, -1)
    o_diff = diff_attend_sample(page_table, _diff_qt(q_diff), rows_diff.reshape(bs, 1, -1), lam_vec, dn_g,
                                cache_v, lam_init).reshape(bs, -1)
    mixed = jnp.concatenate([o_nsa, o_diff], axis=-1).astype(BF16)
    return mixed, rows_nsa, rows_win, rows_diff


def dil_mix_sample(hs, g_in, w_c_rows, cos, sin, states):
    bs = hs.shape[0]
    defs, plan = _c_sample_defs()
    outs = project(hs, 1, g_in, cos, sin, w_c_rows, plan, defs)
    news = [x.reshape(bs, 1, -1) for x in outs[1:]]
    sts = []
    for (win, dil), st in zip(DIL_GROUPS, states):
        assert st.shape[1] == win and win == DIL_BAND * dil, "state buffer must hold the full dilated window"
        sts.append(_rows_to_t(st))
    o, new_sts = dil_attend_sample(outs[0].reshape(bs, 1, -1), news, sts)
    return o.reshape(bs, -1).astype(BF16), [_rows_from_t(x, (2, DIL_HEADS)) for x in new_sts]


def kernel(x_prompt, x_sample, cache_nsa, cache_diff, state_nsa_win, state_dil_0, state_dil_1, state_dil_2,
           page_table, norm_g, ffn_w_in, ffn_w_out, w_in_ab, w_out_ab, nsa_w_cmp, nsa_pe_cmp, diff_lambda,
           diff_norm_g, w_in_c, w_out_c):
    b, t, d = x_prompt.shape
    bs, ns, _ = x_sample.shape
    assert ns == 1, "sample group is one new token per sequence"
    depth = norm_g.shape[0]
    past = page_table.shape[1] * PAGE_SIZE
    hp = x_prompt.reshape(b * t, d)
    hs = x_sample.reshape(bs, d)
    cos_p, sin_p = _rope_tables(jnp.tile(jnp.arange(t, dtype=jnp.int32), b))
    cos_s, sin_s = _rope_tables(jnp.full((bs,), past, jnp.int32))
    state_dil = (state_dil_0, state_dil_1, state_dil_2)
    nsa_p, nsa_s, win_p, win_s, diff_p, diff_s = [], [], [], [], [], []
    dil_p = [[] for _ in DIL_GROUPS]
    dil_s = [[] for _ in DIL_GROUPS]
    for layer in range(depth):
        g = norm_g[layer]
        hp = ffn_half(hp, g[0], g[1], ffn_w_in, ffn_w_out, layer, 0)
        hs = ffn_half(hs, g[0], g[1], ffn_w_in, ffn_w_out, layer, 0)
        if layer % 2 == 0:
            e = layer // 2
            lam_init = 0.8 - 0.6 * math.exp(-0.3 * layer)
            w_ab = _prep_w_ab(w_in_ab[e])
            cmp_prep = _prep_cmp(nsa_w_cmp[e], nsa_pe_cmp[e])
            mp, rn_t, rw_t, rd = ab_mix_prompt(hp, b, t, g[2], w_ab, cmp_prep, diff_lambda[e], diff_norm_g[e],
                                               lam_init, cos_p, sin_p)
            nsa_p.append(_rows_from_t(rn_t, (4, NSA_KV_HEADS)))
            win_p.append(_rows_from_t(rw_t[:, :, -min(NSA_WINDOW, t):], (2, NSA_KV_HEADS)))
            diff_p.append(rd.reshape(b, t, 2, DIFF_HEADS, 2 * HEAD_DIM))
            ms, rn, rw, rd = ab_mix_sample(hs, g[2], w_ab, cmp_prep, diff_lambda[e], diff_norm_g[e], lam_init,
                                           cos_s, sin_s, cache_nsa[e], cache_diff[e], state_nsa_win[e], page_table)
            nsa_s.append(rn.reshape(bs, 1, 4, NSA_KV_HEADS, HEAD_DIM))
            win_full = jnp.concatenate([state_nsa_win[e], rw.reshape(bs, 1, 2, NSA_KV_HEADS, HEAD_DIM)], axis=1)
            win_s.append(win_full[:, -min(NSA_WINDOW, win_full.shape[1]):])
            diff_s.append(rd.reshape(bs, 1, 2, DIFF_HEADS, 2 * HEAD_DIM))
            w_o = w_out_ab[e].astype(BF16)
            hp = outproj(hp, mp, w_o, g[3])
            hs = outproj(hs, [ms], w_o, g[3])
        else:
            o = layer // 2
            w_o = w_out_c[o].astype(BF16)
            w_c = w_in_c[o].astype(BF16)
            os_, lses, rows_t = dil_mix_prompt(hp, b, t, g[2], w_c, cos_p, sin_p)
            hp = outproj_dil(hp, b, os_, lses, w_o, g[3])
            ms, new_states = dil_mix_sample(hs, g[2], w_c, cos_s, sin_s, [st[o] for st in state_dil])
            hs = outproj(hs, [ms], w_o, g[3])
            for gi, (win, dil) in enumerate(DIL_GROUPS):
                dil_p[gi].append(_rows_from_t(rows_t[gi][:, :, -min(win, t):], (2, DIL_HEADS)))
                dil_s[gi].append(new_states[gi])
        hp = ffn_half(hp, g[4], g[5], ffn_w_in, ffn_w_out, layer, 1)
        hs = ffn_half(hs, g[4], g[5], ffn_w_in, ffn_w_out, layer, 1)
    return (hp.reshape(b, t, d), hs.reshape(bs, 1, d), jnp.stack(nsa_p), jnp.stack(nsa_s), jnp.stack(win_p),
            jnp.stack(win_s), jnp.stack(diff_p), jnp.stack(diff_s), jnp.stack(dil_p[0]), jnp.stack(dil_s[0]),
            jnp.stack(dil_p[1]), jnp.stack(dil_s[1]), jnp.stack(dil_p[2]), jnp.stack(dil_s[2]))
```

```python
import functools
import math

import jax
import jax.numpy as jnp
import numpy as np
from jax import lax
from jax.experimental import pallas as pl
from jax.experimental.pallas import tpu as pltpu

F32 = jnp.float32
BF16 = jnp.bfloat16

LANES = 128
SUBLANES = 8
HEAD_DIM = 64
ROPE_THETA = 10000.0
NORM_EPS = 1e-6
PAGE_SIZE = 128
NSA_HEADS = 8
NSA_KV_HEADS = 2
NSA_GROUP = NSA_HEADS // NSA_KV_HEADS
CMP_LEN = 32
CMP_STRIDE = 16
SLC_BLOCK = 64
N_SEL = 16
NSA_WINDOW = 512
FORCED_BLOCK_SCORE = 1.0e4
DIFF_HEADS = 4
DIFF_NORM_EPS = 1e-5
DIL_GROUPS = ((128, 1), (512, 4), (2048, 16))
DIL_HEADS = 8
DIL_BAND = 128
NEG_BIG = -1.0e30
LN2 = math.log(2.0)
LOG2E = 1.0 / LN2
VMEM_LIMIT_BYTES = 56 * 1024 * 1024
TOKEN_TILE = 512


def _cparams(*sem):
    return pltpu.CompilerParams(dimension_semantics=sem, vmem_limit_bytes=VMEM_LIMIT_BYTES)


def _const_spec(shape):
    nd = len(shape)
    return pl.BlockSpec(shape, lambda *_: (0,) * nd, pipeline_mode=pl.Buffered(1))


def _rms(x, g, eps):
    return x * lax.rsqrt(jnp.mean(x * x, axis=-1, keepdims=True) + eps) * g


def _dot(a, b):
    return jnp.dot(a, b, preferred_element_type=F32)


def _dot_nt(a, b):
    return lax.dot_general(a, b, (((1,), (1,)), ((), ())), preferred_element_type=F32)


def _split3(x):
    hi = x.astype(BF16)
    r1 = x - hi.astype(F32)
    mid = r1.astype(BF16)
    lo = (r1 - mid.astype(F32)).astype(BF16)
    return hi, mid, lo


def _pad_rows(x, rows):
    return jnp.concatenate([x, jnp.zeros((rows - x.shape[0], x.shape[1]), x.dtype)], axis=0)


def _token_tile(n):
    return TOKEN_TILE if n % TOKEN_TILE == 0 else n


FFN_CHUNK = 256


def _ffn_kernel(x_ref, gpre_ref, gpost_ref, win_ref, wout_ref, o_ref, *, d_ff):
    x = x_ref[...]
    xn = _rms(x, gpre_ref[...], NORM_EPS).astype(BF16)
    acc = jnp.zeros(x.shape, F32)
    for c in range(d_ff // FFN_CHUNK):
        lo = c * FFN_CHUNK
        gate = _dot(xn, win_ref[:, lo:lo + FFN_CHUNK].astype(BF16))
        up = _dot(xn, win_ref[:, d_ff + lo:d_ff + lo + FFN_CHUNK].astype(BF16))
        act = (gate * jax.nn.sigmoid(gate) * up).astype(BF16)
        acc = acc + _dot(act, wout_ref[lo:lo + FFN_CHUNK, :].astype(BF16))
    o_ref[...] = x + 0.5 * _rms(acc, gpost_ref[...], NORM_EPS)


def ffn_half(h, g_pre, g_post, w_in_all, w_out_all, layer, which):
    n, d = h.shape
    d_ff = w_out_all.shape[2]
    tm = _token_tile(n)
    pick = lambda shape: pl.BlockSpec((None, None) + tuple(shape[2:]), lambda i: (layer, which, 0, 0),
                                      pipeline_mode=pl.Buffered(1))
    return pl.pallas_call(
        functools.partial(_ffn_kernel, d_ff=d_ff),
        grid=(n // tm,),
        in_specs=[pl.BlockSpec((tm, d), lambda i: (i, 0)),
                  _const_spec((1, d)), _const_spec((1, d)),
                  pick(w_in_all.shape), pick(w_out_all.shape)],
        out_specs=pl.BlockSpec((tm, d), lambda i: (i, 0)),
        out_shape=jax.ShapeDtypeStruct((n, d), F32),
        compiler_params=_cparams("parallel"),
        name="ffn_half",
    )(h, g_pre.reshape(1, d), g_post.reshape(1, d), w_in_all, w_out_all)


def _rope_slab(y, cos, sin):
    lane = lax.broadcasted_iota(jnp.int32, y.shape, 1)
    swapped = jnp.where(lane % HEAD_DIM < HEAD_DIM // 2,
                        pltpu.roll(y, LANES - HEAD_DIM // 2, 1),
                        pltpu.roll(y, HEAD_DIM // 2, 1))
    return y * cos + swapped * sin


def _proj_kernel(x_ref, g_ref, cos_ref, sin_ref, w_ref, *refs, plan, out_defs, first_tiles, tiles_per_b):
    out_refs = refs[:len(out_defs)]
    scr_ref = refs[len(out_defs)]
    tm = x_ref.shape[0]
    tile_in_b = pl.program_id(0) % tiles_per_b
    xn = _rms(x_ref[...], g_ref[...], NORM_EPS).astype(BF16)
    cos = cos_ref[...]
    sin = sin_ref[...]
    lane = lax.broadcasted_iota(jnp.int32, (tm, LANES), 1)

    def emit(val, val_t, out_idx, slab):
        ref = out_refs[out_idx]
        kind = out_defs[out_idx][0]
        cs = slice(slab * LANES, (slab + 1) * LANES)
        if kind == "N":
            ref[:, cs] = val().astype(ref.dtype)
        elif kind == "I":
            ref[pl.ds(slab, tm, stride=out_defs[out_idx][1]), :] = val().astype(ref.dtype)
        elif kind == "T":
            def write_t():
                ref[cs, :] = val_t().astype(ref.dtype)
            if first_tiles[out_idx] == 0:
                write_t()
            else:
                pl.when(tile_in_b >= first_tiles[out_idx])(write_t)
        else:
            dil = out_defs[out_idx][3]
            if dil == 1:
                ref[0, :, cs] = val().astype(ref.dtype)
            else:
                scr_ref[...] = val()
                for r in range(dil):
                    ref[r, :, cs] = scr_ref[pl.ds(r, tm // dil, stride=dil), :].astype(ref.dtype)

    for col0, nslab, rope, scale, pad, dests in plan:
        y = _dot(xn, w_ref[:, col0:col0 + nslab * LANES])
        for j in range(nslab):
            ys = y[:, j * LANES:(j + 1) * LANES]
            if rope:
                ys = _rope_slab(ys, cos, sin)
            if scale != 1.0:
                ys = ys * scale
            if pad is None:
                for out_idx, slab in dests[j]:
                    emit(lambda ys=ys: ys, lambda ys=ys: ys.T, out_idx, slab)
                continue
            transposed = []

            def ys_t(ys=ys, transposed=transposed):
                if not transposed:
                    transposed.append(ys.T)
                return transposed[0]

            for hh in range(2):
                head = 2 * j + hh
                at_hi = hh == 1 if pad == "pair" else head >= nslab

                def val(ys=ys, hh=hh, at_hi=at_hi):
                    v = ys if at_hi == (hh == 1) else pltpu.roll(ys, HEAD_DIM, 1)
                    return jnp.where(lane >= HEAD_DIM if at_hi else lane < HEAD_DIM, v, 0.0)

                def val_t(ys_t=ys_t, hh=hh, at_hi=at_hi):
                    rows = ys_t()[hh * HEAD_DIM:(hh + 1) * HEAD_DIM, :]
                    zero = jnp.zeros_like(rows)
                    return jnp.concatenate([zero, rows] if at_hi else [rows, zero], axis=0)

                for out_idx, slab in dests[head]:
                    emit(val, val_t, out_idx, slab)


def project(h, b, g, cos, sin, w, plan, out_defs):
    n, d = h.shape
    t = n // b
    tm = _token_tile(t)
    tpb = t // tm
    specs, shapes, first_tiles = [], [], []
    for od in out_defs:
        kind, c, dt = od[:3]
        if kind == "N":
            specs.append(pl.BlockSpec((tm, c), lambda i: (i, 0)))
            shapes.append(jax.ShapeDtypeStruct((n, c), dt))
            first_tiles.append(0)
        elif kind == "I":
            specs.append(pl.BlockSpec((tm * c, LANES), lambda i: (i, 0)))
            shapes.append(jax.ShapeDtypeStruct((n * c, LANES), dt))
            first_tiles.append(0)
        elif kind == "T":
            keep = max(min(od[3], t), tm)
            ft = (t - keep) // tm
            specs.append(pl.BlockSpec((None, c, tm),
                                      functools.partial(lambda i, ft: (i // tpb, 0, jnp.maximum(i % tpb - ft, 0)),
                                                        ft=ft)))
            shapes.append(jax.ShapeDtypeStruct((b, c, keep), dt))
            first_tiles.append(ft)
        else:
            dil = od[3]
            specs.append(pl.BlockSpec((None, dil, tm // dil, c), lambda i: (i // tpb, 0, i % tpb, 0)))
            shapes.append(jax.ShapeDtypeStruct((b, dil, t // dil, c), dt))
            first_tiles.append(0)
    return pl.pallas_call(
        functools.partial(_proj_kernel, plan=plan, out_defs=out_defs, first_tiles=tuple(first_tiles),
                          tiles_per_b=tpb),
        grid=(n // tm,),
        in_specs=[pl.BlockSpec((tm, d), lambda i: (i, 0)), _const_spec((1, d)),
                  pl.BlockSpec((tm, LANES), lambda i: (i, 0)),
                  pl.BlockSpec((tm, LANES), lambda i: (i, 0)),
                  _const_spec(w.shape)],
        out_specs=specs,
        out_shape=shapes,
        scratch_shapes=[pltpu.VMEM((tm, LANES), F32)],
        compiler_params=_cparams("arbitrary"),
        name="project",
    )(h, g.reshape(1, d), cos, sin, w)


def _outproj_kernel(h_ref, *refs):
    w_ref, g_ref, o_ref = refs[-3:]
    y, row0 = None, 0
    for m_ref in refs[:-3]:
        c = m_ref.shape[1]
        part = _dot(m_ref[...], w_ref[row0:row0 + c, :])
        y = part if y is None else y + part
        row0 += c
    o_ref[...] = h_ref[...] + _rms(y, g_ref[...], NORM_EPS)


def outproj(h, ms, w, g):
    n, d = h.shape
    tm = _token_tile(n)
    return pl.pallas_call(
        _outproj_kernel,
        grid=(n // tm,),
        in_specs=[pl.BlockSpec((tm, d), lambda i: (i, 0))]
                 + [pl.BlockSpec((tm, m.shape[1]), lambda i: (i, 0)) for m in ms]
                 + [_const_spec(w.shape), _const_spec((1, d))],
        out_specs=pl.BlockSpec((tm, d), lambda i: (i, 0)),
        out_shape=jax.ShapeDtypeStruct((n, d), F32),
        compiler_params=_cparams("parallel"),
        name="outproj",
    )(h, *ms, w, g.reshape(1, d))


def _outproj_dil_kernel(h_ref, *refs):
    ng = len(DIL_GROUPS)
    w_ref, g_ref, o_ref, scr_ref = refs[2 * ng:]
    tm = h_ref.shape[0]
    nslab = DIL_HEADS * HEAD_DIM // LANES
    vals = []
    k = 0
    for gi, (_, dil) in enumerate(DIL_GROUPS):
        per_g = []
        for ref in (refs[2 * gi], refs[2 * gi + 1]):
            slabs = []
            for s in range(nslab):
                cs = slice(s * LANES, (s + 1) * LANES)
                if dil == 1:
                    slabs.append(ref[0, :, cs])
                else:
                    for r in range(dil):
                        scr_ref[k, pl.ds(r, tm // dil, stride=dil), :] = ref[r, :, cs]
                    slabs.append(scr_ref[k])
                    k += 1
            per_g.append(slabs)
        vals.append(per_g)
    mixed = []
    for s in range(nslab):
        l0, l1, l2 = vals[0][1][s], vals[1][1][s], vals[2][1][s]
        mx = jnp.maximum(jnp.maximum(l0, l1), l2)
        e0, e1, e2 = jnp.exp(l0 - mx), jnp.exp(l1 - mx), jnp.exp(l2 - mx)
        den = e0 + e1 + e2
        mixed.append(((e0 / den) * vals[0][0][s] + (e1 / den) * vals[1][0][s]
                      + (e2 / den) * vals[2][0][s]).astype(BF16))
    y = _dot(jnp.concatenate(mixed, axis=1), w_ref[...])
    o_ref[...] = h_ref[...] + _rms(y, g_ref[...], NORM_EPS)


def outproj_dil(h, b, outs, lses, w, g):
    n, d = h.shape
    t = n // b
    tm = _token_tile(t)
    tpb = t // tm
    c = DIL_HEADS * HEAD_DIM
    specs, args, n_scr = [], [], 0
    for (_, dil), o, l in zip(DIL_GROUPS, outs, lses):
        spec = pl.BlockSpec((None, dil, tm // dil, c), lambda i: (i // tpb, 0, i % tpb, 0))
        specs += [spec, spec]
        args += [o, l]
        if dil > 1:
            n_scr += 2 * (c // LANES)
    return pl.pallas_call(
        _outproj_dil_kernel,
        grid=(n // tm,),
        in_specs=[pl.BlockSpec((tm, d), lambda i: (i, 0))] + specs + [_const_spec(w.shape), _const_spec((1, d))],
        out_specs=pl.BlockSpec((tm, d), lambda i: (i, 0)),
        out_shape=jax.ShapeDtypeStruct((n, d), F32),
        scratch_shapes=[pltpu.VMEM((n_scr, tm, LANES), F32)],
        compiler_params=_cparams("parallel"),
        name="outproj_dil",
    )(h, *args, w, g.reshape(1, d))


def _compress_rows(k_ref, v_ref, pe_ref, wa_ref, wb_ref, nsub):
    half = CMP_LEN // 2
    acc_a = jnp.zeros((nsub, 2 * LANES), F32)
    acc_b = jnp.zeros((nsub, 2 * LANES), F32)
    for l in range(half):
        x = jnp.concatenate([k_ref[pl.ds(l, nsub, stride=CMP_STRIDE), :],
                             v_ref[pl.ds(l, nsub, stride=CMP_STRIDE), :]], axis=1)
        acc_a = acc_a + _dot((x + pe_ref[l:l + 1, :]).astype(BF16), wa_ref[l])
        acc_b = acc_b + _dot((x + pe_ref[half + l:half + l + 1, :]).astype(BF16), wb_ref[l])
    comp = acc_a + pltpu.roll(acc_b, nsub - 1, 0)
    row = lax.broadcasted_iota(jnp.int32, comp.shape, 0)
    return jnp.where(row < nsub - 1, comp, 0.0)


def _compress_kernel(k_ref, v_ref, pe_ref, wa_ref, wb_ref, k_out_ref, vt_out_ref, *, nsub):
    comp = _compress_rows(k_ref, v_ref, pe_ref, wa_ref, wb_ref, nsub)
    k_out_ref[...] = comp[:, 0:LANES].astype(k_out_ref.dtype)
    vt_out_ref[...] = comp[:, LANES:2 * LANES].T.astype(vt_out_ref.dtype)


def nsa_compress_prompt(cmp_rows, pe4, wa, wb):
    b, t, _ = cmp_rows.shape
    nsub = t // CMP_STRIDE
    return pl.pallas_call(
        functools.partial(_compress_kernel, nsub=nsub),
        grid=(b,),
        in_specs=[pl.BlockSpec((None, t, LANES), lambda i: (i, 0, 0)),
                  pl.BlockSpec((None, t, LANES), lambda i: (i, 0, 1)),
                  _const_spec(pe4.shape), _const_spec(wa.shape), _const_spec(wb.shape)],
        out_specs=[pl.BlockSpec((None, nsub, LANES), lambda i: (i, 0, 0)),
                   pl.BlockSpec((None, LANES, nsub), lambda i: (i, 0, 0))],
        out_shape=[jax.ShapeDtypeStruct((b, nsub, LANES), BF16), jax.ShapeDtypeStruct((b, LANES, nsub), BF16)],
        compiler_params=_cparams("parallel"),
        name="nsa_compress_prompt",
    )(cmp_rows, cmp_rows, pe4, wa, wb)


NSA_TQ = 256
NSA_TK = 512


def _softmax_terms(s, mask, exp_fn):
    s = jnp.where(mask, s, -jnp.inf)
    m = jnp.max(s, axis=-1, keepdims=True)
    m = jnp.where(m == -jnp.inf, 0.0, m)
    e = exp_fn(s - m)
    den = jnp.sum(e, axis=-1, keepdims=True)
    return e, 1.0 / jnp.where(den > 0, den, 1.0)


def _softmax_rows(s, mask, exp_fn=jnp.exp):
    e, inv = _softmax_terms(s, mask, exp_fn)
    return e * inv


def _topk_mask_t(score_t, k):
    j_io = lax.broadcasted_iota(jnp.int32, score_t.shape, 0)
    nj = score_t.shape[0]
    work = score_t
    for _ in range(k):
        m = jnp.max(work, axis=0, keepdims=True)
        jmin = jnp.min(jnp.where(work == m, j_io, nj), axis=0, keepdims=True)
        work = jnp.where(j_io == jmin, -jnp.inf, work)
    return jnp.logical_and(work == -jnp.inf, score_t > -jnp.inf)


SWEEP_CHUNKS = 4


def _causal_sweep_t(score_fn, pv_fn, mask_fn, n_full, cols, chunks=None):
    nc = chunks or SWEEP_CHUNKS
    cc = cols // nc

    def update(tiles, masked, state):
        scs = [[score_fn(t, c) for t in tiles] for c in range(nc)]
        out = []
        for c, (m, l, acc) in enumerate(state):
            sc = scs[c]
            if masked:
                sc = [jnp.where(mask_fn(t, c), s, NEG_BIG) for t, s in zip(tiles, sc)]
            m_new = m
            for s in sc:
                m_new = jnp.maximum(m_new, jnp.max(s, axis=0, keepdims=True))
            alpha = jnp.exp2(m - m_new)
            l = alpha * l
            acc = alpha * acc
            for t, s in zip(tiles, sc):
                pe = jnp.exp2(s - m_new)
                l = l + jnp.sum(pe, axis=0, keepdims=True)
                acc = acc + pv_fn(t, pe.astype(BF16))
            out.append((m_new, l, acc))
        return out

    init = [(jnp.full((1, cc), NEG_BIG, F32), jnp.zeros((1, cc), F32), jnp.zeros((LANES, cc), F32))
            for _ in range(nc)]
    state = lax.fori_loop(0, n_full // 4, lambda u, st: update([4 * u + k for k in range(4)], False, st), init)
    base2 = (n_full // 4) * 4
    state = lax.fori_loop(0, (n_full % 4) // 2, lambda _, st: update([base2, base2 + 1], False, st), state)
    state = lax.fori_loop(0, n_full % 2, lambda _, st: update([n_full - 1], False, st), state)
    state = update([n_full], True, state)
    return (jnp.concatenate([s[1] for s in state], axis=1), jnp.concatenate([s[2] for s in state], axis=1))


def _softmax_terms_t(s, mask):
    s = jnp.where(mask, s, -jnp.inf)
    m = jnp.max(s, axis=0, keepdims=True)
    m = jnp.where(m == -jnp.inf, 0.0, m)
    e = jnp.exp2(s - m)
    den = jnp.sum(e, axis=0, keepdims=True)
    return e, 1.0 / jnp.where(den > 0, den, 1.0)


def _nsa_prompt_kernel(qt_ref, gt_ref, kcmp_ref, vcmp_ref, kslc_ref, vslc_ref, kwin_ref, vwin_ref, eall_ref,
                       selt_ref, o_ref):
    tq = NSA_TQ
    qs = pl.program_id(1) * tq
    rows = NSA_GROUP * tq
    qpos = qs + lax.broadcasted_iota(jnp.int32, (1, rows), 1) % tq
    ncmp = kcmp_ref.shape[0]
    gates_t = jax.nn.sigmoid(gt_ref[...])
    head_out = []
    wlen = NSA_WINDOW + tq
    ws = pl.multiple_of(jnp.maximum(qs - NSA_WINDOW, 0), tq)
    per_kvh = []
    for kvh in range(NSA_KV_HEADS):
        q4t = jnp.concatenate(
            [qt_ref[(NSA_GROUP * kvh + g) * LANES:(NSA_GROUP * kvh + g + 1) * LANES, :]
             for g in range(NSA_GROUP)], axis=1)
        s = _dot(kcmp_ref[...], q4t)
        sw = _dot(kwin_ref[pl.ds(ws, wlen), :], q4t)
        cmp_end = lax.broadcasted_iota(jnp.int32, (ncmp, 1), 0) * CMP_STRIDE + (CMP_LEN - 1)
        e, inv = _softmax_terms_t(s, cmp_end <= qpos)
        p = e * inv
        o_cmp_t = _dot(vcmp_ref[...], p.astype(BF16))
        psum = p[:, 0:tq] + p[:, tq:2 * tq] + p[:, 2 * tq:3 * tq] + p[:, 3 * tq:4 * tq]
        hi, mid, lo = _split3(psum)
        selt = selt_ref[...]
        imp_t = _dot(selt, hi) + _dot(selt, mid) + _dot(selt, lo)
        j_io = lax.broadcasted_iota(jnp.int32, imp_t.shape, 0)
        cur = (qs + lax.broadcasted_iota(jnp.int32, imp_t.shape, 1)) // SLC_BLOCK
        forced = jnp.logical_or(j_io == 0, j_io == cur)
        score_t = jnp.where(j_io <= cur, jnp.where(forced, FORCED_BLOCK_SCORE, imp_t), -jnp.inf)
        bias_t = jnp.where(_topk_mask_t(score_t, N_SEL), 0.0, NEG_BIG).astype(BF16)
        qext_t = jnp.concatenate([q4t, jnp.concatenate([bias_t] * NSA_GROUP, axis=1)], axis=0)
        per_kvh.append((qext_t, o_cmp_t, sw))

    def tile(t):
        return pl.ds(t * NSA_TK if isinstance(t, int) else pl.multiple_of(t * NSA_TK, NSA_TK), NSA_TK)

    n_chunks = NSA_KV_HEADS * SWEEP_CHUNKS
    cc = rows // SWEEP_CHUNKS
    qext_all = jnp.concatenate([x[0] for x in per_kvh], axis=1)

    def sel_scores(t, c):
        return _dot(jnp.concatenate([kslc_ref[tile(t), :], eall_ref[tile(t), :]], axis=1),
                    qext_all[:, c * cc:(c + 1) * cc])

    def sel_values(t, p):
        return _dot(vslc_ref[:, tile(t)], p)

    def sel_mask(t, c):
        kpos = t * NSA_TK + lax.broadcasted_iota(jnp.int32, (NSA_TK, 1), 0)
        return kpos <= qs + (c * cc + lax.broadcasted_iota(jnp.int32, (1, cc), 1)) % tq

    l_sel, acc_sel = _causal_sweep_t(sel_scores, sel_values, sel_mask, qs // NSA_TK, NSA_KV_HEADS * rows, n_chunks)
    o_sel_all = acc_sel / l_sel
    for kvh, (_, o_cmp_t, sw) in enumerate(per_kvh):
        o_sel_t = o_sel_all[:, kvh * rows:(kvh + 1) * rows]
        dist = qpos - (ws + lax.broadcasted_iota(jnp.int32, (wlen, 1), 0))
        ew, inv_w = _softmax_terms_t(sw, jnp.logical_and(dist >= 0, dist < NSA_WINDOW))
        o_win_t = _dot(vwin_ref[:, pl.ds(ws, wlen)], ew.astype(BF16)) * inv_w
        for g in range(NSA_GROUP):
            h = NSA_GROUP * kvh + g
            cs = slice(g * tq, (g + 1) * tq)
            mixed = (gates_t[3 * h:3 * h + 1, :] * o_cmp_t[:, cs] + gates_t[3 * h + 1:3 * h + 2, :] * o_sel_t[:, cs]
                     + gates_t[3 * h + 2:3 * h + 3, :] * o_win_t[:, cs])
            head_out.append(mixed[kvh * HEAD_DIM:(kvh + 1) * HEAD_DIM, :])
    for pair in range(NSA_HEADS // 2):
        slab_t = jnp.concatenate([head_out[2 * pair], head_out[2 * pair + 1]], axis=0)
        o_ref[:, pair * LANES:(pair + 1) * LANES] = slab_t.T.astype(o_ref.dtype)


def nsa_attend_prompt(q_nsa_t, gate_t, kcmp16, vcmp_t, kslc16, vslc_t, kwin16, vwin_t, eall, selt):
    b, _, t = q_nsa_t.shape
    ncmp = kcmp16.shape[1]
    rows = lambda n: pl.BlockSpec((None, n, LANES), lambda i, j: (i, 0, 0))
    cols = lambda n: pl.BlockSpec((None, LANES, n), lambda i, j: (i, 0, 0))
    return pl.pallas_call(
        _nsa_prompt_kernel,
        grid=(b, t // NSA_TQ),
        in_specs=[pl.BlockSpec((None, NSA_HEADS * LANES, NSA_TQ), lambda i, j: (i, 0, j)),
                  pl.BlockSpec((None, LANES, NSA_TQ), lambda i, j: (i, 0, j)),
                  rows(ncmp), cols(ncmp), rows(t), cols(t), rows(t), cols(t),
                  _const_spec(eall.shape), _const_spec(selt.shape)],
        out_specs=pl.BlockSpec((None, NSA_TQ, NSA_HEADS * HEAD_DIM), lambda i, j: (i, j, 0)),
        out_shape=jax.ShapeDtypeStruct((b, t, NSA_HEADS * HEAD_DIM), BF16),
        compiler_params=_cparams("parallel", "parallel"),
        name="nsa_attend_prompt",
    )(q_nsa_t, gate_t, kcmp16, vcmp_t, kslc16, vslc_t, kwin16, vwin_t, eall, selt)


DIFF_TQ = 512
DIFF_TK = 512


def _diff_lambda(lam_ref, lam_init):
    lv = lam_ref[...]
    a = jnp.sum(lv[0:1] * lv[1:2], axis=-1, keepdims=True)
    b = jnp.sum(lv[2:3] * lv[3:4], axis=-1, keepdims=True)
    return jnp.exp(a) - jnp.exp(b) + lam_init


def _diff_prompt_kernel(qt_ref, k_ref, vt_ref, lam_ref, ng_ref, o_ref, *, lam_init):
    tq = DIFF_TQ
    qs = pl.program_id(2) * tq
    q2t = jnp.concatenate([qt_ref[0:LANES, :], qt_ref[LANES:2 * LANES, :]], axis=1)
    cc = 2 * tq // SWEEP_CHUNKS

    def tile(t):
        return pl.ds(t * DIFF_TK if isinstance(t, int) else pl.multiple_of(t * DIFF_TK, DIFF_TK), DIFF_TK)

    def scores(t, c):
        return _dot(k_ref[tile(t), :], q2t[:, c * cc:(c + 1) * cc])

    def values(t, p):
        return _dot(vt_ref[:, tile(t)], p)

    def mask(t, c):
        kpos = t * DIFF_TK + lax.broadcasted_iota(jnp.int32, (DIFF_TK, 1), 0)
        qpos = qs + (c * cc + lax.broadcasted_iota(jnp.int32, (1, cc), 1)) % tq
        return kpos <= qpos

    l, acc = _causal_sweep_t(scores, values, mask, qs // DIFF_TK, 2 * tq)
    ot = acc / l
    lam = _diff_lambda(lam_ref, lam_init)
    ot = ot[:, 0:tq] - lam * ot[:, tq:2 * tq]
    o = ot.T
    o_ref[...] = (_rms(o, ng_ref[...], DIFF_NORM_EPS) * (1.0 - lam_init)).astype(o_ref.dtype)


def diff_attend_prompt(q_diff_t, k_diff16, v_diff_t, lam_vec, norm_g, lam_init):
    b, _, t = q_diff_t.shape
    return pl.pallas_call(
        functools.partial(_diff_prompt_kernel, lam_init=lam_init),
        grid=(b, DIFF_HEADS, t // DIFF_TQ),
        in_specs=[pl.BlockSpec((None, 2 * LANES, DIFF_TQ), lambda i, h, j: (i, h, j)),
                  pl.BlockSpec((None, t, LANES), lambda i, h, j: (i, 0, h)),
                  pl.BlockSpec((None, LANES, t), lambda i, h, j: (i, h, 0)),
                  _const_spec(lam_vec.shape), _const_spec((1, LANES))],
        out_specs=pl.BlockSpec((None, DIFF_TQ, LANES), lambda i, h, j: (i, j, h)),
        out_shape=jax.ShapeDtypeStruct((b, t, DIFF_HEADS * LANES), BF16),
        compiler_params=_cparams("parallel", "parallel", "parallel"),
        name="diff_attend_prompt",
    )(q_diff_t, k_diff16, v_diff_t, lam_vec, norm_g.reshape(1, LANES))


def _dil_prompt_kernel(q_ref, kvp_ref, kvc_ref, o_ref, lse_ref):
    band = DIL_BAND
    hw = DIL_HEADS * HEAD_DIM
    first = pl.program_id(2) == 0
    qi = lax.broadcasted_iota(jnp.int32, (band, 2 * band), 0) + band
    kj = lax.broadcasted_iota(jnp.int32, (band, 2 * band), 1)
    rel = qi - kj
    ok = jnp.logical_and(rel >= 0, rel <= band)
    ok = jnp.logical_and(ok, jnp.logical_not(jnp.logical_and(first, kj < band)))
    lane = lax.broadcasted_iota(jnp.int32, (band, LANES), 1)
    scores = []
    for h in range(DIL_HEADS):
        cs = slice((h // 2) * LANES, (h // 2 + 1) * LANES)
        k2 = jnp.concatenate([kvp_ref[:, cs], kvc_ref[:, cs]], axis=0)
        scores.append(_dot_nt(q_ref[:, h * LANES:(h + 1) * LANES], k2))
    for pair in range(DIL_HEADS // 2):
        cs = slice(pair * LANES, (pair + 1) * LANES)
        vs = slice(hw + pair * LANES, hw + (pair + 1) * LANES)
        v2 = jnp.concatenate([kvp_ref[:, vs], kvc_ref[:, vs]], axis=0)
        outs, lses = [], []
        for hh in range(2):
            s = jnp.where(ok, scores[2 * pair + hh], -jnp.inf)
            m = jnp.max(s, axis=-1, keepdims=True)
            e = jnp.exp2(s - m)
            den = jnp.sum(e, axis=-1, keepdims=True)
            outs.append(_dot(e.astype(BF16), v2) * (1.0 / den))
            lses.append(jnp.log(den) + m * LN2)
        o_ref[:, cs] = jnp.where(lane < HEAD_DIM, outs[0], outs[1])
        lse_ref[:, cs] = jnp.where(lane < HEAD_DIM, lses[0], lses[1])


def dil_attend_prompt(q_r, kv_r):
    b, dil, n, _ = q_r.shape
    hw = DIL_HEADS * HEAD_DIM
    blk = lambda w, prev: pl.BlockSpec(
        (None, None, DIL_BAND, w), (lambda i, r, u: (i, r, jnp.maximum(u - 1, 0), 0)) if prev
        else (lambda i, r, u: (i, r, u, 0)))
    return pl.pallas_call(
        _dil_prompt_kernel,
        grid=(b, dil, n // DIL_BAND),
        in_specs=[blk(DIL_HEADS * LANES, False), blk(2 * hw, True), blk(2 * hw, False)],
        out_specs=[blk(hw, False), blk(hw, False)],
        out_shape=[jax.ShapeDtypeStruct((b, dil, n, hw), F32)] * 2,
        compiler_params=_cparams("parallel", "parallel", "parallel"),
        name="dil_attend_prompt",
    )(q_r, kv_r, kv_r)


TRANSPOSE_UNROLL = 8


def _page_copies(cache_ref, pt_ref, buf_ref, sem_ref, bi, slot, n_pages):
    return [pltpu.make_async_copy(cache_ref.at[pt_ref[bi, j], pl.ds(0, 2 * LANES), :],
                                  buf_ref.at[slot, j], sem_ref.at[slot]) for j in range(n_pages)]


def _compress_sample_kernel(pt_ref, cache_ref, pe_ref, wa_ref, wb_ref, o_ref, page_ref, kbuf_ref, vbuf_ref,
                            sem_ref, *, n_pages):
    i = pl.program_id(0)
    slot = i % 2
    nsub = n_pages * PAGE_SIZE // CMP_STRIDE
    copies = functools.partial(_page_copies, cache_ref, pt_ref, page_ref, sem_ref, n_pages=n_pages)

    @pl.when(i == 0)
    def _():
        for cp in copies(0, 0):
            cp.start()

    @pl.when(i + 1 < pl.num_programs(0))
    def _():
        for cp in copies(i + 1, 1 - slot):
            cp.start()

    for cp in copies(i, slot):
        cp.wait()

    def to_rows(jj, carry):
        for u in range(TRANSPOSE_UNROLL):
            j = jj * TRANSPOSE_UNROLL + u
            r0 = pl.multiple_of(j * PAGE_SIZE, PAGE_SIZE)
            kbuf_ref[pl.ds(r0, PAGE_SIZE), :] = page_ref[slot, j, 0:LANES, :].T
            vbuf_ref[pl.ds(r0, PAGE_SIZE), :] = page_ref[slot, j, LANES:2 * LANES, :].T
        return carry

    lax.fori_loop(0, n_pages // TRANSPOSE_UNROLL, to_rows, 0)
    o_ref[...] = _compress_rows(kbuf_ref, vbuf_ref, pe_ref, wa_ref, wb_ref, nsub).astype(o_ref.dtype)


def nsa_compress_sample(cache_t, page_table, pe4, wa, wb):
    bs, n_pages = page_table.shape
    past = n_pages * PAGE_SIZE
    nsub = past // CMP_STRIDE
    grid_spec = pltpu.PrefetchScalarGridSpec(
        num_scalar_prefetch=1,
        grid=(bs,),
        in_specs=[pl.BlockSpec(memory_space=pl.ANY),
                  pl.BlockSpec(pe4.shape, lambda i, pt: (0, 0)),
                  pl.BlockSpec(wa.shape, lambda i, pt: (0, 0, 0)),
                  pl.BlockSpec(wb.shape, lambda i, pt: (0, 0, 0))],
        out_specs=pl.BlockSpec((None, nsub, 2 * LANES), lambda i, pt: (i, 0, 0)),
        scratch_shapes=[pltpu.VMEM((2, n_pages, 2 * LANES, PAGE_SIZE), F32),
                        pltpu.VMEM((past, LANES), F32), pltpu.VMEM((past, LANES), F32),
                        pltpu.SemaphoreType.DMA((2,))],
    )
    return pl.pallas_call(
        functools.partial(_compress_sample_kernel, n_pages=n_pages),
        grid_spec=grid_spec,
        out_shape=jax.ShapeDtypeStruct((bs, nsub, 2 * LANES), BF16),
        compiler_params=_cparams("arbitrary"),
        name="nsa_compress_sample",
    )(page_table, cache_t, pe4, wa, wb)


def _group_sum_rows(x):
    parts = [jnp.sum(x[NSA_GROUP * k:NSA_GROUP * (k + 1)], axis=0, keepdims=True) for k in range(NSA_KV_HEADS)]
    return _pad_rows(jnp.concatenate(parts, axis=0), x.shape[0])


def _nsa_sample_cmp_kernel(q_ref, cmp_ref, selt_ref, ocmp_ref, imp_ref, *, qpos):
    q8 = q_ref[...]
    ncmp = cmp_ref.shape[0]
    s = _dot_nt(q8, cmp_ref[:, 0:LANES])
    cmp_end = lax.broadcasted_iota(jnp.int32, (1, ncmp), 1) * CMP_STRIDE + (CMP_LEN - 1)
    p = _softmax_rows(s, cmp_end <= qpos)
    ocmp_ref[...] = _dot(p.astype(BF16), cmp_ref[:, LANES:2 * LANES])
    hi, mid, lo = _split3(_group_sum_rows(p))
    selt = selt_ref[...]
    imp_ref[...] = _dot_nt(hi, selt) + _dot_nt(mid, selt) + _dot_nt(lo, selt)


def nsa_sample_cmp(q8, cmp_kv, selt, qpos):
    bs = q8.shape[0]
    ncmp = cmp_kv.shape[1]
    blk = pl.BlockSpec((None, NSA_HEADS, LANES), lambda i: (i, 0, 0))
    return pl.pallas_call(
        functools.partial(_nsa_sample_cmp_kernel, qpos=qpos),
        grid=(bs,),
        in_specs=[blk, pl.BlockSpec((None, ncmp, 2 * LANES), lambda i: (i, 0, 0)), _const_spec(selt.shape)],
        out_specs=[blk, blk],
        out_shape=[jax.ShapeDtypeStruct((bs, NSA_HEADS, LANES), F32)] * 2,
        compiler_params=_cparams("parallel"),
        name="nsa_sample_cmp",
    )(q8, cmp_kv, selt)


def _topk_lanes_kernel(imp_ref, idx_ref, *, k):
    score = imp_ref[...]
    lane = lax.broadcasted_iota(jnp.int32, score.shape, 1)
    lane_f = lane.astype(F32)
    score = jnp.where(lane == 0, FORCED_BLOCK_SCORE, score)
    idx = jnp.zeros(score.shape, F32)
    for r in range(k):
        m = jnp.max(score, axis=-1, keepdims=True)
        jmin = jnp.min(jnp.where(score == m, lane_f, float(LANES)), axis=-1, keepdims=True)
        idx = jnp.where(lane == r, jmin, idx)
        score = jnp.where(lane_f == jmin, -jnp.inf, score)
    idx_ref[...] = idx.astype(jnp.int32)


def topk_lanes(imp, k):
    bs = imp.shape[0]
    x = imp.reshape(bs * NSA_HEADS, LANES)
    out = pl.pallas_call(
        functools.partial(_topk_lanes_kernel, k=k),
        out_shape=jax.ShapeDtypeStruct(x.shape, jnp.int32),
        name="topk_lanes",
    )(x)
    return out.reshape(bs, NSA_HEADS, LANES)


N_SEL_CACHE = N_SEL - 1


def _sel_copies(cache_ref, pt_ref, sel_ref, buf_ref, sem_ref, bi, slot):
    cps = []
    for kvh in range(NSA_KV_HEADS):
        for r in range(N_SEL_CACHE):
            j = sel_ref[bi, kvh * N_SEL_CACHE + r]
            cps.append(pltpu.make_async_copy(
                cache_ref.at[pt_ref[bi, j // 2], pl.ds(2 * LANES, 2 * LANES), :],
                buf_ref.at[slot, kvh, r], sem_ref.at[slot]))
    return cps


def _pick_gate(gates8, branch):
    row = lax.broadcasted_iota(jnp.int32, gates8.shape, 0)
    lane = lax.broadcasted_iota(jnp.int32, gates8.shape, 1)
    return jnp.sum(jnp.where(lane == 3 * row + branch, gates8, 0.0), axis=-1, keepdims=True)


def _nsa_sample_attend_kernel(pt_ref, sel_ref, q_ref, gate_ref, ocmp_ref, new_ref, win_ref, cache_ref,
                              o_ref, buf_ref, sem_ref):
    i = pl.program_id(0)
    slot = i % 2
    copies = functools.partial(_sel_copies, cache_ref, pt_ref, sel_ref, buf_ref, sem_ref)

    @pl.when(i == 0)
    def _():
        for cp in copies(0, 0):
            cp.start()

    @pl.when(i + 1 < pl.num_programs(0))
    def _():
        for cp in copies(i + 1, 1 - slot):
            cp.start()

    q8 = q_ref[...]
    q8f = q8.astype(F32)
    row = lax.broadcasted_iota(jnp.int32, (NSA_HEADS, 1), 0)
    new = new_ref[...]
    rnd = lambda x: x.astype(BF16).astype(F32)

    def probs_with_new_key(s, k_new, mask):
        s_new = jnp.sum(q8f * rnd(k_new), axis=-1, keepdims=True)
        s = jnp.where(mask, s, -jnp.inf)
        m = jnp.maximum(jnp.max(s, axis=-1, keepdims=True), s_new)
        e = jnp.exp(s - m)
        e_new = jnp.exp(s_new - m)
        den = jnp.sum(e, axis=-1, keepdims=True) + e_new
        return (e / den).astype(BF16), rnd(e_new / den)

    wb = win_ref.shape[1]
    widx = lax.broadcasted_iota(jnp.int32, (1, wb), 1)
    pw, pw_new = probs_with_new_key(_dot(q8, win_ref[0:LANES, :].astype(BF16)), new[:, 4 * LANES:5 * LANES],
                                    widx > wb - NSA_WINDOW)
    o_win = _dot_nt(pw, win_ref[LANES:2 * LANES, :].astype(BF16)) + pw_new * rnd(new[:, 5 * LANES:6 * LANES])
    for cp in copies(i, slot):
        cp.wait()
    lane = lax.broadcasted_iota(jnp.int32, (1, PAGE_SIZE), 1)
    o_sel = []
    for kvh in range(NSA_KV_HEADS):
        ss, masks = [], []
        for r in range(N_SEL_CACHE):
            ss.append(_dot(q8, buf_ref[slot, kvh, r, 0:LANES, :].astype(BF16)))
            masks.append(lane // SLC_BLOCK == sel_ref[i, kvh * N_SEL_CACHE + r] % 2)
        p, p_new = probs_with_new_key(jnp.concatenate(ss, axis=1), new[:, 2 * LANES:3 * LANES],
                                      jnp.concatenate(masks, axis=1))
        o = p_new * rnd(new[:, 3 * LANES:4 * LANES])
        for r in range(N_SEL_CACHE):
            o = o + _dot_nt(p[:, r * PAGE_SIZE:(r + 1) * PAGE_SIZE],
                            buf_ref[slot, kvh, r, LANES:2 * LANES, :].astype(BF16))
        o_sel.append(o)
    o_sel = jnp.where(row < NSA_GROUP, o_sel[0], o_sel[1])
    gates8 = jnp.broadcast_to(jax.nn.sigmoid(gate_ref[...]), (NSA_HEADS, LANES))
    o_ref[...] = (_pick_gate(gates8, 0) * ocmp_ref[...] + _pick_gate(gates8, 1) * o_sel
                  + _pick_gate(gates8, 2) * o_win)


def nsa_sample_attend(page_table, sel_idx, q8, gate, o_cmp, new_rows, win_t, cache_t):
    bs = q8.shape[0]
    wb = win_t.shape[2]
    blk = lambda w: pl.BlockSpec((None, NSA_HEADS, w), lambda i, pt, sel: (i, 0, 0))
    one = lambda w: pl.BlockSpec((None, 1, w), lambda i, pt, sel: (i, 0, 0))
    grid_spec = pltpu.PrefetchScalarGridSpec(
        num_scalar_prefetch=2,
        grid=(bs,),
        in_specs=[blk(LANES), one(LANES), blk(LANES), one(new_rows.shape[-1]),
                  pl.BlockSpec((None, 2 * LANES, wb), lambda i, pt, sel: (i, 0, 0)),
                  pl.BlockSpec(memory_space=pl.ANY)],
        out_specs=blk(LANES),
        scratch_shapes=[pltpu.VMEM((2, NSA_KV_HEADS, N_SEL_CACHE, 2 * LANES, PAGE_SIZE), F32),
                        pltpu.SemaphoreType.DMA((2,))],
    )
    return pl.pallas_call(
        _nsa_sample_attend_kernel,
        grid_spec=grid_spec,
        out_shape=jax.ShapeDtypeStruct((bs, NSA_HEADS, LANES), F32),
        compiler_params=_cparams("arbitrary"),
        name="nsa_sample_attend",
    )(page_table, sel_idx, q8, gate, o_cmp, new_rows, win_t, cache_t)


DIFF_PAGES_PER_STEP = 16
DIFF_ROW_STRIDE = 2 * DIFF_HEADS


def _diff_sample_kernel(pt_ref, qt_ref, new_ref, lam_ref, ng_ref, *rest, lam_init):
    pages = rest[:DIFF_PAGES_PER_STEP]
    o_ref, m_ref, l_ref, acc_ref = rest[DIFF_PAGES_PER_STEP:]
    c = pl.program_id(1)
    hw = DIFF_HEADS * LANES
    qt = qt_ref[...]
    rnd = lambda x: x.astype(BF16).astype(F32)

    def heads(page, which):
        return jnp.concatenate(
            [page[pl.ds(which * DIFF_HEADS + h, PAGE_SIZE, stride=DIFF_ROW_STRIDE), :] for h in range(DIFF_HEADS)],
            axis=1).astype(BF16)

    @pl.when(c == 0)
    def _():
        s_new = jnp.sum(qt.astype(F32) * rnd(new_ref[:, 0:hw]), axis=-1, keepdims=True)
        m_ref[...] = jnp.broadcast_to(s_new, m_ref.shape)
        l_ref[...] = jnp.ones(l_ref.shape, F32)
        acc_ref[...] = jnp.broadcast_to(rnd(new_ref[:, hw:2 * hw]), acc_ref.shape)

    s = jnp.concatenate([_dot_nt(qt, heads(page, 0)) for page in pages], axis=1)
    m_old = m_ref[:, 0:1]
    m_new = jnp.maximum(m_old, jnp.max(s, axis=-1, keepdims=True))
    alpha = jnp.exp(m_old - m_new)
    e = jnp.exp(s - m_new)
    l = l_ref[:, 0:1] * alpha + jnp.sum(e, axis=-1, keepdims=True)
    acc = acc_ref[...] * alpha
    for i, page in enumerate(pages):
        acc = acc + _dot(e[:, i * PAGE_SIZE:(i + 1) * PAGE_SIZE].astype(BF16), heads(page, 1))
    m_ref[...] = jnp.broadcast_to(m_new, m_ref.shape)
    l_ref[...] = jnp.broadcast_to(l, l_ref.shape)
    acc_ref[...] = acc

    @pl.when(c == pl.num_programs(1) - 1)
    def _():
        o = acc / l
        lam = _diff_lambda(lam_ref, lam_init)
        for h in range(DIFF_HEADS):
            cs = slice(h * LANES, (h + 1) * LANES)
            oh = o[h:h + 1, cs] - lam * o[DIFF_HEADS + h:DIFF_HEADS + h + 1, cs]
            o_ref[:, cs] = _rms(oh, ng_ref[...], DIFF_NORM_EPS) * (1.0 - lam_init)


def diff_attend_sample(page_table, qt, new_rows, lam_vec, norm_g, cache_v, lam_init):
    bs, n_pages = page_table.shape
    p = DIFF_PAGES_PER_STEP
    hw = DIFF_HEADS * LANES
    page_specs = [pl.BlockSpec((None, PAGE_SIZE * DIFF_ROW_STRIDE, LANES),
                               functools.partial(lambda i, c, pt, k: (pt[i, c * p + k], 0, 0), k=k))
                  for k in range(p)]
    grid_spec = pltpu.PrefetchScalarGridSpec(
        num_scalar_prefetch=1,
        grid=(bs, n_pages // p),
        in_specs=[pl.BlockSpec((None, 2 * DIFF_HEADS, hw), lambda i, c, pt: (i, 0, 0)),
                  pl.BlockSpec((None, 1, 2 * hw), lambda i, c, pt: (i, 0, 0)),
                  pl.BlockSpec(lam_vec.shape, lambda i, c, pt: (0, 0)),
                  pl.BlockSpec((1, LANES), lambda i, c, pt: (0, 0))] + page_specs,
        out_specs=pl.BlockSpec((None, 1, hw), lambda i, c, pt: (i, 0, 0)),
        scratch_shapes=[pltpu.VMEM((2 * DIFF_HEADS, LANES), F32), pltpu.VMEM((2 * DIFF_HEADS, LANES), F32),
                        pltpu.VMEM((2 * DIFF_HEADS, hw), F32)],
    )
    return pl.pallas_call(
        functools.partial(_diff_sample_kernel, lam_init=lam_init),
        grid_spec=grid_spec,
        out_shape=jax.ShapeDtypeStruct((bs, 1, hw), F32),
        compiler_params=_cparams("parallel", "arbitrary"),
        name="diff_attend_sample",
    )(page_table, qt, new_rows, lam_vec, norm_g.reshape(1, LANES), *([cache_v] * p))


def _col_rep(row):
    x = jnp.broadcast_to(row, (LANES, row.shape[1]))
    return jnp.concatenate([x[:, c * LANES:(c + 1) * LANES].T for c in range(row.shape[1] // LANES)], axis=0)


def _head_sum(x):
    return jnp.sum(x.reshape(DIL_HEADS, HEAD_DIM, x.shape[1]), axis=1)


def _head_expand(x):
    return jnp.broadcast_to(x[:, None, :], (DIL_HEADS, HEAD_DIM, x.shape[1])).reshape(
        DIL_HEADS * HEAD_DIM, x.shape[1])


STATE_SHIFT_ROWS = 256


def _shift_in(st_ref, new_col, out_ref):
    n_rows, width = st_ref.shape
    lane = lax.broadcasted_iota(jnp.int32, (STATE_SHIFT_ROWS, LANES), 1)
    for r0 in range(0, n_rows, STATE_SHIFT_ROWS):
        rs = slice(r0, r0 + STATE_SHIFT_ROWS)
        rolled = pltpu.roll(st_ref[rs, :], width - 1, 1)
        if width > LANES:
            out_ref[rs, 0:width - LANES] = rolled[:, 0:width - LANES]
        out_ref[rs, width - LANES:width] = jnp.where(lane == LANES - 1, new_col[rs, :], rolled[:, width - LANES:width])


def _dil_sample_kernel(q_ref, new0_ref, new1_ref, new2_ref, st0_ref, st1_ref, st2_ref, o_ref,
                       nst0_ref, nst1_ref, nst2_ref):
    hw = DIL_HEADS * HEAD_DIM
    outs, lses = [], []
    for g, (new_ref, st_ref, nst_ref) in enumerate(((new0_ref, st0_ref, nst0_ref), (new1_ref, st1_ref, nst1_ref),
                                                    (new2_ref, st2_ref, nst2_ref))):
        win, dil = DIL_GROUPS[g]
        qc = _col_rep(q_ref[:, g * hw:(g + 1) * hw])
        kn = _col_rep(new_ref[:, 0:hw])
        vn = _col_rep(new_ref[:, hw:2 * hw])
        _shift_in(st_ref, jnp.concatenate([kn, vn], axis=0), nst_ref)
        s_new = _head_sum(qc * kn)[:, 0:1]
        n_chunks = win // LANES
        s = jnp.concatenate([_head_sum(st_ref[0:hw, c * LANES:(c + 1) * LANES] * qc) for c in range(n_chunks)],
                            axis=1)
        lane = lax.broadcasted_iota(jnp.int32, s.shape, 1)
        s = jnp.where(lane % dil == 0, s, -jnp.inf)
        m = jnp.maximum(jnp.max(s, axis=-1, keepdims=True), s_new)
        e = jnp.exp(s - m)
        e_new = jnp.exp(s_new - m)
        den = jnp.sum(e, axis=-1, keepdims=True) + e_new
        p = e / den
        acc = _head_expand(jnp.broadcast_to(e_new / den, (DIL_HEADS, LANES))) * vn * (1.0 / LANES)
        for c in range(n_chunks):
            cs = slice(c * LANES, (c + 1) * LANES)
            acc = acc + st_ref[hw:2 * hw, cs] * _head_expand(p[:, cs])
        outs.append(jnp.sum(acc, axis=-1, keepdims=True))
        lses.append(jnp.log(den) + m)
    mx = jnp.maximum(jnp.maximum(lses[0], lses[1]), lses[2])
    es = [jnp.exp(l - mx) for l in lses]
    tot = es[0] + es[1] + es[2]
    mix = jnp.zeros((hw, LANES), F32)
    for g in range(len(DIL_GROUPS)):
        alpha = _head_expand(jnp.broadcast_to(es[g] / tot, (DIL_HEADS, LANES)))
        mix = mix + alpha * jnp.broadcast_to(outs[g], (hw, LANES))
    rows = jnp.concatenate([mix[c * LANES:(c + 1) * LANES, :].T for c in range(hw // LANES)], axis=1)
    o_ref[...] = rows[0:1]


def dil_attend_sample(q, news, states_t):
    bs = q.shape[0]
    hw = DIL_HEADS * HEAD_DIM
    st_specs = [pl.BlockSpec((None, 2 * hw, st.shape[2]), lambda i: (i, 0, 0)) for st in states_t]
    outs = pl.pallas_call(
        _dil_sample_kernel,
        grid=(bs,),
        in_specs=[pl.BlockSpec((None, 1, q.shape[2]), lambda i: (i, 0, 0))]
                 + [pl.BlockSpec((None, 1, 2 * hw), lambda i: (i, 0, 0))] * 3 + st_specs,
        out_specs=[pl.BlockSpec((None, 1, hw), lambda i: (i, 0, 0))] + st_specs,
        out_shape=[jax.ShapeDtypeStruct((bs, 1, hw), F32)]
                  + [jax.ShapeDtypeStruct(st.shape, F32) for st in states_t],
        compiler_params=_cparams("parallel"),
        name="dil_attend_sample",
    )(q, *news, *states_t)
    return outs[0], outs[1:]


AB_SIZES = (NSA_HEADS * HEAD_DIM, 6 * NSA_KV_HEADS * HEAD_DIM, 3 * NSA_HEADS,
            DIFF_HEADS * 2 * HEAD_DIM, DIFF_HEADS * 2 * HEAD_DIM, DIFF_HEADS * 2 * HEAD_DIM)
_QSCALE = HEAD_DIM ** -0.5
_QSCALE2 = _QSCALE * LOG2E


def _rope_tables(pos):
    half = HEAD_DIM // 2
    inv = ROPE_THETA ** (-jnp.arange(half, dtype=F32) / half)
    ang = pos.astype(F32)[:, None] * inv[None, :]
    c, s = jnp.cos(ang), jnp.sin(ang)
    return jnp.tile(c, (1, 4)), jnp.tile(jnp.concatenate([-s, s], axis=1), (1, 2))


def _prep_w_ab(w):
    d = w.shape[0]
    qa, kvb, gl, qd, kd, vd = jnp.split(w, np.cumsum(AB_SIZES)[:-1].tolist(), axis=1)
    kvb = kvb.reshape(d, 6, LANES)
    k3 = kvb[:, 0::2].reshape(d, 3 * LANES)
    v3 = kvb[:, 1::2].reshape(d, 3 * LANES)
    gl = jnp.pad(gl, ((0, 0), (0, LANES - gl.shape[1])))
    return jnp.concatenate([qa, k3, v3, qd, kd, vd, gl], axis=1).astype(BF16)


def _ab_plan(dest, qscale):
    return (
        (0, 4, True, qscale, "half", tuple(dest("qa", j) for j in range(8))),
        (512, 3, True, 1.0, None, (dest("k_cmp", 0), dest("k_slc", 0), dest("k_win", 0))),
        (896, 3, False, 1.0, None, (dest("v_cmp", 0), dest("v_slc", 0), dest("v_win", 0))),
        (1280, 4, True, qscale, "pair", tuple(dest("qd", j) for j in range(8))),
        (1792, 4, True, 1.0, None, tuple(dest("kd", j) for j in range(4))),
        (2304, 4, False, 1.0, None, tuple(dest("vd", j) for j in range(4))),
        (2816, 1, False, 1.0, None, (dest("gate", 0),)),
    )


def _ab_prompt_defs(t):
    defs = (("T", 1024, BF16, t), ("T", 512, F32, t), ("N", 128, BF16), ("N", 256, F32),
            ("T", 256, F32, min(NSA_WINDOW, t)), ("N", 128, BF16), ("T", 1024, BF16, t), ("I", 8, F32),
            ("N", 512, BF16), ("T", 128, F32, t), ("T", 128, BF16, t), ("T", 128, BF16, t), ("T", 512, BF16, t))
    table = {
        "qa": lambda j: ((0, j),), "qd": lambda j: ((6, j),), "gate": lambda j: ((9, 0),),
        "k_cmp": lambda j: ((1, 0), (3, 0)), "v_cmp": lambda j: ((1, 1), (3, 1)),
        "k_slc": lambda j: ((1, 2), (2, 0)), "v_slc": lambda j: ((1, 3), (11, 0)),
        "k_win": lambda j: ((4, 0), (5, 0)), "v_win": lambda j: ((4, 1), (10, 0)),
        "kd": lambda j: ((7, j), (8, j)), "vd": lambda j: ((7, 4 + j), (12, j)),
    }
    return defs, _ab_plan(lambda name, j: table[name](j), _QSCALE2)


def _ab_sample_defs():
    defs = (("N", 1024, BF16), ("N", 512, F32), ("N", 256, F32), ("N", 1024, BF16), ("N", 1024, F32),
            ("N", 128, F32))
    table = {
        "qa": lambda j: ((0, j),), "qd": lambda j: ((3, j),), "gate": lambda j: ((5, 0),),
        "k_cmp": lambda j: ((1, 0),), "v_cmp": lambda j: ((1, 1),),
        "k_slc": lambda j: ((1, 2),), "v_slc": lambda j: ((1, 3),),
        "k_win": lambda j: ((2, 0),), "v_win": lambda j: ((2, 1),),
        "kd": lambda j: ((4, j),), "vd": lambda j: ((4, 4 + j),),
    }
    return defs, _ab_plan(lambda name, j: table[name](j), _QSCALE)


def _c_prompt_defs(t):
    defs, plan = [], []
    for g, (win, dil) in enumerate(DIL_GROUPS):
        defs += [("R", 1024, BF16, dil), ("R", 1024, BF16, dil), ("T", 1024, F32, min(win, t))]
        plan += [
            (g * 1536, 4, True, _QSCALE2, "pair", tuple(((3 * g, j),) for j in range(8))),
            (g * 1536 + 512, 4, True, 1.0, None, tuple(((3 * g + 1, j), (3 * g + 2, j)) for j in range(4))),
            (g * 1536 + 1024, 4, False, 1.0, None,
             tuple(((3 * g + 1, 4 + j), (3 * g + 2, 4 + j)) for j in range(4))),
        ]
    return tuple(defs), tuple(plan)


def _c_sample_defs():
    defs = (("N", 1536, F32), ("N", 1024, F32), ("N", 1024, F32), ("N", 1024, F32))
    plan = []
    for g in range(len(DIL_GROUPS)):
        plan += [
            (g * 1536, 4, True, _QSCALE, None, tuple(((0, 4 * g + j),) for j in range(4))),
            (g * 1536 + 512, 4, True, 1.0, None, tuple(((1 + g, j),) for j in range(4))),
            (g * 1536 + 1024, 4, False, 1.0, None, tuple(((1 + g, 4 + j),) for j in range(4))),
        ]
    return defs, tuple(plan)


def _prep_cmp(w_cmp, pe_cmp):
    wk, wv = w_cmp[0], w_cmp[1]
    z = jnp.zeros_like(wk)
    w4 = jnp.concatenate([jnp.concatenate([wk, z, z, z], axis=-1), jnp.concatenate([z, wk, z, z], axis=-1),
                          jnp.concatenate([z, z, wv, z], axis=-1), jnp.concatenate([z, z, z, wv], axis=-1)],
                         axis=1).astype(BF16)
    pe4 = jnp.concatenate([pe_cmp[0], pe_cmp[0], pe_cmp[1], pe_cmp[1]], axis=-1)
    half = CMP_LEN // 2
    return pe4, w4[:half], w4[half:]


def _block_indicator(n_keys):
    return (jnp.arange(n_keys)[:, None] // SLC_BLOCK == jnp.arange(LANES)[None, :]).astype(BF16)


def _cmp_to_block(n_cmp):
    r = SLC_BLOCK // CMP_STRIDE
    return (jnp.arange(n_cmp)[None, :] // r == jnp.arange(LANES)[:, None]).astype(BF16)


def _rows_from_t(x_t, lead):
    b, _, r = x_t.shape
    nd = len(lead)
    return x_t.reshape((b,) + tuple(lead) + (HEAD_DIM, r)).transpose((0, nd + 2) + tuple(range(1, nd + 2)))


def _rows_to_t(x):
    b, r = x.shape[:2]
    nd = x.ndim
    return x.transpose((0,) + tuple(range(2, nd)) + (1,)).reshape(b, -1, r)


def ab_mix_prompt(h, b, t, g_in, w_ab, cmp_prep, lam_vec, dn_g, lam_init, cos, sin):
    defs, plan = _ab_prompt_defs(t)
    (q_nsa_t, rows_nsa_t, kslc16, cmp_rows, rows_win_t, kwin16, q_diff_t, rows_diff, k_diff16, gate_t, vwin_t,
     vslc_t, v_diff_t) = project(h, b, g_in, cos, sin, w_ab, plan, defs)
    r3 = lambda x: x.reshape(b, t, x.shape[-1])
    pe4, wa, wb = cmp_prep
    kcmp16, vcmp_t = nsa_compress_prompt(r3(cmp_rows), pe4, wa, wb)
    o_nsa = nsa_attend_prompt(q_nsa_t, gate_t, kcmp16, vcmp_t, r3(kslc16), vslc_t, r3(kwin16), vwin_t,
                              _block_indicator(t), _cmp_to_block(t // CMP_STRIDE))
    o_diff = diff_attend_prompt(q_diff_t, r3(k_diff16), v_diff_t, lam_vec, dn_g, lam_init)
    mixed = [o_nsa.reshape(b * t, -1), o_diff.reshape(b * t, -1)]
    return mixed, rows_nsa_t, rows_win_t, rows_diff.reshape(b, t, 2, DIFF_HEADS, 2 * HEAD_DIM)


def dil_mix_prompt(h, b, t, g_in, w_c, cos, sin):
    defs, plan = _c_prompt_defs(t)
    outs = project(h, b, g_in, cos, sin, w_c, plan, defs)
    os_, lses, rows_t = [], [], []
    for gi in range(len(DIL_GROUPS)):
        o, lse = dil_attend_prompt(outs[3 * gi], outs[3 * gi + 1])
        os_.append(o)
        lses.append(lse)
        rows_t.append(outs[3 * gi + 2])
    return os_, lses, rows_t


def _diff_qt(q_diff):
    bs = q_diff.shape[0]
    qd = q_diff.reshape(bs, DIFF_HEADS, 2, LANES).transpose(0, 2, 1, 3)
    eye = jnp.eye(DIFF_HEADS, dtype=q_diff.dtype)
    return (qd[:, :, :, None, :] * eye[None, None, :, :, None]).reshape(bs, 2 * DIFF_HEADS, DIFF_HEADS * LANES)


def ab_mix_sample(hs, g_in, w_ab, cmp_prep, lam_vec, dn_g, lam_init, cos, sin,
                  cache_nsa, cache_diff, win_state, page_table):
    bs = hs.shape[0]
    defs, plan = _ab_sample_defs()
    q_nsa, rows_nsa, rows_win, q_diff, rows_diff, gate = project(hs, 1, g_in, cos, sin, w_ab, plan, defs)
    n_pages = page_table.shape[1]
    past = n_pages * PAGE_SIZE
    assert past // SLC_BLOCK == LANES, "selection-block axis is laid out on the 128 lanes"
    n_pool = cache_nsa.shape[0]
    cache_t = _rows_to_t(cache_nsa)
    cache_v = cache_diff.reshape(n_pool, PAGE_SIZE * DIFF_ROW_STRIDE, LANES)
    pe4, wa, wb = cmp_prep
    cmp_kv = nsa_compress_sample(cache_t, page_table, pe4, wa, wb)
    q8 = q_nsa.reshape(bs, NSA_HEADS, LANES)
    o_cmp, imp = nsa_sample_cmp(q8, cmp_kv, _cmp_to_block(past // CMP_STRIDE), past)
    sel_idx = topk_lanes(imp, N_SEL_CACHE)[:, :NSA_KV_HEADS, :N_SEL_CACHE].reshape(bs, -1)
    new_rows = jnp.concatenate([rows_nsa, rows_win], axis=-1).reshape(bs, 1, -1)
    o8 = nsa_sample_attend(page_table, sel_idx, q8, gate.reshape(bs, 1, LANES), o_cmp, new_rows,
                           _rows_to_t(win_state), cache_t)
    o8 = o8.reshape(bs, NSA_HEADS, 2, HEAD_DIM)
    o_nsa = jnp.concatenate([o8[:, :NSA_GROUP, 0], o8[:, NSA_GROUP:, 1]], axis=1).reshape(bs, -1)
    o_diff = diff_attend_sample(page_table, _diff_qt(q_diff), rows_diff.reshape(bs, 1, -1), lam_vec, dn_g,
                                cache_v, lam_init).reshape(bs, -1)
    mixed = jnp.concatenate([o_nsa, o_diff], axis=-1).astype(BF16)
    return mixed, rows_nsa, rows_win, rows_diff


def dil_mix_sample(hs, g_in, w_c_rows, cos, sin, states):
    bs = hs.shape[0]
    defs, plan = _c_sample_defs()
    outs = project(hs, 1, g_in, cos, sin, w_c_rows, plan, defs)
    news = [x.reshape(bs, 1, -1) for x in outs[1:]]
    sts = []
    for (win, dil), st in zip(DIL_GROUPS, states):
        assert st.shape[1] == win and win == DIL_BAND * dil, "state buffer must hold the full dilated window"
        sts.append(_rows_to_t(st))
    o, new_sts = dil_attend_sample(outs[0].reshape(bs, 1, -1), news, sts)
    return o.reshape(bs, -1).astype(BF16), [_rows_from_t(x, (2, DIL_HEADS)) for x in new_sts]


def kernel(x_prompt, x_sample, cache_nsa, cache_diff, state_nsa_win, state_dil_0, state_dil_1, state_dil_2,
           page_table, norm_g, ffn_w_in, ffn_w_out, w_in_ab, w_out_ab, nsa_w_cmp, nsa_pe_cmp, diff_lambda,
           diff_norm_g, w_in_c, w_out_c):
    b, t, d = x_prompt.shape
    bs, ns, _ = x_sample.shape
    assert ns == 1, "sample group is one new token per sequence"
    depth = norm_g.shape[0]
    past = page_table.shape[1] * PAGE_SIZE
    hp = x_prompt.reshape(b * t, d)
    hs = x_sample.reshape(bs, d)
    cos_p, sin_p = _rope_tables(jnp.tile(jnp.arange(t, dtype=jnp.int32), b))
    cos_s, sin_s = _rope_tables(jnp.full((bs,), past, jnp.int32))
    state_dil = (state_dil_0, state_dil_1, state_dil_2)
    nsa_p, nsa_s, win_p, win_s, diff_p, diff_s = [], [], [], [], [], []
    dil_p = [[] for _ in DIL_GROUPS]
    dil_s = [[] for _ in DIL_GROUPS]
    for layer in range(depth):
        g = norm_g[layer]
        hp = ffn_half(hp, g[0], g[1], ffn_w_in, ffn_w_out, layer, 0)
        hs = ffn_half(hs, g[0], g[1], ffn_w_in, ffn_w_out, layer, 0)
        if layer % 2 == 0:
            e = layer // 2
            lam_init = 0.8 - 0.6 * math.exp(-0.3 * layer)
            w_ab = _prep_w_ab(w_in_ab[e])
            cmp_prep = _prep_cmp(nsa_w_cmp[e], nsa_pe_cmp[e])
            mp, rn_t, rw_t, rd = ab_mix_prompt(hp, b, t, g[2], w_ab, cmp_prep, diff_lambda[e], diff_norm_g[e],
                                               lam_init, cos_p, sin_p)
            nsa_p.append(_rows_from_t(rn_t, (4, NSA_KV_HEADS)))
            win_p.append(_rows_from_t(rw_t[:, :, -min(NSA_WINDOW, t):], (2, NSA_KV_HEADS)))
            diff_p.append(rd.reshape(b, t, 2, DIFF_HEADS, 2 * HEAD_DIM))
            ms, rn, rw, rd = ab_mix_sample(hs, g[2], w_ab, cmp_prep, diff_lambda[e], diff_norm_g[e], lam_init,
                                           cos_s, sin_s, cache_nsa[e], cache_diff[e], state_nsa_win[e], page_table)
            nsa_s.append(rn.reshape(bs, 1, 4, NSA_KV_HEADS, HEAD_DIM))
            win_full = jnp.concatenate([state_nsa_win[e], rw.reshape(bs, 1, 2, NSA_KV_HEADS, HEAD_DIM)], axis=1)
            win_s.append(win_full[:, -min(NSA_WINDOW, win_full.shape[1]):])
            diff_s.append(rd.reshape(bs, 1, 2, DIFF_HEADS, 2 * HEAD_DIM))
            w_o = w_out_ab[e].astype(BF16)
            hp = outproj(hp, mp, w_o, g[3])
            hs = outproj(hs, [ms], w_o, g[3])
        else:
            o = layer // 2
            w_o = w_out_c[o].astype(BF16)
            w_c = w_in_c[o].astype(BF16)
            os_, lses, rows_t = dil_mix_prompt(hp, b, t, g[2], w_c, cos_p, sin_p)
            hp = outproj_dil(hp, b, os_, lses, w_o, g[3])
            ms, new_states = dil_mix_sample(hs, g[2], w_c, cos_s, sin_s, [st[o] for st in state_dil])
            hs = outproj(hs, [ms], w_o, g[3])
            for gi, (win, dil) in enumerate(DIL_GROUPS):
                dil_p[gi].append(_rows_from_t(rows_t[gi][:, :, -min(win, t):], (2, DIL_HEADS)))
                dil_s[gi].append(new_states[gi])
        hp = ffn_half(hp, g[4], g[5], ffn_w_in, ffn_w_out, layer, 1)
        hs = ffn_half(hs, g[4], g[5], ffn_w_in, ffn_w_out, layer, 1)
    return (hp.reshape(b, t, d), hs.reshape(bs, 1, d), jnp.stack(nsa_p), jnp.stack(nsa_s), jnp.stack(win_p),
            jnp.stack(win_s), jnp.stack(diff_p), jnp.stack(diff_s), jnp.stack(dil_p[0]), jnp.stack(dil_s[0]),
            jnp.stack(dil_p[1]), jnp.stack(dil_s[1]), jnp.stack(dil_p[2]), jnp.stack(dil_s[2]))
```
